```python
import math
import jax, jax.numpy as jnp
from jax import lax
import numpy as np

D_MODEL = 1024
BATCH = 4
SEQ = 8192
DEPTH = 4

CHUNK = 64
Q_BLOCK = 128
EPS = 1e-6
SB_HEAD_DIM = 64
SB_WIDTH = D_MODEL // 2
SB_HEADS = SB_WIDTH // SB_HEAD_DIM
CONV_WIDTH = D_MODEL // 2
CONV_KERNEL = 31
EVEN_IN = 3 * SB_WIDTH + 2 * CONV_WIDTH
EVEN_MIX = SB_WIDTH + CONV_WIDTH
DIFF_HEAD_DIM = 64
DIFF_HEADS = D_MODEL // (2 * DIFF_HEAD_DIM)
DIFF_V_DIM = 2 * DIFF_HEAD_DIM
DIFF_QK_WIDTH = DIFF_HEADS * 2 * DIFF_HEAD_DIM
ODD_IN = 2 * DIFF_QK_WIDTH + DIFF_HEADS * DIFF_V_DIM
ODD_MIX = DIFF_HEADS * DIFF_V_DIM
ALIBI_MAX_EXP = 8.0
N_GROUPS = 4
EXPERTS_PER_GROUP = 8
N_EXPERTS = N_GROUPS * EXPERTS_PER_GROUP
TOP_K = 2
D_EXPERT = D_MODEL // 2
MOE_BLOCK = 256
N_EVEN = (DEPTH + 1) // 2
N_ODD = DEPTH // 2

kernel_name = "hybrid_stickbreak_conformer_diffattn_hmoe"


def rmsnorm(x, g):
    x32 = x.astype(jnp.float32)
    y = x32 * lax.rsqrt(jnp.mean(x32 * x32, axis=-1, keepdims=True) + EPS)
    return (y * g.astype(jnp.float32)).astype(x.dtype)


def stick_breaking_attention(q, k, v):
    B, H, S, d = q.shape
    nb = S // Q_BLOCK
    scale = d ** -0.5
    qb = q.reshape(B, H, nb, Q_BLOCK, d).transpose(2, 0, 1, 3, 4)
    key_pos = jnp.arange(S)

    def block(args):
        q_blk, i = args
        z = jnp.einsum('bhqd,bhkd->bhqk', q_blk, k).astype(jnp.float32) * scale
        t = i * Q_BLOCK + jnp.arange(Q_BLOCK)
        causal = key_pos[None, :] < t[:, None]
        log_fail = jnp.where(causal, jax.nn.log_sigmoid(-z), 0.0)
        tail = lax.cumsum(log_fail, axis=3, reverse=True) - log_fail
        a = jnp.where(causal, jnp.exp(jax.nn.log_sigmoid(z) + tail), 0.0)
        return jnp.einsum('bhqk,bhkd->bhqd', a.astype(v.dtype), v)

    out = lax.map(block, (qb, jnp.arange(nb)))
    return out.transpose(1, 2, 0, 3, 4).reshape(B, H, S, d)


def conformer_conv(u, b_glu, dw, dw_b, ln_g, ln_b):
    u = u + b_glu
    a, g = jnp.split(u, 2, axis=-1)
    h = a * jax.nn.sigmoid(g)
    h = lax.conv_general_dilated(h, dw[:, None, :], window_strides=(1,),
                                 padding=[(CONV_KERNEL - 1, 0)],
                                 dimension_numbers=('NWC', 'WIO', 'NWC'),
                                 feature_group_count=CONV_WIDTH) + dw_b
    h32 = h.astype(jnp.float32)
    mu = jnp.mean(h32, axis=-1, keepdims=True)
    var = jnp.mean(jnp.square(h32 - mu), axis=-1, keepdims=True)
    h = ((h32 - mu) * lax.rsqrt(var + EPS) * ln_g.astype(jnp.float32) + ln_b.astype(jnp.float32)).astype(u.dtype)
    return jax.nn.silu(h)


def diff_attention(q1, q2, k1, k2, v, lam, slopes):
    B, H, S, d = q1.shape
    nb = S // Q_BLOCK
    scale = d ** -0.5
    key_pos = jnp.arange(S)

    def to_blocks(t):
        return t.reshape(B, H, nb, Q_BLOCK, d).transpose(2, 0, 1, 3, 4)

    def block(args):
        qa, qb, i = args
        t = i * Q_BLOCK + jnp.arange(Q_BLOCK)
        allowed = (key_pos[None, :] // CHUNK) <= (t[:, None] // CHUNK)
        dist = jnp.abs(t[:, None] - key_pos[None, :]).astype(jnp.float32)
        bias = jnp.where(allowed[None], -slopes[:, None, None] * dist[None], -jnp.inf)
        s1 = jnp.einsum('bhqd,bhkd->bhqk', qa, k1).astype(jnp.float32) * scale + bias
        s2 = jnp.einsum('bhqd,bhkd->bhqk', qb, k2).astype(jnp.float32) * scale + bias
        p = jax.nn.softmax(s1, axis=-1) - lam * jax.nn.softmax(s2, axis=-1)
        return jnp.einsum('bhqk,bhkd->bhqd', p.astype(v.dtype), v)

    out = lax.map(block, (to_blocks(q1), to_blocks(q2), jnp.arange(nb)))
    return out.transpose(1, 2, 0, 3, 4).reshape(B, H, S, DIFF_V_DIM)


def even_mixer(h, w_in, b_glu, dw, dw_b, ln_g, ln_b, w_out):
    B, S, _ = h.shape
    proj = h @ w_in

    def heads(t):
        return t.reshape(B, S, SB_HEADS, SB_HEAD_DIM).transpose(0, 2, 1, 3)

    q = heads(proj[..., :SB_WIDTH])
    k = heads(proj[..., SB_WIDTH:2 * SB_WIDTH])
    v = heads(proj[..., 2 * SB_WIDTH:3 * SB_WIDTH])
    a_out = stick_breaking_attention(q, k, v).transpose(0, 2, 1, 3).reshape(B, S, SB_WIDTH)
    b_out = conformer_conv(proj[..., 3 * SB_WIDTH:], b_glu, dw, dw_b, ln_g, ln_b)
    return jnp.concatenate([a_out, b_out], axis=-1) @ w_out


def odd_mixer(h, w_in, q_g, k_g, lq1, lk1, lq2, lk2, sub_g, w_out, lam_init, slopes):
    B, S, _ = h.shape
    proj = h @ w_in
    q = proj[..., :DIFF_QK_WIDTH].reshape(B, S, DIFF_HEADS, 2, DIFF_HEAD_DIM)
    k = proj[..., DIFF_QK_WIDTH:2 * DIFF_QK_WIDTH].reshape(B, S, DIFF_HEADS, 2, DIFF_HEAD_DIM)
    v = proj[..., 2 * DIFF_QK_WIDTH:].reshape(B, S, DIFF_HEADS, DIFF_V_DIM).transpose(0, 2, 1, 3)
    q = rmsnorm(q, q_g).transpose(0, 2, 3, 1, 4)
    k = rmsnorm(k, k_g).transpose(0, 2, 3, 1, 4)
    lam = (jnp.exp(jnp.sum(lq1.astype(jnp.float32) * lk1.astype(jnp.float32)))
           - jnp.exp(jnp.sum(lq2.astype(jnp.float32) * lk2.astype(jnp.float32))) + lam_init)
    o = diff_attention(q[:, :, 0], q[:, :, 1], k[:, :, 0], k[:, :, 1], v, lam, slopes)
    o = rmsnorm(o, sub_g) * (1.0 - lam_init)
    o = o.transpose(0, 2, 1, 3).reshape(B, S, ODD_MIX)
    return o @ w_out


def hierarchical_moe(h, w_group, b_group, w_router, b_router, w_gate, w_up, w_down):
    B, S, D = h.shape
    N = B * S
    xf = h.reshape(N, D)
    g_logits = (xf @ w_group).astype(jnp.float32) + b_group.astype(jnp.float32)
    g_prob = jax.nn.softmax(g_logits, axis=-1)
    g_idx = jnp.argmax(g_logits, axis=-1).astype(jnp.int32)
    g_w = jnp.take_along_axis(g_prob, g_idx[:, None], axis=-1)[:, 0]
    e_logits = ((xf @ w_router).astype(jnp.float32) + b_router.astype(jnp.float32)).reshape(N, N_GROUPS, EXPERTS_PER_GROUP)
    e_logits = jnp.take_along_axis(e_logits, g_idx[:, None, None], axis=1)[:, 0]
    top_v, top_i = lax.top_k(e_logits, TOP_K)
    e_w = jax.nn.softmax(top_v, axis=-1) * g_w[:, None]
    e_id = g_idx[:, None] * EXPERTS_PER_GROUP + top_i.astype(jnp.int32)
    flat_e = e_id.reshape(-1)
    flat_w = e_w.reshape(-1)
    flat_tok = jnp.repeat(jnp.arange(N, dtype=jnp.int32), TOP_K)
    order = jnp.argsort(flat_e)
    sorted_e = flat_e[order]
    counts = jnp.zeros(N_EXPERTS, jnp.int32).at[flat_e].add(1)
    padded = (counts + MOE_BLOCK - 1) // MOE_BLOCK * MOE_BLOCK
    pad_end = jnp.cumsum(padded)
    pad_start = pad_end - padded
    start = jnp.cumsum(counts) - counts
    dest = pad_start[sorted_e] + jnp.arange(N * TOP_K, dtype=jnp.int32) - start[sorted_e]
    n_blocks = -(-(N * TOP_K) // MOE_BLOCK) + N_EXPERTS
    P = n_blocks * MOE_BLOCK
    slot_tok = jnp.zeros(P, jnp.int32).at[dest].set(flat_tok[order])
    slot_w = jnp.zeros(P, jnp.float32).at[dest].set(flat_w[order])
    block_e = jnp.minimum(jnp.searchsorted(pad_end, jnp.arange(n_blocks) * MOE_BLOCK, side='right'),
                          N_EXPERTS - 1).astype(jnp.int32)
    xs = xf[slot_tok].reshape(n_blocks, MOE_BLOCK, D)

    def expert_block(args):
        xb, e = args
        return (jax.nn.silu(xb @ w_gate[e]) * (xb @ w_up[e])) @ w_down[e]

    ys = lax.map(expert_block, (xs, block_e)).reshape(P, D)
    out = jnp.zeros((N, D), ys.dtype).at[slot_tok].add(ys * slot_w[:, None].astype(ys.dtype))
    return out.reshape(B, S, D)


def setup_inputs(seed: int = 0) -> dict:
    key = jax.random.key(seed)
    ks = jax.random.split(key, 32)

    def nrm(k, shape, std):
        return jax.random.normal(k, shape, jnp.float32) * std

    D = D_MODEL
    return {
        "x": nrm(ks[0], (BATCH, SEQ, D), 1.0),
        "c": nrm(ks[1], (BATCH, D), 1.0),
        "w_ada": nrm(ks[2], (DEPTH, D, 6 * D), 0.5 * D ** -0.5),
        "b_ada": nrm(ks[3], (DEPTH, 6 * D), 0.02),
        "mix_norm_g": 1.0 + nrm(ks[4], (DEPTH, D), 0.02),
        "ffn_norm_g": 1.0 + nrm(ks[5], (DEPTH, D), 0.02),
        "even_w_in": nrm(ks[6], (N_EVEN, D, EVEN_IN), D ** -0.5),
        "conv_b_glu": nrm(ks[7], (N_EVEN, 2 * CONV_WIDTH), 0.02),
        "conv_dw": nrm(ks[8], (N_EVEN, CONV_KERNEL, CONV_WIDTH), CONV_KERNEL ** -0.5),
        "conv_dw_b": nrm(ks[9], (N_EVEN, CONV_WIDTH), 0.02),
        "conv_ln_g": 1.0 + nrm(ks[10], (N_EVEN, CONV_WIDTH), 0.02),
        "conv_ln_b": nrm(ks[11], (N_EVEN, CONV_WIDTH), 0.02),
        "even_w_out": nrm(ks[12], (N_EVEN, EVEN_MIX, D), EVEN_MIX ** -0.5),
        "odd_w_in": nrm(ks[13], (N_ODD, D, ODD_IN), D ** -0.5),
        "qk_norm_q": 1.0 + nrm(ks[14], (N_ODD, DIFF_HEAD_DIM), 0.02),
        "qk_norm_k": 1.0 + nrm(ks[15], (N_ODD, DIFF_HEAD_DIM), 0.02),
        "lambda_q1": nrm(ks[16], (N_ODD, DIFF_HEAD_DIM), 0.1),
        "lambda_k1": nrm(ks[17], (N_ODD, DIFF_HEAD_DIM), 0.1),
        "lambda_q2": nrm(ks[18], (N_ODD, DIFF_HEAD_DIM), 0.1),
        "lambda_k2": nrm(ks[19], (N_ODD, DIFF_HEAD_DIM), 0.1),
        "diff_sub_g": 1.0 + nrm(ks[20], (N_ODD, DIFF_V_DIM), 0.02),
        "odd_w_out": nrm(ks[21], (N_ODD, ODD_MIX, D), ODD_MIX ** -0.5),
        "moe_w_group": nrm(ks[22], (DEPTH, D, N_GROUPS), D ** -0.5),
        "moe_b_group": nrm(ks[23], (DEPTH, N_GROUPS), 0.01),
        "moe_w_router": nrm(ks[24], (DEPTH, D, N_EXPERTS), D ** -0.5),
        "moe_b_router": nrm(ks[25], (DEPTH, N_EXPERTS), 0.01),
        "moe_w_gate": nrm(ks[26], (DEPTH, N_EXPERTS, D, D_EXPERT), D ** -0.5),
        "moe_w_up": nrm(ks[27], (DEPTH, N_EXPERTS, D, D_EXPERT), D ** -0.5),
        "moe_w_down": nrm(ks[28], (DEPTH, N_EXPERTS, D_EXPERT, D), D_EXPERT ** -0.5),
    }


def reference(x, c, w_ada, b_ada, mix_norm_g, ffn_norm_g, even_w_in, conv_b_glu, conv_dw,
              conv_dw_b, conv_ln_g, conv_ln_b, even_w_out, odd_w_in, qk_norm_q, qk_norm_k,
              lambda_q1, lambda_k1, lambda_q2, lambda_k2, diff_sub_g, odd_w_out,
              moe_w_group, moe_b_group, moe_w_router, moe_b_router, moe_w_gate, moe_w_up,
              moe_w_down):
    slopes = jnp.exp2(-ALIBI_MAX_EXP * jnp.arange(1, DIFF_HEADS + 1, dtype=jnp.float32) / DIFF_HEADS)
    c_act = jax.nn.silu(c)
    for layer in range(DEPTH):
        mod = c_act @ w_ada[layer] + b_ada[layer]
        sh1, sc1, g1, sh2, sc2, g2 = jnp.split(mod[:, None, :], 6, axis=-1)
        h = rmsnorm(x, mix_norm_g[layer]) * (1.0 + sc1) + sh1
        j = layer // 2
        if layer % 2 == 0:
            y = even_mixer(h, even_w_in[j], conv_b_glu[j], conv_dw[j], conv_dw_b[j],
                           conv_ln_g[j], conv_ln_b[j], even_w_out[j])
        else:
            lam_init = 0.8 - 0.6 * math.exp(-0.3 * layer)
            y = odd_mixer(h, odd_w_in[j], qk_norm_q[j], qk_norm_k[j], lambda_q1[j], lambda_k1[j],
                          lambda_q2[j], lambda_k2[j], diff_sub_g[j], odd_w_out[j], lam_init, slopes)
        x = x + g1 * y
        h = rmsnorm(x, ffn_norm_g[layer]) * (1.0 + sc2) + sh2
        x = x + g2 * hierarchical_moe(h, moe_w_group[layer], moe_b_group[layer], moe_w_router[layer],
                                      moe_b_router[layer], moe_w_gate[layer], moe_w_up[layer],
                                      moe_w_down[layer])
    return x
```

```python
import functools
import math

import jax
import jax.numpy as jnp
from jax import lax
from jax.experimental import pallas as pl
from jax.experimental.pallas import tpu as pltpu

F32 = jnp.float32
BF16 = jnp.bfloat16
U32 = jnp.uint32
I32 = jnp.int32

EPS = 1e-6
LANES = 128
HEAD_DIM = 64
CHUNK = 64
CONV_KERNEL = 31
CONV_HALO = 32
N_GROUPS = 4
EXPERTS_PER_GROUP = 8
N_EXPERTS = N_GROUPS * EXPERTS_PER_GROUP
MOE_BLOCK = 256
ALIBI_MAX_EXP = 8.0
VMEM_LIMIT_BYTES = 56 * 1024 * 1024
HIGHEST = lax.Precision.HIGHEST


def _cparams(*sem):
    return pltpu.CompilerParams(dimension_semantics=sem, vmem_limit_bytes=VMEM_LIMIT_BYTES)


def _div_pow2(v, n):
    shift = n.bit_length() - 1
    assert 1 << shift == n
    return lax.shift_right_arithmetic(v, jnp.int32(shift))


def _row_tile(n, want):
    t = min(n, want)
    assert n % t == 0
    return t


def _ada_kernel(c_ref, w_ref, b_ref, o_ref):
    c = c_ref[...]
    c_act = c * jax.nn.sigmoid(c)
    o_ref[...] = jnp.dot(c_act, w_ref[...], preferred_element_type=F32, precision=HIGHEST) + b_ref[...]


def _ada_mod(c, w_ada, b_ada):
    depth, d, d6 = w_ada.shape
    b = c.shape[0]
    rows = 8
    c_pad = jnp.zeros((rows, d), F32).at[:b].set(c)
    tn = _row_tile(d6, 1536)
    out = pl.pallas_call(
        _ada_kernel,
        out_shape=jax.ShapeDtypeStruct((depth, rows, d6), F32),
        grid=(depth, d6 // tn),
        in_specs=[
            pl.BlockSpec((rows, d), lambda l, j: (0, 0)),
            pl.BlockSpec((None, d, tn), lambda l, j: (l, 0, j)),
            pl.BlockSpec((None, 1, tn), lambda l, j: (l, 0, j)),
        ],
        out_specs=pl.BlockSpec((None, rows, tn), lambda l, j: (l, 0, j)),
        compiler_params=_cparams("parallel", "parallel"),
        name="ada_mod",
    )(c_pad, w_ada, b_ada.reshape(depth, 1, d6))
    return out[:, :b]


def _modulated_norm(x, g, sc, sh):
    ms = jnp.mean(x * x, axis=-1, keepdims=True)
    return (x * lax.rsqrt(ms + EPS) * g) * (1.0 + sc) + sh


def _inproj_kernel(x_ref, g_ref, sc_ref, sh_ref, w_ref, gn_ref, o_ref, *, n_norm, tn):
    h = _modulated_norm(x_ref[...], g_ref[...], sc_ref[...], sh_ref[...]).astype(BF16)
    n_out = o_ref.shape[-1]
    if n_norm:
        r = _div_pow2(lax.broadcasted_iota(I32, (tn, tn), 0), HEAD_DIM)
        c = _div_pow2(lax.broadcasted_iota(I32, (tn, tn), 1), HEAD_DIM)
        group_ones = jnp.where(r == c, 1.0, 0.0).astype(BF16)
    for c0 in range(0, n_out, tn):
        y = jnp.dot(h, w_ref[:, c0:c0 + tn], preferred_element_type=F32)
        if c0 < n_norm:
            sq = y * y
            hi = sq.astype(BF16)
            lo = (sq - hi.astype(F32)).astype(BF16)
            gs = (jnp.dot(hi, group_ones, preferred_element_type=F32)
                  + jnp.dot(lo, group_ones, preferred_element_type=F32))
            y = y * lax.rsqrt(gs * (1.0 / HEAD_DIM) + EPS) * gn_ref[:, c0:c0 + tn]
        o_ref[:, c0:c0 + tn] = y.astype(o_ref.dtype)


def _inproj(x, g, sc, sh, w, qk_gain=None):
    b, s, d = x.shape
    n_out = w.shape[1]
    tm = _row_tile(s, 512)
    tn = 256
    assert n_out % tn == 0
    n_norm = 0 if qk_gain is None else qk_gain.shape[1]
    assert n_norm % tn == 0
    gn = jnp.zeros((1, n_out), F32)
    if n_norm:
        gn = gn.at[:, :n_norm].set(qk_gain)
    vec = pl.BlockSpec((None, 1, d), lambda bi, i: (bi, 0, 0))
    return pl.pallas_call(
        functools.partial(_inproj_kernel, n_norm=n_norm, tn=tn),
        out_shape=jax.ShapeDtypeStruct((b, s, n_out), BF16),
        grid=(b, s // tm),
        in_specs=[
            pl.BlockSpec((None, tm, d), lambda bi, i: (bi, i, 0)),
            pl.BlockSpec((1, d), lambda bi, i: (0, 0)),
            vec, vec,
            pl.BlockSpec((d, n_out), lambda bi, i: (0, 0)),
            pl.BlockSpec((1, n_out), lambda bi, i: (0, 0)),
        ],
        out_specs=pl.BlockSpec((None, tm, n_out), lambda bi, i: (bi, i, 0)),
        compiler_params=_cparams("parallel", "parallel"),
        name="inproj",
    )(x, g.reshape(1, d), sc, sh, w.astype(BF16), gn)


def _outproj_kernel(*refs, n_in):
    ins, ws = refs[:n_in], refs[n_in:2 * n_in]
    x_ref, g_ref, o_ref = refs[2 * n_in:]
    acc = jnp.dot(ins[0][...], ws[0][...], preferred_element_type=F32)
    for a, w in zip(ins[1:], ws[1:]):
        acc += jnp.dot(a[...], w[...], preferred_element_type=F32)
    o_ref[...] = x_ref[...] + g_ref[...] * acc


def _outproj_residual(parts, weights, x, gate):
    b, s, d = x.shape
    tm = _row_tile(s, 512)
    n_in = len(parts)
    in_specs = [pl.BlockSpec((None, tm, p.shape[-1]), lambda bi, i: (bi, i, 0)) for p in parts]
    in_specs += [pl.BlockSpec(w.shape, lambda bi, i: (0, 0)) for w in weights]
    in_specs += [pl.BlockSpec((None, tm, d), lambda bi, i: (bi, i, 0)),
                 pl.BlockSpec((None, 1, d), lambda bi, i: (bi, 0, 0))]
    return pl.pallas_call(
        functools.partial(_outproj_kernel, n_in=n_in),
        out_shape=jax.ShapeDtypeStruct((b, s, d), F32),
        grid=(b, s // tm),
        in_specs=in_specs,
        out_specs=pl.BlockSpec((None, tm, d), lambda bi, i: (bi, i, 0)),
        compiler_params=_cparams("parallel", "parallel"),
        name="outproj",
    )(*parts, *[w.astype(BF16) for w in weights], x, gate)


def _dot_nt(a, b):
    return lax.dot_general(a, b, (((1,), (1,)), ((), ())), preferred_element_type=F32)


def _sb_kernel(q_ref, k_ref, v_ref, o_ref, acc_ref, *, t):
    i = pl.program_id(2)
    lane = lax.broadcasted_iota(I32, (1, LANES), 1)
    row = lax.broadcasted_iota(I32, (t, t), 0)
    col = lax.broadcasted_iota(I32, (t, t), 1)
    later = jnp.where(row > col, 1.0, 0.0).astype(BF16)
    causal = col < row
    q = q_ref[...]
    scale = HEAD_DIM ** -0.5

    def tile(qh, j, carry, masked):
        kt = k_ref[pl.ds(pl.multiple_of(j * t, t), t), :]
        vt = v_ref[pl.ds(pl.multiple_of(j * t, t), t), :]
        z = _dot_nt(qh, kt)
        log_fail = jnp.minimum(-z, 0.0) - jnp.log(1.0 + jnp.exp(-jnp.abs(z)))
        if masked:
            log_fail = jnp.where(causal, log_fail, 0.0)
        hi = log_fail.astype(BF16)
        lo = (log_fail - hi.astype(F32)).astype(BF16)
        tail = (jnp.dot(hi, later, preferred_element_type=F32)
                + jnp.dot(lo, later, preferred_element_type=F32)) + carry
        a = jnp.exp(z + log_fail + tail)
        if masked:
            a = jnp.where(causal, a, 0.0)
        pv = jnp.dot(a.astype(BF16), vt, preferred_element_type=F32)
        return pv, carry + jnp.sum(log_fail, axis=-1, keepdims=True)

    for hh in range(LANES // HEAD_DIM):
        in_head = (lane >= hh * HEAD_DIM) & (lane < (hh + 1) * HEAD_DIM)
        qh = jnp.where(in_head, q, jnp.zeros_like(q)) * jnp.asarray(scale, q.dtype)
        pv, carry = tile(qh, i, jnp.zeros((t, 1), F32), True)
        acc_ref[hh] = pv

        def body(jj, carry, qh=qh, hh=hh):
            pv, carry = tile(qh, i - 1 - jj, carry, False)
            acc_ref[hh] += pv
            return carry

        lax.fori_loop(0, i, body, carry)
    o_ref[...] = jnp.where(lane < HEAD_DIM, acc_ref[0], acc_ref[1]).astype(o_ref.dtype)


def _stick_breaking(proj, width):
    b, s, _ = proj.shape
    t = _row_tile(s, 256)
    nblk = width // LANES
    return pl.pallas_call(
        functools.partial(_sb_kernel, t=t),
        out_shape=jax.ShapeDtypeStruct((b, s, width), BF16),
        grid=(b, nblk, s // t),
        in_specs=[
            pl.BlockSpec((None, t, LANES), lambda bi, h, i: (bi, i, h)),
            pl.BlockSpec((None, s, LANES), lambda bi, h, i: (bi, 0, nblk + h)),
            pl.BlockSpec((None, s, LANES), lambda bi, h, i: (bi, 0, 2 * nblk + h)),
        ],
        out_specs=pl.BlockSpec((None, t, LANES), lambda bi, h, i: (bi, i, h)),
        scratch_shapes=[pltpu.VMEM((LANES // HEAD_DIM, t, LANES), F32)],
        compiler_params=_cparams("parallel", "parallel", "parallel"),
        name="stick_breaking",
    )(proj, proj, proj)


def _conv_kernel(a_ref, g_ref, pa_ref, pg_ref, bglu_ref, dw_ref, dwb_ref, lng_ref, lnb_ref, o_ref, hbuf, *, ts, rc):
    i = pl.program_id(1)
    w = a_ref.shape[-1]

    def glu(a, g):
        a = a.astype(F32) + bglu_ref[:, :w]
        g = g.astype(F32) + bglu_ref[:, w:]
        return a * jax.nn.sigmoid(g)

    prev = glu(pa_ref[...], pg_ref[...])
    hbuf[0:CONV_HALO] = jnp.where(i > 0, prev, 0.0)
    hbuf[CONV_HALO:CONV_HALO + ts] = glu(a_ref[...], g_ref[...])
    off = CONV_HALO - (CONV_KERNEL - 1)
    for r0 in range(0, ts, rc):
        acc = jnp.zeros((rc, w), F32) + dwb_ref[...]
        for k in range(CONV_KERNEL):
            acc = acc + dw_ref[k:k + 1, :] * hbuf[r0 + off + k:r0 + off + k + rc, :]
        mu = jnp.mean(acc, axis=-1, keepdims=True)
        cen = acc - mu
        var = jnp.mean(cen * cen, axis=-1, keepdims=True)
        y = cen * lax.rsqrt(var + EPS) * lng_ref[...] + lnb_ref[...]
        o_ref[r0:r0 + rc, :] = (y * jax.nn.sigmoid(y)).astype(o_ref.dtype)


def _conformer_conv(proj, col0, width, b_glu, dw, dw_b, ln_g, ln_b):
    b, s, _ = proj.shape
    ts = _row_tile(s, 256)
    assert col0 % width == 0 and ts % CONV_HALO == 0
    cb = col0 // width
    hb = ts // CONV_HALO
    cur = lambda off: pl.BlockSpec((None, ts, width), lambda bi, i: (bi, i, cb + off))
    prev = lambda off: pl.BlockSpec((None, CONV_HALO, width),
                                    lambda bi, i: (bi, jnp.maximum(i * hb - 1, 0), cb + off))
    row = lambda n: pl.BlockSpec((1, n), lambda bi, i: (0, 0))
    return pl.pallas_call(
        functools.partial(_conv_kernel, ts=ts, rc=32),
        out_shape=jax.ShapeDtypeStruct((b, s, width), BF16),
        grid=(b, s // ts),
        in_specs=[cur(0), cur(1), prev(0), prev(1), row(2 * width),
                  pl.BlockSpec((CONV_KERNEL, width), lambda bi, i: (0, 0)),
                  row(width), row(width), row(width)],
        out_specs=pl.BlockSpec((None, ts, width), lambda bi, i: (bi, i, 0)),
        scratch_shapes=[pltpu.VMEM((CONV_HALO + ts, width), F32)],
        compiler_params=_cparams("parallel", "parallel"),
        name="conformer_conv",
    )(proj, proj, proj, proj, b_glu.reshape(1, -1), dw, dw_b.reshape(1, -1),
      ln_g.reshape(1, -1), ln_b.reshape(1, -1))


def _diff_kernel(slope_ref, q_ref, k_ref, v_ref, lq1_ref, lk1_ref, lq2_ref, lk2_ref, subg_ref, o_ref,
                 acc_ref, *, t, lam_init):
    h = pl.program_id(1)
    i = pl.program_id(2)
    slope = slope_ref[h]
    lane = lax.broadcasted_iota(I32, (1, LANES), 1)
    row = lax.broadcasted_iota(I32, (t, t), 0)
    col = lax.broadcasted_iota(I32, (t, t), 1)
    rel = (row - col).astype(F32)
    bias_past = -slope * rel
    bias_diag = jnp.where(_div_pow2(col, CHUNK) <= _div_pow2(row, CHUNK), -slope * jnp.abs(rel), -jnp.inf)
    q = q_ref[...]
    zero = jnp.zeros_like(q)
    qs = (jnp.where(lane < HEAD_DIM, q, zero), jnp.where(lane >= HEAD_DIM, q, zero))

    def tile(j, state, bias, shift):
        kt = k_ref[pl.ds(pl.multiple_of(j * t, t), t), :]
        vt = v_ref[pl.ds(pl.multiple_of(j * t, t), t), :]
        new_state = []
        for n in range(2):
            m, l = state[n]
            s = _dot_nt(qs[n], kt) + bias
            m_new = jnp.maximum(m, jnp.max(s, axis=-1, keepdims=True) + shift)
            alpha = jnp.exp(m - m_new)
            p = jnp.exp(s - (m_new - shift))
            l = alpha * l + jnp.sum(p, axis=-1, keepdims=True)
            acc_ref[n] = alpha * acc_ref[n] + jnp.dot(p.astype(BF16), vt, preferred_element_type=F32)
            new_state.append((m_new, l))
        return tuple(new_state)

    acc_ref[...] = jnp.zeros_like(acc_ref)
    neg = jnp.full((t, 1), -jnp.inf, F32)
    zl = jnp.zeros((t, 1), F32)
    state = tile(i, ((neg, zl), (neg, zl)), bias_diag, 0.0)

    def body(jj, state):
        gap = ((jj + 1) * t).astype(F32)
        return tile(i - 1 - jj, state, bias_past, -slope * gap)

    (m1, l1), (m2, l2) = lax.fori_loop(0, i, body, state)
    lam = (jnp.exp(jnp.sum(lq1_ref[...] * lk1_ref[...], axis=-1, keepdims=True))
           - jnp.exp(jnp.sum(lq2_ref[...] * lk2_ref[...], axis=-1, keepdims=True)) + lam_init)
    o = acc_ref[0] / l1 - lam * (acc_ref[1] / l2)
    ms = jnp.mean(o * o, axis=-1, keepdims=True)
    o_ref[...] = (o * lax.rsqrt(ms + EPS) * subg_ref[...] * (1.0 - lam_init)).astype(o_ref.dtype)


def _diff_attention(proj, n_heads, lq1, lk1, lq2, lk2, sub_g, lam_init):
    b, s, _ = proj.shape
    t = _row_tile(s, 256)
    slopes = jnp.exp2(-ALIBI_MAX_EXP * jnp.arange(1, n_heads + 1, dtype=F32) / n_heads)
    vec = lambda n: pl.BlockSpec((1, n), lambda bi, h, i, sl: (0, 0))
    return pl.pallas_call(
        functools.partial(_diff_kernel, t=t, lam_init=lam_init),
        out_shape=jax.ShapeDtypeStruct((b, s, n_heads * LANES), BF16),
        grid_spec=pltpu.PrefetchScalarGridSpec(
            num_scalar_prefetch=1,
            grid=(b, n_heads, s // t),
            in_specs=[
                pl.BlockSpec((None, t, LANES), lambda bi, h, i, sl: (bi, i, h)),
                pl.BlockSpec((None, s, LANES), lambda bi, h, i, sl: (bi, 0, n_heads + h)),
                pl.BlockSpec((None, s, LANES), lambda bi, h, i, sl: (bi, 0, 2 * n_heads + h)),
                vec(HEAD_DIM), vec(HEAD_DIM), vec(HEAD_DIM), vec(HEAD_DIM), vec(LANES),
            ],
            out_specs=pl.BlockSpec((None, t, LANES), lambda bi, h, i, sl: (bi, i, h)),
            scratch_shapes=[pltpu.VMEM((2, t, LANES), F32)],
        ),
        compiler_params=_cparams("parallel", "parallel", "parallel"),
        name="diff_attention",
    )(slopes, proj, proj, proj, lq1.reshape(1, -1), lk1.reshape(1, -1), lq2.reshape(1, -1),
      lk2.reshape(1, -1), sub_g.reshape(1, -1))


ROUTE_E0, ROUTE_E1, ROUTE_W0, ROUTE_W1, ROUTE_R0, ROUTE_R1 = range(6)


def _pack_bf16_pair(lo, hi):
    lo_bits = pltpu.bitcast(lo.astype(BF16).astype(F32), U32) >> 16
    hi_bits = pltpu.bitcast(hi.astype(BF16).astype(F32), U32) & jnp.uint32(0xFFFF0000)
    return hi_bits | lo_bits


def _unpack_bf16_pair(u):
    lo = pltpu.bitcast(u << 16, F32).astype(BF16)
    hi = pltpu.bitcast(u & jnp.uint32(0xFFFF0000), F32).astype(BF16)
    return lo, hi


def _router_kernel(x_ref, g_ref, sc_ref, sh_ref, wr_ref, br_ref, hp_ref, rt_ref, cnt_ref, run_ref, *, tm):
    first = (pl.program_id(0) == 0) & (pl.program_id(1) == 0)

    @pl.when(first)
    def _():
        run_ref[...] = jnp.zeros_like(run_ref)

    h = _modulated_norm(x_ref[...], g_ref[...], sc_ref[...], sh_ref[...])
    half = h.shape[-1] // 2
    hp_ref[...] = _pack_bf16_pair(h[:, :half], h[:, half:])

    logits = jnp.dot(h, wr_ref[...], preferred_element_type=F32, precision=HIGHEST) + br_ref[...]
    lane = lax.broadcasted_iota(I32, (tm, LANES), 1)
    lanef = lane.astype(F32)
    ninf = -jnp.inf
    big = float(LANES)

    def first_argmax(vals):
        top = jnp.max(vals, axis=-1, keepdims=True)
        idx = jnp.min(jnp.where(vals == top, lanef, big), axis=-1, keepdims=True)
        return top, idx

    is_group = lane < N_GROUPS
    gl = jnp.where(is_group, logits, ninf)
    gmax, gidx = first_argmax(gl)
    g_w = 1.0 / jnp.sum(jnp.where(is_group, jnp.exp(gl - gmax), 0.0), axis=-1, keepdims=True)
    lane_group = _div_pow2(lane - N_GROUPS, EXPERTS_PER_GROUP).astype(F32)
    in_group = (lane >= N_GROUPS) & (lane < N_GROUPS + N_EXPERTS) & (lane_group == gidx)
    el = jnp.where(in_group, logits, ninf)
    v0, i0 = first_argmax(el)
    v1, i1 = first_argmax(jnp.where(lanef == i0, ninf, el))
    tt = jnp.exp(v1 - v0)
    w0 = g_w / (1.0 + tt)
    w1 = g_w * tt / (1.0 + tt)
    e0 = i0 - N_GROUPS
    e1 = i1 - N_GROUPS

    member = (lanef == e0) | (lanef == e1)
    r = lax.broadcasted_iota(I32, (tm, tm), 0)
    c = lax.broadcasted_iota(I32, (tm, tm), 1)
    earlier = jnp.where(c < r, 1.0, 0.0).astype(BF16)
    before = jnp.dot(earlier, jnp.where(member, 1.0, 0.0).astype(BF16), preferred_element_type=F32)
    before = before + run_ref[...]
    r0 = jnp.sum(jnp.where(lanef == e0, before, 0.0), axis=-1, keepdims=True)
    r1 = jnp.sum(jnp.where(lanef == e1, before, 0.0), axis=-1, keepdims=True)
    run_ref[...] += jnp.sum(jnp.where(member, 1.0, 0.0), axis=0, keepdims=True)
    cnt_ref[...] = run_ref[...]

    slab = jnp.zeros((tm, LANES), F32)
    for colv, val in ((ROUTE_E0, e0), (ROUTE_E1, e1), (ROUTE_W0, w0), (ROUTE_W1, w1), (ROUTE_R0, r0), (ROUTE_R1, r1)):
        slab = jnp.where(lane == colv, val, slab)
    rt_ref[...] = slab


def _router(x, g, sc, sh, w_group, b_group, w_router, b_router):
    b, s, d = x.shape
    tm = _row_tile(s, 256)
    wr = jnp.zeros((d, LANES), F32).at[:, :N_GROUPS].set(w_group).at[:, N_GROUPS:N_GROUPS + N_EXPERTS].set(w_router)
    br = jnp.zeros((1, LANES), F32).at[0, :N_GROUPS].set(b_group).at[0, N_GROUPS:N_GROUPS + N_EXPERTS].set(b_router)
    vec = pl.BlockSpec((None, 1, d), lambda bi, i: (bi, 0, 0))
    return pl.pallas_call(
        functools.partial(_router_kernel, tm=tm),
        out_shape=(jax.ShapeDtypeStruct((b, s, d // 2), U32),
                   jax.ShapeDtypeStruct((b, s, LANES), F32),
                   jax.ShapeDtypeStruct((1, LANES), F32)),
        grid=(b, s // tm),
        in_specs=[
            pl.BlockSpec((None, tm, d), lambda bi, i: (bi, i, 0)),
            pl.BlockSpec((1, d), lambda bi, i: (0, 0)),
            vec, vec,
            pl.BlockSpec((d, LANES), lambda bi, i: (0, 0)),
            pl.BlockSpec((1, LANES), lambda bi, i: (0, 0)),
        ],
        out_specs=(pl.BlockSpec((None, tm, d // 2), lambda bi, i: (bi, i, 0)),
                   pl.BlockSpec((None, tm, LANES), lambda bi, i: (bi, i, 0)),
                   pl.BlockSpec((1, LANES), lambda bi, i: (0, 0))),
        scratch_shapes=[pltpu.VMEM((1, LANES), F32)],
        compiler_params=_cparams("arbitrary", "arbitrary"),
        name="moe_router",
    )(x, g.reshape(1, d), sc, sh, wr, br)


def _zero_kernel(o_ref):
    o_ref[...] = jnp.zeros_like(o_ref)


def _zeros_u32(rows, cols):
    tr = _row_tile(rows, 2048)
    return pl.pallas_call(
        _zero_kernel,
        out_shape=jax.ShapeDtypeStruct((rows, cols), U32),
        grid=(rows // tr,),
        out_specs=pl.BlockSpec((tr, cols), lambda i: (i, 0)),
        compiler_params=_cparams("parallel"),
        name="moe_zero_slots",
    )()


def _dispatch_kernel(dest_ref, h_ref, xs_in_ref, xs_ref, sem, *, tm):
    del xs_in_ref
    base = pl.program_id(0) * tm

    def row_copy(r, slot):
        return pltpu.make_async_copy(h_ref.at[pl.ds(r, 1), :], xs_ref.at[pl.ds(slot, 1), :], sem)

    def body(r, carry):
        row_copy(r, dest_ref[2 * (base + r)]).start()
        row_copy(r, dest_ref[2 * (base + r) + 1]).start()
        return carry

    lax.fori_loop(0, tm, body, 0)

    def drain(r, carry):
        row_copy(r, 0).wait()
        row_copy(r, 0).wait()
        return carry

    lax.fori_loop(0, tm, drain, 0)


def _dispatch(dest, h_packed, n_slots):
    n, w = h_packed.shape
    tm = _row_tile(n, 256)
    return pl.pallas_call(
        functools.partial(_dispatch_kernel, tm=tm),
        out_shape=jax.ShapeDtypeStruct((n_slots, w), U32),
        grid_spec=pltpu.PrefetchScalarGridSpec(
            num_scalar_prefetch=1,
            grid=(n // tm,),
            in_specs=[pl.BlockSpec((tm, w), lambda i, d: (i, 0)),
                      pl.BlockSpec(memory_space=pl.ANY)],
            out_specs=pl.BlockSpec(memory_space=pl.ANY),
            scratch_shapes=[pltpu.SemaphoreType.DMA(())],
        ),
        input_output_aliases={2: 0},
        compiler_params=_cparams("arbitrary"),
        name="moe_dispatch",
    )(dest, h_packed, _zeros_u32(n_slots, w))


def _expert_kernel(be_ref, nv_ref, xs_ref, wg_ref, wu_ref, wd_ref, ys_ref, wg_b, wu_b, wd_b):
    blk = pl.program_id(0)
    valid = blk < nv_ref[0]
    changed = (blk == 0) | (be_ref[blk] != be_ref[jnp.maximum(blk - 1, 0)])

    @pl.when(valid & changed)
    def _():
        wg_b[...] = wg_ref[...].astype(BF16)
        wu_b[...] = wu_ref[...].astype(BF16)
        wd_b[...] = wd_ref[...].astype(BF16)

    @pl.when(valid)
    def _():
        x_lo, x_hi = _unpack_bf16_pair(xs_ref[...])
        half = x_lo.shape[-1]

        def proj(w):
            return (jnp.dot(x_lo, w[:half, :], preferred_element_type=F32)
                    + jnp.dot(x_hi, w[half:, :], preferred_element_type=F32))

        gate = proj(wg_b)
        hid = (gate * jax.nn.sigmoid(gate)) * proj(wu_b)
        ys_ref[...] = jnp.dot(hid.astype(BF16), wd_b[...], preferred_element_type=F32)

    @pl.when(jnp.logical_not(valid))
    def _():
        ys_ref[...] = jnp.zeros_like(ys_ref)


def _experts(block_e, n_valid, xs, w_gate, w_up, w_down, layer):
    n_slots, half = xs.shape
    d = 2 * half
    de = w_gate.shape[-1]
    n_blocks = n_slots // MOE_BLOCK
    last = lambda blk, nv: jnp.minimum(blk, nv[0] - 1)
    w_in = pl.BlockSpec((None, None, d, de), lambda blk, be, nv: (layer, be[last(blk, nv)], 0, 0))
    return pl.pallas_call(
        _expert_kernel,
        out_shape=jax.ShapeDtypeStruct((n_slots, d), F32),
        grid_spec=pltpu.PrefetchScalarGridSpec(
            num_scalar_prefetch=2,
            grid=(n_blocks,),
            in_specs=[pl.BlockSpec((MOE_BLOCK, half), lambda blk, be, nv: (last(blk, nv), 0)),
                      w_in, w_in,
                      pl.BlockSpec((None, None, de, d), lambda blk, be, nv: (layer, be[last(blk, nv)], 0, 0))],
            out_specs=pl.BlockSpec((MOE_BLOCK, d), lambda blk, be, nv: (blk, 0)),
            scratch_shapes=[pltpu.VMEM((d, de), BF16), pltpu.VMEM((d, de), BF16), pltpu.VMEM((de, d), BF16)],
        ),
        compiler_params=_cparams("arbitrary"),
        name="moe_experts",
    )(block_e, n_valid, xs, w_gate, w_up, w_down)


def _combine_kernel(dest_ref, x_ref, rt_ref, g_ref, ys_ref, o_ref, buf, sem, *, tm):
    base = pl.program_id(0) * tm

    def row_copy(k, r, slot):
        return pltpu.make_async_copy(ys_ref.at[pl.ds(slot, 1), :], buf.at[k, pl.ds(r, 1), :], sem)

    def body(r, carry):
        row_copy(0, r, dest_ref[2 * (base + r)]).start()
        row_copy(1, r, dest_ref[2 * (base + r) + 1]).start()
        return carry

    lax.fori_loop(0, tm, body, 0)

    def drain(r, carry):
        row_copy(0, r, 0).wait()
        row_copy(1, r, 0).wait()
        return carry

    lax.fori_loop(0, tm, drain, 0)
    rt = rt_ref[...]
    w0 = rt[:, ROUTE_W0:ROUTE_W0 + 1]
    w1 = rt[:, ROUTE_W1:ROUTE_W1 + 1]
    o_ref[...] = x_ref[...] + g_ref[...] * (w0 * buf[0] + w1 * buf[1])


def _combine(dest, x2, route, gate, ys, tiles_per_batch_of):
    n, d = x2.shape
    tm = _row_tile(n, 256)
    per_batch = tiles_per_batch_of(tm)
    return pl.pallas_call(
        functools.partial(_combine_kernel, tm=tm),
        out_shape=jax.ShapeDtypeStruct((n, d), F32),
        grid_spec=pltpu.PrefetchScalarGridSpec(
            num_scalar_prefetch=1,
            grid=(n // tm,),
            in_specs=[pl.BlockSpec((tm, d), lambda i, de: (i, 0)),
                      pl.BlockSpec((tm, LANES), lambda i, de: (i, 0)),
                      pl.BlockSpec((None, 1, d), lambda i, de: (i // per_batch, 0, 0)),
                      pl.BlockSpec(memory_space=pl.ANY)],
            out_specs=pl.BlockSpec((tm, d), lambda i, de: (i, 0)),
            scratch_shapes=[pltpu.VMEM((2, tm, d), F32), pltpu.SemaphoreType.DMA(())],
        ),
        compiler_params=_cparams("arbitrary"),
        name="moe_combine",
    )(dest, x2, route, gate, ys)


def _moe_layer(x, g, sc, sh, gate, w_group, b_group, w_router, b_router, w_gate, w_up, w_down, layer):
    b, s, d = x.shape
    n = b * s
    h_packed, route, counts = _router(x, g, sc, sh, w_group, b_group, w_router, b_router)
    route = route.reshape(n, LANES)
    counts = counts[0, :N_EXPERTS].astype(I32)
    padded = (counts + MOE_BLOCK - 1) // MOE_BLOCK * MOE_BLOCK
    pad_end = jnp.cumsum(padded)
    pad_start = pad_end - padded
    e_id = route[:, ROUTE_E0:ROUTE_E1 + 1].astype(I32)
    rank = route[:, ROUTE_R0:ROUTE_R1 + 1].astype(I32)
    dest = (pad_start[e_id] + rank).reshape(-1)
    n_blocks = -(-(n * 2) // MOE_BLOCK) + N_EXPERTS
    block_e = jnp.minimum(jnp.searchsorted(pad_end, jnp.arange(n_blocks, dtype=I32) * MOE_BLOCK, side='right'),
                          N_EXPERTS - 1).astype(I32)
    n_valid = (pad_end[-1:] // MOE_BLOCK).astype(I32)
    xs = _dispatch(dest, h_packed.reshape(n, d // 2), n_blocks * MOE_BLOCK)
    ys = _experts(block_e, n_valid, xs, w_gate, w_up, w_down, layer)
    out = _combine(dest, x.reshape(n, d), route, gate, ys, lambda tm: s // tm)
    return out.reshape(b, s, d)


def kernel(x, c, w_ada, b_ada, mix_norm_g, ffn_norm_g, even_w_in, conv_b_glu, conv_dw, conv_dw_b, conv_ln_g, conv_ln_b, even_w_out, odd_w_in, qk_norm_q, qk_norm_k, lambda_q1, lambda_k1, lambda_q2, lambda_k2, diff_sub_g, odd_w_out, moe_w_group, moe_b_group, moe_w_router, moe_b_router, moe_w_gate, moe_w_up, moe_w_down):
    depth = w_ada.shape[0]
    d = x.shape[-1]
    sb_width = d // 2
    conv_width = d // 2
    diff_heads = d // (2 * HEAD_DIM)
    qk_width = diff_heads * 2 * HEAD_DIM
    mod = _ada_mod(c, w_ada, b_ada)
    for layer in range(depth):
        sh1, sc1, g1, sh2, sc2, g2 = [m[:, None, :] for m in jnp.split(mod[layer], 6, axis=-1)]
        j = layer // 2
        if layer % 2 == 0:
            proj = _inproj(x, mix_norm_g[layer], sc1, sh1, even_w_in[j])
            a_out = _stick_breaking(proj, sb_width)
            b_out = _conformer_conv(proj, 3 * sb_width, conv_width, conv_b_glu[j], conv_dw[j], conv_dw_b[j],
                                    conv_ln_g[j], conv_ln_b[j])
            x = _outproj_residual([a_out, b_out], [even_w_out[j][:sb_width], even_w_out[j][sb_width:]], x, g1)
        else:
            lam_init = 0.8 - 0.6 * math.exp(-0.3 * layer)
            reps = qk_width // HEAD_DIM
            qk_gain = jnp.concatenate([jnp.tile(qk_norm_q[j] * HEAD_DIM ** -0.5, reps),
                                       jnp.tile(qk_norm_k[j], reps)])[None, :]
            proj = _inproj(x, mix_norm_g[layer], sc1, sh1, odd_w_in[j], qk_gain)
            o = _diff_attention(proj, diff_heads, lambda_q1[j], lambda_k1[j], lambda_q2[j], lambda_k2[j],
                                diff_sub_g[j], lam_init)
            x = _outproj_residual([o], [odd_w_out[j]], x, g1)
        x = _moe_layer(x, ffn_norm_g[layer], sc2, sh2, g2, moe_w_group[layer], moe_b_group[layer],
                       moe_w_router[layer], moe_b_router[layer], moe_w_gate, moe_w_up, moe_w_down, layer)
    return x
```

```python
import functools
import math

import jax
import jax.numpy as jnp
from jax import lax
from jax.experimental import pallas as pl
from jax.experimental.pallas import tpu as pltpu

F32 = jnp.float32
BF16 = jnp.bfloat16
U32 = jnp.uint32
I32 = jnp.int32

EPS = 1e-6
LANES = 128
HEAD_DIM = 64
CHUNK = 64
CONV_KERNEL = 31
CONV_HALO = 32
N_GROUPS = 4
EXPERTS_PER_GROUP = 8
N_EXPERTS = N_GROUPS * EXPERTS_PER_GROUP
MOE_BLOCK = 256
ALIBI_MAX_EXP = 8.0
VMEM_LIMIT_BYTES = 56 * 1024 * 1024
HIGHEST = lax.Precision.HIGHEST


def _cparams(*sem):
    return pltpu.CompilerParams(dimension_semantics=sem, vmem_limit_bytes=VMEM_LIMIT_BYTES)


def _div_pow2(v, n):
    shift = n.bit_length() - 1
    assert 1 << shift == n
    return lax.shift_right_arithmetic(v, jnp.int32(shift))


def _row_tile(n, want):
    t = min(n, want)
    assert n % t == 0
    return t


def _ada_kernel(c_ref, w_ref, b_ref, o_ref):
    c = c_ref[...]
    c_act = c * jax.nn.sigmoid(c)
    o_ref[...] = jnp.dot(c_act, w_ref[...], preferred_element_type=F32, precision=HIGHEST) + b_ref[...]


def _ada_mod(c, w_ada, b_ada):
    depth, d, d6 = w_ada.shape
    b = c.shape[0]
    rows = 8
    c_pad = jnp.zeros((rows, d), F32).at[:b].set(c)
    tn = _row_tile(d6, 1536)
    out = pl.pallas_call(
        _ada_kernel,
        out_shape=jax.ShapeDtypeStruct((depth, rows, d6), F32),
        grid=(depth, d6 // tn),
        in_specs=[
            pl.BlockSpec((rows, d), lambda l, j: (0, 0)),
            pl.BlockSpec((None, d, tn), lambda l, j: (l, 0, j)),
            pl.BlockSpec((None, 1, tn), lambda l, j: (l, 0, j)),
        ],
        out_specs=pl.BlockSpec((None, rows, tn), lambda l, j: (l, 0, j)),
        compiler_params=_cparams("parallel", "parallel"),
        name="ada_mod",
    )(c_pad, w_ada, b_ada.reshape(depth, 1, d6))
    return out[:, :b]


def _modulated_norm(x, g, sc, sh):
    ms = jnp.mean(x * x, axis=-1, keepdims=True)
    return (x * lax.rsqrt(ms + EPS) * g) * (1.0 + sc) + sh


def _inproj_kernel(x_ref, g_ref, sc_ref, sh_ref, w_ref, gn_ref, o_ref, *, n_norm, tn):
    h = _modulated_norm(x_ref[...], g_ref[...], sc_ref[...], sh_ref[...]).astype(BF16)
    n_out = o_ref.shape[-1]
    if n_norm:
        r = _div_pow2(lax.broadcasted_iota(I32, (tn, tn), 0), HEAD_DIM)
        c = _div_pow2(lax.broadcasted_iota(I32, (tn, tn), 1), HEAD_DIM)
        group_ones = jnp.where(r == c, 1.0, 0.0).astype(BF16)
    for c0 in range(0, n_out, tn):
        y = jnp.dot(h, w_ref[:, c0:c0 + tn], preferred_element_type=F32)
        if c0 < n_norm:
            sq = y * y
            hi = sq.astype(BF16)
            lo = (sq - hi.astype(F32)).astype(BF16)
            gs = (jnp.dot(hi, group_ones, preferred_element_type=F32)
                  + jnp.dot(lo, group_ones, preferred_element_type=F32))
            y = y * lax.rsqrt(gs * (1.0 / HEAD_DIM) + EPS) * gn_ref[:, c0:c0 + tn]
        o_ref[:, c0:c0 + tn] = y.astype(o_ref.dtype)


def _inproj(x, g, sc, sh, w, qk_gain=None):
    b, s, d = x.shape
    n_out = w.shape[1]
    tm = _row_tile(s, 512)
    tn = 256
    assert n_out % tn == 0
    n_norm = 0 if qk_gain is None else qk_gain.shape[1]
    assert n_norm % tn == 0
    gn = jnp.zeros((1, n_out), F32)
    if n_norm:
        gn = gn.at[:, :n_norm].set(qk_gain)
    vec = pl.BlockSpec((None, 1, d), lambda bi, i: (bi, 0, 0))
    return pl.pallas_call(
        functools.partial(_inproj_kernel, n_norm=n_norm, tn=tn),
        out_shape=jax.ShapeDtypeStruct((b, s, n_out), BF16),
        grid=(b, s // tm),
        in_specs=[
            pl.BlockSpec((None, tm, d), lambda bi, i: (bi, i, 0)),
            pl.BlockSpec((1, d), lambda bi, i: (0, 0)),
            vec, vec,
            pl.BlockSpec((d, n_out), lambda bi, i: (0, 0)),
            pl.BlockSpec((1, n_out), lambda bi, i: (0, 0)),
        ],
        out_specs=pl.BlockSpec((None, tm, n_out), lambda bi, i: (bi, i, 0)),
        compiler_params=_cparams("parallel", "parallel"),
        name="inproj",
    )(x, g.reshape(1, d), sc, sh, w.astype(BF16), gn)


def _outproj_kernel(*refs, n_in):
    ins, ws = refs[:n_in], refs[n_in:2 * n_in]
    x_ref, g_ref, o_ref = refs[2 * n_in:]
    acc = jnp.dot(ins[0][...], ws[0][...], preferred_element_type=F32)
    for a, w in zip(ins[1:], ws[1:]):
        acc += jnp.dot(a[...], w[...], preferred_element_type=F32)
    o_ref[...] = x_ref[...] + g_ref[...] * acc


def _outproj_residual(parts, weights, x, gate):
    b, s, d = x.shape
    tm = _row_tile(s, 512)
    n_in = len(parts)
    in_specs = [pl.BlockSpec((None, tm, p.shape[-1]), lambda bi, i: (bi, i, 0)) for p in parts]
    in_specs += [pl.BlockSpec(w.shape, lambda bi, i: (0, 0)) for w in weights]
    in_specs += [pl.BlockSpec((None, tm, d), lambda bi, i: (bi, i, 0)),
                 pl.BlockSpec((None, 1, d), lambda bi, i: (bi, 0, 0))]
    return pl.pallas_call(
        functools.partial(_outproj_kernel, n_in=n_in),
        out_shape=jax.ShapeDtypeStruct((b, s, d), F32),
        grid=(b, s // tm),
        in_specs=in_specs,
        out_specs=pl.BlockSpec((None, tm, d), lambda bi, i: (bi, i, 0)),
        compiler_params=_cparams("parallel", "parallel"),
        name="outproj",
    )(*parts, *[w.astype(BF16) for w in weights], x, gate)


def _dot_nt(a, b):
    return lax.dot_general(a, b, (((1,), (1,)), ((), ())), preferred_element_type=F32)


def _sb_kernel(q_ref, k_ref, v_ref, o_ref, acc_ref, z_ref, *, tq, tk):
    i = pl.program_id(2)
    n_heads = LANES // HEAD_DIM
    n_diag = tq // tk
    lane = lax.broadcasted_iota(I32, (1, LANES), 1)
    r = lax.broadcasted_iota(I32, (tk, tk), 0)
    c = lax.broadcasted_iota(I32, (tk, tk), 1)
    later = jnp.where(r >= c, 1.0, 0.0).astype(BF16)
    row = lax.broadcasted_iota(I32, (tq, tk), 0)
    col = lax.broadcasted_iota(I32, (tq, tk), 1)
    q = q_ref[...]
    scale = jnp.asarray(HEAD_DIM ** -0.5, q.dtype)
    qs = [jnp.where((lane >= hh * HEAD_DIM) & (lane < (hh + 1) * HEAD_DIM), q, jnp.zeros_like(q)) * scale
          for hh in range(n_heads)]

    def scores(hh, j):
        return _dot_nt(qs[hh], k_ref[pl.ds(pl.multiple_of(j * tk, tk), tk), :])

    def tile(hh, j, z, carry, causal):
        fail = jnp.maximum(z, 0.0) + jnp.log(1.0 + jnp.exp(-jnp.abs(z)))
        if causal is not None:
            fail = jnp.where(causal, fail, 0.0)
        hi = fail.astype(BF16)
        lo = (fail - hi.astype(F32)).astype(BF16)
        incl = (jnp.dot(hi, later, preferred_element_type=F32)
                + jnp.dot(lo, later, preferred_element_type=F32))
        a = jnp.exp(z - (incl + carry))
        if causal is not None:
            a = jnp.where(causal, a, 0.0)
        vt = v_ref[pl.ds(pl.multiple_of(j * tk, tk), tk), :]
        acc_ref[hh] += jnp.dot(a.astype(BF16), vt, preferred_element_type=F32)
        return carry + incl[:, 0:1]

    carries = [jnp.zeros((tq, 1), F32)] * n_heads
    acc_ref[...] = jnp.zeros_like(acc_ref)
    for dj in reversed(range(n_diag)):
        causal = col + dj * tk < row
        carries = [tile(hh, i * n_diag + dj, scores(hh, i * n_diag + dj), carries[hh], causal)
                   for hh in range(n_heads)]

    assert n_diag % 2 == 0
    n_pairs = i * (n_diag // 2)
    first = i * n_diag - 1
    for hh in range(n_heads):
        z_ref[hh] = scores(hh, jnp.maximum(first, 0))

    def body(jj, carries):
        ja = first - 2 * jj
        za = [z_ref[hh] for hh in range(n_heads)]
        zb = [scores(hh, ja - 1) for hh in range(n_heads)]
        carries = [tile(hh, ja, za[hh], carries[hh], None) for hh in range(n_heads)]
        for hh in range(n_heads):
            z_ref[hh] = scores(hh, jnp.maximum(ja - 2, 0))
        return tuple(tile(hh, ja - 1, zb[hh], carries[hh], None) for hh in range(n_heads))

    lax.fori_loop(0, n_pairs, body, tuple(carries))
    o_ref[...] = jnp.where(lane < HEAD_DIM, acc_ref[0], acc_ref[1]).astype(o_ref.dtype)


def _stick_breaking(proj, width):
    b, s, _ = proj.shape
    tk = _row_tile(s, 256)
    tq = _row_tile(s, 2 * tk)
    nblk = width // LANES
    return pl.pallas_call(
        functools.partial(_sb_kernel, tq=tq, tk=tk),
        out_shape=jax.ShapeDtypeStruct((b, s, width), BF16),
        grid=(b, nblk, s // tq),
        in_specs=[
            pl.BlockSpec((None, tq, LANES), lambda bi, h, i: (bi, i, h)),
            pl.BlockSpec((None, s, LANES), lambda bi, h, i: (bi, 0, nblk + h)),
            pl.BlockSpec((None, s, LANES), lambda bi, h, i: (bi, 0, 2 * nblk + h)),
        ],
        out_specs=pl.BlockSpec((None, tq, LANES), lambda bi, h, i: (bi, i, h)),
        scratch_shapes=[pltpu.VMEM((LANES // HEAD_DIM, tq, LANES), F32),
                        pltpu.VMEM((LANES // HEAD_DIM, tq, tk), F32)],
        compiler_params=_cparams("parallel", "parallel", "parallel"),
        name="stick_breaking",
    )(proj, proj, proj)


def _conv_kernel(a_ref, g_ref, pa_ref, pg_ref, bglu_ref, dw_ref, dwb_ref, lng_ref, lnb_ref, o_ref, hbuf, *, ts, rc):
    i = pl.program_id(1)
    w = a_ref.shape[-1]

    def glu(a, g):
        a = a.astype(F32) + bglu_ref[:, :w]
        g = g.astype(F32) + bglu_ref[:, w:]
        return a * jax.nn.sigmoid(g)

    prev = glu(pa_ref[...], pg_ref[...])
    hbuf[0:CONV_HALO] = jnp.where(i > 0, prev, 0.0)
    hbuf[CONV_HALO:CONV_HALO + ts] = glu(a_ref[...], g_ref[...])
    off = CONV_HALO - (CONV_KERNEL - 1)
    for r0 in range(0, ts, rc):
        acc = jnp.zeros((rc, w), F32) + dwb_ref[...]
        for k in range(CONV_KERNEL):
            acc = acc + dw_ref[k:k + 1, :] * hbuf[r0 + off + k:r0 + off + k + rc, :]
        mu = jnp.mean(acc, axis=-1, keepdims=True)
        cen = acc - mu
        var = jnp.mean(cen * cen, axis=-1, keepdims=True)
        y = cen * lax.rsqrt(var + EPS) * lng_ref[...] + lnb_ref[...]
        o_ref[r0:r0 + rc, :] = (y * jax.nn.sigmoid(y)).astype(o_ref.dtype)


def _conformer_conv(proj, col0, width, b_glu, dw, dw_b, ln_g, ln_b):
    b, s, _ = proj.shape
    ts = _row_tile(s, 256)
    assert col0 % width == 0 and ts % CONV_HALO == 0
    cb = col0 // width
    hb = ts // CONV_HALO
    cur = lambda off: pl.BlockSpec((None, ts, width), lambda bi, i: (bi, i, cb + off))
    prev = lambda off: pl.BlockSpec((None, CONV_HALO, width),
                                    lambda bi, i: (bi, jnp.maximum(i * hb - 1, 0), cb + off))
    row = lambda n: pl.BlockSpec((1, n), lambda bi, i: (0, 0))
    return pl.pallas_call(
        functools.partial(_conv_kernel, ts=ts, rc=32),
        out_shape=jax.ShapeDtypeStruct((b, s, width), BF16),
        grid=(b, s // ts),
        in_specs=[cur(0), cur(1), prev(0), prev(1), row(2 * width),
                  pl.BlockSpec((CONV_KERNEL, width), lambda bi, i: (0, 0)),
                  row(width), row(width), row(width)],
        out_specs=pl.BlockSpec((None, ts, width), lambda bi, i: (bi, i, 0)),
        scratch_shapes=[pltpu.VMEM((CONV_HALO + ts, width), F32)],
        compiler_params=_cparams("parallel", "parallel"),
        name="conformer_conv",
    )(proj, proj, proj, proj, b_glu.reshape(1, -1), dw, dw_b.reshape(1, -1),
      ln_g.reshape(1, -1), ln_b.reshape(1, -1))


def _diff_kernel(slope_ref, q_ref, k_ref, vt_ref, lq1_ref, lk1_ref, lq2_ref, lk2_ref, subg_ref, o_ref,
                 acc_ref, s_ref, *, tq, tk, lam_init):
    h = pl.program_id(1)
    i = pl.program_id(2)
    n_diag = tq // tk
    slope = slope_ref[h]
    lane = lax.broadcasted_iota(I32, (1, LANES), 1)
    key = lax.broadcasted_iota(I32, (tk, tq), 0)
    qry = lax.broadcasted_iota(I32, (tk, tq), 1)
    rel = (qry - key).astype(F32)
    bias_past = -slope * rel
    q = q_ref[...]
    zero = jnp.zeros_like(q)
    qs = (jnp.where(lane < HEAD_DIM, q, zero), jnp.where(lane >= HEAD_DIM, q, zero))

    def scores(j):
        kt = k_ref[pl.ds(pl.multiple_of(j * tk, tk), tk), :]
        return [_dot_nt(kt, qs[n]) for n in range(2)]

    def tile(j, raw, state, bias, shift):
        vt = vt_ref[:, pl.ds(pl.multiple_of(j * tk, tk), tk)]
        new_state = []
        for n in range(2):
            m, l = state[n]
            s = raw[n] + bias
            m_new = jnp.maximum(m, jnp.max(s, axis=0, keepdims=True) + shift)
            alpha = jnp.exp(m - m_new)
            p = jnp.exp(s - (m_new - shift))
            l = alpha * l + jnp.sum(p, axis=0, keepdims=True)
            acc_ref[n] = alpha * acc_ref[n] + jnp.dot(vt, p.astype(BF16), preferred_element_type=F32)
            new_state.append((m_new, l))
        return tuple(new_state)

    acc_ref[...] = jnp.zeros_like(acc_ref)
    neg = jnp.full((1, tq), -jnp.inf, F32)
    zl = jnp.zeros((1, tq), F32)
    state = ((neg, zl), (neg, zl))
    for dj in range(n_diag):
        allowed = _div_pow2(key + dj * tk, CHUNK) <= _div_pow2(qry, CHUNK)
        bias = jnp.where(allowed, -slope * jnp.abs(rel - float(dj * tk)), -jnp.inf)
        state = tile(i * n_diag + dj, scores(i * n_diag + dj), state, bias, 0.0)

    assert n_diag % 2 == 0
    n_pairs = i * (n_diag // 2)
    first = i * n_diag - 1
    nxt = scores(jnp.maximum(first, 0))
    s_ref[0], s_ref[1] = nxt[0], nxt[1]

    def shift_of(j):
        return -slope * ((i * n_diag - j) * tk).astype(F32)

    def body(jj, state):
        ja = first - 2 * jj
        raw_a = [s_ref[0], s_ref[1]]
        raw_b = scores(ja - 1)
        state = tile(ja, raw_a, state, bias_past, shift_of(ja))
        nxt = scores(jnp.maximum(ja - 2, 0))
        s_ref[0], s_ref[1] = nxt[0], nxt[1]
        return tile(ja - 1, raw_b, state, bias_past, shift_of(ja - 1))

    (m1, l1), (m2, l2) = lax.fori_loop(0, n_pairs, body, state)
    lam = (jnp.exp(jnp.sum(lq1_ref[...] * lk1_ref[...], axis=-1, keepdims=True))
           - jnp.exp(jnp.sum(lq2_ref[...] * lk2_ref[...], axis=-1, keepdims=True)) + lam_init)
    o = acc_ref[0] / l1 - lam * (acc_ref[1] / l2)
    ms = jnp.mean(o * o, axis=0, keepdims=True)
    o = o * lax.rsqrt(ms + EPS) * (1.0 - lam_init)
    o_ref[...] = (o.T * subg_ref[...]).astype(o_ref.dtype)


def _diff_attention(proj, n_heads, lq1, lk1, lq2, lk2, sub_g, lam_init):
    b, s, _ = proj.shape
    tk = _row_tile(s, 256)
    tq = _row_tile(s, 2 * tk)
    assert tk % CHUNK == 0
    v_t = jnp.swapaxes(proj[:, :, 2 * n_heads * LANES:], 1, 2)
    slopes = jnp.exp2(-ALIBI_MAX_EXP * jnp.arange(1, n_heads + 1, dtype=F32) / n_heads)
    vec = lambda n: pl.BlockSpec((1, n), lambda bi, h, i, sl: (0, 0))
    return pl.pallas_call(
        functools.partial(_diff_kernel, tq=tq, tk=tk, lam_init=lam_init),
        out_shape=jax.ShapeDtypeStruct((b, s, n_heads * LANES), BF16),
        grid_spec=pltpu.PrefetchScalarGridSpec(
            num_scalar_prefetch=1,
            grid=(b, n_heads, s // tq),
            in_specs=[
                pl.BlockSpec((None, tq, LANES), lambda bi, h, i, sl: (bi, i, h)),
                pl.BlockSpec((None, s, LANES), lambda bi, h, i, sl: (bi, 0, n_heads + h)),
                pl.BlockSpec((None, LANES, s), lambda bi, h, i, sl: (bi, h, 0)),
                vec(HEAD_DIM), vec(HEAD_DIM), vec(HEAD_DIM), vec(HEAD_DIM), vec(LANES),
            ],
            out_specs=pl.BlockSpec((None, tq, LANES), lambda bi, h, i, sl: (bi, i, h)),
            scratch_shapes=[pltpu.VMEM((2, LANES, tq), F32), pltpu.VMEM((2, tk, tq), F32)],
        ),
        compiler_params=_cparams("parallel", "parallel", "parallel"),
        name="diff_attention",
    )(slopes, proj, proj, v_t, lq1.reshape(1, -1), lk1.reshape(1, -1), lq2.reshape(1, -1),
      lk2.reshape(1, -1), sub_g.reshape(1, -1))


ROUTE_E0, ROUTE_E1, ROUTE_W0, ROUTE_W1, ROUTE_R0, ROUTE_R1 = range(6)


def _pack_bf16_pair(lo, hi):
    lo_bits = pltpu.bitcast(lo.astype(BF16).astype(F32), U32) >> 16
    hi_bits = pltpu.bitcast(hi.astype(BF16).astype(F32), U32) & jnp.uint32(0xFFFF0000)
    return hi_bits | lo_bits


def _unpack_bf16_pair(u):
    lo = pltpu.bitcast(u << 16, F32).astype(BF16)
    hi = pltpu.bitcast(u & jnp.uint32(0xFFFF0000), F32).astype(BF16)
    return lo, hi


def _router_kernel(x_ref, g_ref, sc_ref, sh_ref, wr_ref, br_ref, hp_ref, rt_ref, cnt_ref, run_ref, *, tm):
    first = (pl.program_id(0) == 0) & (pl.program_id(1) == 0)

    @pl.when(first)
    def _():
        run_ref[...] = jnp.zeros_like(run_ref)

    h = _modulated_norm(x_ref[...], g_ref[...], sc_ref[...], sh_ref[...])
    half = h.shape[-1] // 2
    hp_ref[...] = _pack_bf16_pair(h[:, :half], h[:, half:])

    logits = jnp.dot(h, wr_ref[...], preferred_element_type=F32, precision=HIGHEST) + br_ref[...]
    lane = lax.broadcasted_iota(I32, (tm, LANES), 1)
    lanef = lane.astype(F32)
    ninf = -jnp.inf
    big = float(LANES)

    def first_argmax(vals):
        top = jnp.max(vals, axis=-1, keepdims=True)
        idx = jnp.min(jnp.where(vals == top, lanef, big), axis=-1, keepdims=True)
        return top, idx

    is_group = lane < N_GROUPS
    gl = jnp.where(is_group, logits, ninf)
    gmax, gidx = first_argmax(gl)
    g_w = 1.0 / jnp.sum(jnp.where(is_group, jnp.exp(gl - gmax), 0.0), axis=-1, keepdims=True)
    lane_group = _div_pow2(lane - N_GROUPS, EXPERTS_PER_GROUP).astype(F32)
    in_group = (lane >= N_GROUPS) & (lane < N_GROUPS + N_EXPERTS) & (lane_group == gidx)
    el = jnp.where(in_group, logits, ninf)
    v0, i0 = first_argmax(el)
    v1, i1 = first_argmax(jnp.where(lanef == i0, ninf, el))
    tt = jnp.exp(v1 - v0)
    w0 = g_w / (1.0 + tt)
    w1 = g_w * tt / (1.0 + tt)
    e0 = i0 - N_GROUPS
    e1 = i1 - N_GROUPS

    member = (lanef == e0) | (lanef == e1)
    r = lax.broadcasted_iota(I32, (tm, tm), 0)
    c = lax.broadcasted_iota(I32, (tm, tm), 1)
    earlier = jnp.where(c < r, 1.0, 0.0).astype(BF16)
    before = jnp.dot(earlier, jnp.where(member, 1.0, 0.0).astype(BF16), preferred_element_type=F32)
    before = before + run_ref[...]
    r0 = jnp.sum(jnp.where(lanef == e0, before, 0.0), axis=-1, keepdims=True)
    r1 = jnp.sum(jnp.where(lanef == e1, before, 0.0), axis=-1, keepdims=True)
    run_ref[...] += jnp.sum(jnp.where(member, 1.0, 0.0), axis=0, keepdims=True)
    cnt_ref[...] = run_ref[...]

    slab = jnp.zeros((tm, LANES), F32)
    for colv, val in ((ROUTE_E0, e0), (ROUTE_E1, e1), (ROUTE_W0, w0), (ROUTE_W1, w1), (ROUTE_R0, r0), (ROUTE_R1, r1)):
        slab = jnp.where(lane == colv, val, slab)
    rt_ref[...] = slab


def _router(x, g, sc, sh, w_group, b_group, w_router, b_router):
    b, s, d = x.shape
    tm = _row_tile(s, 256)
    wr = jnp.zeros((d, LANES), F32).at[:, :N_GROUPS].set(w_group).at[:, N_GROUPS:N_GROUPS + N_EXPERTS].set(w_router)
    br = jnp.zeros((1, LANES), F32).at[0, :N_GROUPS].set(b_group).at[0, N_GROUPS:N_GROUPS + N_EXPERTS].set(b_router)
    vec = pl.BlockSpec((None, 1, d), lambda bi, i: (bi, 0, 0))
    return pl.pallas_call(
        functools.partial(_router_kernel, tm=tm),
        out_shape=(jax.ShapeDtypeStruct((b, s, d // 2), U32),
                   jax.ShapeDtypeStruct((b, s, LANES), F32),
                   jax.ShapeDtypeStruct((1, LANES), F32)),
        grid=(b, s // tm),
        in_specs=[
            pl.BlockSpec((None, tm, d), lambda bi, i: (bi, i, 0)),
            pl.BlockSpec((1, d), lambda bi, i: (0, 0)),
            vec, vec,
            pl.BlockSpec((d, LANES), lambda bi, i: (0, 0)),
            pl.BlockSpec((1, LANES), lambda bi, i: (0, 0)),
        ],
        out_specs=(pl.BlockSpec((None, tm, d // 2), lambda bi, i: (bi, i, 0)),
                   pl.BlockSpec((None, tm, LANES), lambda bi, i: (bi, i, 0)),
                   pl.BlockSpec((1, LANES), lambda bi, i: (0, 0))),
        scratch_shapes=[pltpu.VMEM((1, LANES), F32)],
        compiler_params=_cparams("arbitrary", "arbitrary"),
        name="moe_router",
    )(x, g.reshape(1, d), sc, sh, wr, br)


def _zero_kernel(o_ref):
    o_ref[...] = jnp.zeros_like(o_ref)


def _zeros_u32(rows, cols):
    tr = _row_tile(rows, 2048)
    return pl.pallas_call(
        _zero_kernel,
        out_shape=jax.ShapeDtypeStruct((rows, cols), U32),
        grid=(rows // tr,),
        out_specs=pl.BlockSpec((tr, cols), lambda i: (i, 0)),
        compiler_params=_cparams("parallel"),
        name="moe_zero_slots",
    )()


def _dispatch_kernel(dest_ref, h_ref, xs_in_ref, xs_ref, sem, *, tm):
    del xs_in_ref
    base = pl.program_id(0) * tm

    def row_copy(r, slot):
        return pltpu.make_async_copy(h_ref.at[pl.ds(r, 1), :], xs_ref.at[pl.ds(slot, 1), :], sem)

    def body(r, carry):
        row_copy(r, dest_ref[2 * (base + r)]).start()
        row_copy(r, dest_ref[2 * (base + r) + 1]).start()
        return carry

    lax.fori_loop(0, tm, body, 0)

    def drain(r, carry):
        row_copy(r, 0).wait()
        row_copy(r, 0).wait()
        return carry

    lax.fori_loop(0, tm, drain, 0)


def _dispatch(dest, h_packed, n_slots):
    n, w = h_packed.shape
    tm = _row_tile(n, 256)
    return pl.pallas_call(
        functools.partial(_dispatch_kernel, tm=tm),
        out_shape=jax.ShapeDtypeStruct((n_slots, w), U32),
        grid_spec=pltpu.PrefetchScalarGridSpec(
            num_scalar_prefetch=1,
            grid=(n // tm,),
            in_specs=[pl.BlockSpec((tm, w), lambda i, d: (i, 0)),
                      pl.BlockSpec(memory_space=pl.ANY)],
            out_specs=pl.BlockSpec(memory_space=pl.ANY),
            scratch_shapes=[pltpu.SemaphoreType.DMA(())],
        ),
        input_output_aliases={2: 0},
        compiler_params=_cparams("arbitrary"),
        name="moe_dispatch",
    )(dest, h_packed, _zeros_u32(n_slots, w))


def _expert_kernel(be_ref, nv_ref, xs_ref, wg_ref, wu_ref, wd_ref, ys_ref, wg_b, wu_b, wd_b):
    blk = pl.program_id(0)
    valid = blk < nv_ref[0]
    changed = (blk == 0) | (be_ref[blk] != be_ref[jnp.maximum(blk - 1, 0)])

    @pl.when(valid & changed)
    def _():
        wg_b[...] = wg_ref[...].astype(BF16)
        wu_b[...] = wu_ref[...].astype(BF16)
        wd_b[...] = wd_ref[...].astype(BF16)

    @pl.when(valid)
    def _():
        x_lo, x_hi = _unpack_bf16_pair(xs_ref[...])
        half = x_lo.shape[-1]

        def proj(w):
            return (jnp.dot(x_lo, w[:half, :], preferred_element_type=F32)
                    + jnp.dot(x_hi, w[half:, :], preferred_element_type=F32))

        gate = proj(wg_b)
        hid = (gate * jax.nn.sigmoid(gate)) * proj(wu_b)
        ys_ref[...] = jnp.dot(hid.astype(BF16), wd_b[...], preferred_element_type=F32)

    @pl.when(jnp.logical_not(valid))
    def _():
        ys_ref[...] = jnp.zeros_like(ys_ref)


def _experts(block_e, n_valid, xs, w_gate, w_up, w_down, layer):
    n_slots, half = xs.shape
    d = 2 * half
    de = w_gate.shape[-1]
    n_blocks = n_slots // MOE_BLOCK
    last = lambda blk, nv: jnp.minimum(blk, nv[0] - 1)
    w_in = pl.BlockSpec((None, None, d, de), lambda blk, be, nv: (layer, be[last(blk, nv)], 0, 0))
    return pl.pallas_call(
        _expert_kernel,
        out_shape=jax.ShapeDtypeStruct((n_slots, d), F32),
        grid_spec=pltpu.PrefetchScalarGridSpec(
            num_scalar_prefetch=2,
            grid=(n_blocks,),
            in_specs=[pl.BlockSpec((MOE_BLOCK, half), lambda blk, be, nv: (last(blk, nv), 0)),
                      w_in, w_in,
                      pl.BlockSpec((None, None, de, d), lambda blk, be, nv: (layer, be[last(blk, nv)], 0, 0))],
            out_specs=pl.BlockSpec((MOE_BLOCK, d), lambda blk, be, nv: (blk, 0)),
            scratch_shapes=[pltpu.VMEM((d, de), BF16), pltpu.VMEM((d, de), BF16), pltpu.VMEM((de, d), BF16)],
        ),
        compiler_params=_cparams("arbitrary"),
        name="moe_experts",
    )(block_e, n_valid, xs, w_gate, w_up, w_down)


def _combine_kernel(dest_ref, x_ref, rt_ref, g_ref, ys_ref, o_ref, buf, sem, *, tm):
    base = pl.program_id(0) * tm

    def row_copy(k, r, slot):
        return pltpu.make_async_copy(ys_ref.at[pl.ds(slot, 1), :], buf.at[k, pl.ds(r, 1), :], sem)

    def body(r, carry):
        row_copy(0, r, dest_ref[2 * (base + r)]).start()
        row_copy(1, r, dest_ref[2 * (base + r) + 1]).start()
        return carry

    lax.fori_loop(0, tm, body, 0)

    def drain(r, carry):
        row_copy(0, r, 0).wait()
        row_copy(1, r, 0).wait()
        return carry

    lax.fori_loop(0, tm, drain, 0)
    rt = rt_ref[...]
    w0 = rt[:, ROUTE_W0:ROUTE_W0 + 1]
    w1 = rt[:, ROUTE_W1:ROUTE_W1 + 1]
    o_ref[...] = x_ref[...] + g_ref[...] * (w0 * buf[0] + w1 * buf[1])


def _combine(dest, x2, route, gate, ys, tiles_per_batch_of):
    n, d = x2.shape
    tm = _row_tile(n, 256)
    per_batch = tiles_per_batch_of(tm)
    return pl.pallas_call(
        functools.partial(_combine_kernel, tm=tm),
        out_shape=jax.ShapeDtypeStruct((n, d), F32),
        grid_spec=pltpu.PrefetchScalarGridSpec(
            num_scalar_prefetch=1,
            grid=(n // tm,),
            in_specs=[pl.BlockSpec((tm, d), lambda i, de: (i, 0)),
                      pl.BlockSpec((tm, LANES), lambda i, de: (i, 0)),
                      pl.BlockSpec((None, 1, d), lambda i, de: (i // per_batch, 0, 0)),
                      pl.BlockSpec(memory_space=pl.ANY)],
            out_specs=pl.BlockSpec((tm, d), lambda i, de: (i, 0)),
            scratch_shapes=[pltpu.VMEM((2, tm, d), F32), pltpu.SemaphoreType.DMA(())],
        ),
        compiler_params=_cparams("arbitrary"),
        name="moe_combine",
    )(dest, x2, route, gate, ys)


def _moe_layer(x, g, sc, sh, gate, w_group, b_group, w_router, b_router, w_gate, w_up, w_down, layer):
    b, s, d = x.shape
    n = b * s
    h_packed, route, counts = _router(x, g, sc, sh, w_group, b_group, w_router, b_router)
    route = route.reshape(n, LANES)
    counts = counts[0, :N_EXPERTS].astype(I32)
    padded = (counts + MOE_BLOCK - 1) // MOE_BLOCK * MOE_BLOCK
    pad_end = jnp.cumsum(padded)
    pad_start = pad_end - padded
    e_id = route[:, ROUTE_E0:ROUTE_E1 + 1].astype(I32)
    rank = route[:, ROUTE_R0:ROUTE_R1 + 1].astype(I32)
    dest = (pad_start[e_id] + rank).reshape(-1)
    n_blocks = -(-(n * 2) // MOE_BLOCK) + N_EXPERTS
    block_e = jnp.minimum(jnp.searchsorted(pad_end, jnp.arange(n_blocks, dtype=I32) * MOE_BLOCK, side='right'),
                          N_EXPERTS - 1).astype(I32)
    n_valid = (pad_end[-1:] // MOE_BLOCK).astype(I32)
    xs = _dispatch(dest, h_packed.reshape(n, d // 2), n_blocks * MOE_BLOCK)
    ys = _experts(block_e, n_valid, xs, w_gate, w_up, w_down, layer)
    out = _combine(dest, x.reshape(n, d), route, gate, ys, lambda tm: s // tm)
    return out.reshape(b, s, d)


def kernel(x, c, w_ada, b_ada, mix_norm_g, ffn_norm_g, even_w_in, conv_b_glu, conv_dw, conv_dw_b, conv_ln_g, conv_ln_b, even_w_out, odd_w_in, qk_norm_q, qk_norm_k, lambda_q1, lambda_k1, lambda_q2, lambda_k2, diff_sub_g, odd_w_out, moe_w_group, moe_b_group, moe_w_router, moe_b_router, moe_w_gate, moe_w_up, moe_w_down):
    depth = w_ada.shape[0]
    d = x.shape[-1]
    sb_width = d // 2
    conv_width = d // 2
    diff_heads = d // (2 * HEAD_DIM)
    qk_width = diff_heads * 2 * HEAD_DIM
    mod = _ada_mod(c, w_ada, b_ada)
    for layer in range(depth):
        sh1, sc1, g1, sh2, sc2, g2 = [m[:, None, :] for m in jnp.split(mod[layer], 6, axis=-1)]
        j = layer // 2
        if layer % 2 == 0:
            proj = _inproj(x, mix_norm_g[layer], sc1, sh1, even_w_in[j])
            a_out = _stick_breaking(proj, sb_width)
            b_out = _conformer_conv(proj, 3 * sb_width, conv_width, conv_b_glu[j], conv_dw[j], conv_dw_b[j],
                                    conv_ln_g[j], conv_ln_b[j])
            x = _outproj_residual([a_out, b_out], [even_w_out[j][:sb_width], even_w_out[j][sb_width:]], x, g1)
        else:
            lam_init = 0.8 - 0.6 * math.exp(-0.3 * layer)
            reps = qk_width // HEAD_DIM
            qk_gain = jnp.concatenate([jnp.tile(qk_norm_q[j] * HEAD_DIM ** -0.5, reps),
                                       jnp.tile(qk_norm_k[j], reps)])[None, :]
            proj = _inproj(x, mix_norm_g[layer], sc1, sh1, odd_w_in[j], qk_gain)
            o = _diff_attention(proj, diff_heads, lambda_q1[j], lambda_k1[j], lambda_q2[j], lambda_k2[j],
                                diff_sub_g[j], lam_init)
            x = _outproj_residual([o], [odd_w_out[j]], x, g1)
        x = _moe_layer(x, ffn_norm_g[layer], sc2, sh2, g2, moe_w_group[layer], moe_b_group[layer],
                       moe_w_router[layer], moe_b_router[layer], moe_w_gate, moe_w_up, moe_w_down, layer)
    return x
```

```python
import functools
import math

import jax
import jax.numpy as jnp
from jax import lax
from jax.experimental import pallas as pl
from jax.experimental.pallas import tpu as pltpu

F32 = jnp.float32
BF16 = jnp.bfloat16
U32 = jnp.uint32
I32 = jnp.int32

EPS = 1e-6
LANES = 128
HEAD_DIM = 64
CHUNK = 64
CONV_KERNEL = 31
CONV_HALO = 32
N_GROUPS = 4
EXPERTS_PER_GROUP = 8
N_EXPERTS = N_GROUPS * EXPERTS_PER_GROUP
MOE_BLOCK = 256
ALIBI_MAX_EXP = 8.0
VMEM_LIMIT_BYTES = 56 * 1024 * 1024
HIGHEST = lax.Precision.HIGHEST


def _cparams(*sem):
    return pltpu.CompilerParams(dimension_semantics=sem, vmem_limit_bytes=VMEM_LIMIT_BYTES)


def _div_pow2(v, n):
    shift = n.bit_length() - 1
    assert 1 << shift == n
    return lax.shift_right_arithmetic(v, jnp.int32(shift))


def _row_tile(n, want):
    t = min(n, want)
    assert n % t == 0
    return t


def _ada_kernel(c_ref, w_ref, b_ref, o_ref):
    c = c_ref[...]
    c_act = c * jax.nn.sigmoid(c)
    o_ref[...] = jnp.dot(c_act, w_ref[...], preferred_element_type=F32, precision=HIGHEST) + b_ref[...]


def _ada_mod(c, w_ada, b_ada):
    depth, d, d6 = w_ada.shape
    b = c.shape[0]
    rows = 8
    c_pad = jnp.zeros((rows, d), F32).at[:b].set(c)
    tn = _row_tile(d6, 1536)
    out = pl.pallas_call(
        _ada_kernel,
        out_shape=jax.ShapeDtypeStruct((depth, rows, d6), F32),
        grid=(depth, d6 // tn),
        in_specs=[
            pl.BlockSpec((rows, d), lambda l, j: (0, 0)),
            pl.BlockSpec((None, d, tn), lambda l, j: (l, 0, j)),
            pl.BlockSpec((None, 1, tn), lambda l, j: (l, 0, j)),
        ],
        out_specs=pl.BlockSpec((None, rows, tn), lambda l, j: (l, 0, j)),
        compiler_params=_cparams("parallel", "parallel"),
        name="ada_mod",
    )(c_pad, w_ada, b_ada.reshape(depth, 1, d6))
    return out[:, :b]


def _modulated_norm(x, g, sc, sh):
    ms = jnp.mean(x * x, axis=-1, keepdims=True)
    return (x * lax.rsqrt(ms + EPS) * g) * (1.0 + sc) + sh


def _inproj_kernel(x_ref, g_ref, sc_ref, sh_ref, w_ref, gn_ref, o_ref, *, n_gain, head_norm, tn):
    h = _modulated_norm(x_ref[...], g_ref[...], sc_ref[...], sh_ref[...]).astype(BF16)
    n_out = o_ref.shape[-1]
    if head_norm:
        r = _div_pow2(lax.broadcasted_iota(I32, (tn, tn), 0), HEAD_DIM)
        c = _div_pow2(lax.broadcasted_iota(I32, (tn, tn), 1), HEAD_DIM)
        group_ones = jnp.where(r == c, 1.0, 0.0).astype(BF16)
    for c0 in range(0, n_out, tn):
        y = jnp.dot(h, w_ref[:, c0:c0 + tn], preferred_element_type=F32)
        if c0 < n_gain:
            if head_norm:
                sq = y * y
                hi = sq.astype(BF16)
                lo = (sq - hi.astype(F32)).astype(BF16)
                gs = (jnp.dot(hi, group_ones, preferred_element_type=F32)
                      + jnp.dot(lo, group_ones, preferred_element_type=F32))
                y = y * lax.rsqrt(gs * (1.0 / HEAD_DIM) + EPS)
            y = y * gn_ref[:, c0:c0 + tn]
        o_ref[:, c0:c0 + tn] = y.astype(o_ref.dtype)


def _inproj(x, g, sc, sh, w, gain, head_norm):
    b, s, d = x.shape
    n_out = w.shape[1]
    tm = _row_tile(s, 512)
    tn = 256
    n_gain = gain.shape[1]
    assert n_out % tn == 0 and n_gain % tn == 0
    gn = jnp.zeros((1, n_out), F32).at[:, :n_gain].set(gain)
    vec = pl.BlockSpec((None, 1, d), lambda bi, i: (bi, 0, 0))
    return pl.pallas_call(
        functools.partial(_inproj_kernel, n_gain=n_gain, head_norm=head_norm, tn=tn),
        out_shape=jax.ShapeDtypeStruct((b, s, n_out), BF16),
        grid=(b, s // tm),
        in_specs=[
            pl.BlockSpec((None, tm, d), lambda bi, i: (bi, i, 0)),
            pl.BlockSpec((1, d), lambda bi, i: (0, 0)),
            vec, vec,
            pl.BlockSpec((d, n_out), lambda bi, i: (0, 0)),
            pl.BlockSpec((1, n_out), lambda bi, i: (0, 0)),
        ],
        out_specs=pl.BlockSpec((None, tm, n_out), lambda bi, i: (bi, i, 0)),
        compiler_params=_cparams("parallel", "parallel"),
        name="inproj",
    )(x, g.reshape(1, d), sc, sh, w.astype(BF16), gn)


def _outproj_kernel(*refs, n_in):
    ins, ws = refs[:n_in], refs[n_in:2 * n_in]
    x_ref, g_ref, o_ref = refs[2 * n_in:]
    acc = jnp.dot(ins[0][...], ws[0][...], preferred_element_type=F32)
    for a, w in zip(ins[1:], ws[1:]):
        acc += jnp.dot(a[...], w[...], preferred_element_type=F32)
    o_ref[...] = x_ref[...] + g_ref[...] * acc


def _outproj_residual(parts, weights, x, gate):
    b, s, d = x.shape
    tm = _row_tile(s, 512)
    n_in = len(parts)
    in_specs = [pl.BlockSpec((None, tm, p.shape[-1]), lambda bi, i: (bi, i, 0)) for p in parts]
    in_specs += [pl.BlockSpec(w.shape, lambda bi, i: (0, 0)) for w in weights]
    in_specs += [pl.BlockSpec((None, tm, d), lambda bi, i: (bi, i, 0)),
                 pl.BlockSpec((None, 1, d), lambda bi, i: (bi, 0, 0))]
    return pl.pallas_call(
        functools.partial(_outproj_kernel, n_in=n_in),
        out_shape=jax.ShapeDtypeStruct((b, s, d), F32),
        grid=(b, s // tm),
        in_specs=in_specs,
        out_specs=pl.BlockSpec((None, tm, d), lambda bi, i: (bi, i, 0)),
        compiler_params=_cparams("parallel", "parallel"),
        name="outproj",
    )(*parts, *[w.astype(BF16) for w in weights], x, gate)


LOG2E = 1.4426950408889634
MASKED_LOG2 = -1e30


def _dot_nt(a, b):
    return lax.dot_general(a, b, (((1,), (1,)), ((), ())), preferred_element_type=F32)


def _sb_kernel(q_ref, k_ref, v_ref, o_ref, acc_ref, z_ref, w_ref, *, tq, tk):
    i = pl.program_id(2)
    n_heads = LANES // HEAD_DIM
    n_diag = tq // tk
    lane = lax.broadcasted_iota(I32, (1, LANES), 1)
    r = lax.broadcasted_iota(I32, (tk, tk), 0)
    c = lax.broadcasted_iota(I32, (tk, tk), 1)
    later = jnp.where(r >= c, 1.0, 0.0).astype(BF16)
    row = lax.broadcasted_iota(I32, (tq, tk), 0)
    col = lax.broadcasted_iota(I32, (tq, tk), 1)
    q = q_ref[...]
    qs = [jnp.where((lane >= hh * HEAD_DIM) & (lane < (hh + 1) * HEAD_DIM), q, jnp.zeros_like(q))
          for hh in range(n_heads)]

    def key_tile(ref, j):
        return ref[pl.ds(pl.multiple_of(j * tk, tk), tk), :]

    def step(slot, j, j_next, j_prev, state, causal):
        carry, carry_prev = state
        new_carry = []
        for hh in range(n_heads):
            a = jnp.exp2(w_ref[1 - slot, hh] - carry_prev[hh])
            acc_ref[hh] += jnp.dot(a.astype(BF16), key_tile(v_ref, j_prev), preferred_element_type=F32)
        for hh in range(n_heads):
            z = z_ref[slot, hh]
            neg_abs = pltpu.bitcast(pltpu.bitcast(z, U32) | jnp.uint32(0x80000000), F32)
            fail = jnp.maximum(z, 0.0) + jnp.log2(1.0 + jnp.exp2(neg_abs))
            if causal is not None:
                fail = jnp.where(causal, fail, 0.0)
            incl = jnp.dot(fail.astype(BF16), later, preferred_element_type=F32)
            w = z - incl
            if causal is not None:
                w = jnp.where(causal, w, MASKED_LOG2)
            w_ref[slot, hh] = w
            new_carry.append(carry[hh] + incl[:, 0:1])
        for hh in range(n_heads):
            z_ref[1 - slot, hh] = _dot_nt(qs[hh], key_tile(k_ref, j_next))
        return tuple(new_carry), carry

    assert n_diag == 2
    top = i * n_diag + 1
    acc_ref[...] = jnp.zeros_like(acc_ref)
    w_ref[1] = jnp.full(w_ref.shape[1:], MASKED_LOG2, F32)
    for hh in range(n_heads):
        z_ref[0, hh] = _dot_nt(qs[hh], key_tile(k_ref, top))
    zero = tuple(jnp.zeros((tq, 1), F32) for _ in range(n_heads))
    state = (zero, zero)
    state = step(0, top, top - 1, top, state, col + tk < row)
    state = step(1, top - 1, jnp.maximum(top - 2, 0), top, state, col < row)

    def body(jj, state):
        ja = top - 2 - 2 * jj
        state = step(0, ja, ja - 1, ja + 1, state, None)
        return step(1, ja - 1, jnp.maximum(ja - 2, 0), ja, state, None)

    carry, carry_prev = lax.fori_loop(0, i, body, state)
    for hh in range(n_heads):
        a = jnp.exp2(w_ref[1, hh] - carry_prev[hh])
        acc_ref[hh] += jnp.dot(a.astype(BF16), key_tile(v_ref, 0), preferred_element_type=F32)
    o_ref[...] = jnp.where(lane < HEAD_DIM, acc_ref[0], acc_ref[1]).astype(o_ref.dtype)


def _stick_breaking(proj, width):
    b, s, _ = proj.shape
    tk = _row_tile(s, 256)
    tq = _row_tile(s, 2 * tk)
    nblk = width // LANES
    return pl.pallas_call(
        functools.partial(_sb_kernel, tq=tq, tk=tk),
        out_shape=jax.ShapeDtypeStruct((b, s, width), BF16),
        grid=(b, nblk, s // tq),
        in_specs=[
            pl.BlockSpec((None, tq, LANES), lambda bi, h, i: (bi, i, h)),
            pl.BlockSpec((None, s, LANES), lambda bi, h, i: (bi, 0, nblk + h)),
            pl.BlockSpec((None, s, LANES), lambda bi, h, i: (bi, 0, 2 * nblk + h)),
        ],
        out_specs=pl.BlockSpec((None, tq, LANES), lambda bi, h, i: (bi, i, h)),
        scratch_shapes=[pltpu.VMEM((LANES // HEAD_DIM, tq, LANES), F32),
                        pltpu.VMEM((2, LANES // HEAD_DIM, tq, tk), F32),
                        pltpu.VMEM((2, LANES // HEAD_DIM, tq, tk), F32)],
        compiler_params=_cparams("parallel", "parallel", "parallel"),
        name="stick_breaking",
    )(proj, proj, proj)


def _conv_kernel(a_ref, g_ref, pa_ref, pg_ref, bglu_ref, dw_ref, dwb_ref, lng_ref, lnb_ref, o_ref, hbuf, *, ts, rc):
    i = pl.program_id(1)
    w = a_ref.shape[-1]

    def glu(a, g):
        a = a.astype(F32) + bglu_ref[:, :w]
        g = g.astype(F32) + bglu_ref[:, w:]
        return a * jax.nn.sigmoid(g)

    prev = glu(pa_ref[...], pg_ref[...])
    hbuf[0:CONV_HALO] = jnp.where(i > 0, prev, 0.0)
    hbuf[CONV_HALO:CONV_HALO + ts] = glu(a_ref[...], g_ref[...])
    off = CONV_HALO - (CONV_KERNEL - 1)
    for r0 in range(0, ts, rc):
        acc = jnp.zeros((rc, w), F32) + dwb_ref[...]
        for k in range(CONV_KERNEL):
            acc = acc + dw_ref[k:k + 1, :] * hbuf[r0 + off + k:r0 + off + k + rc, :]
        mu = jnp.mean(acc, axis=-1, keepdims=True)
        cen = acc - mu
        var = jnp.mean(cen * cen, axis=-1, keepdims=True)
        y = cen * lax.rsqrt(var + EPS) * lng_ref[...] + lnb_ref[...]
        o_ref[r0:r0 + rc, :] = (y * jax.nn.sigmoid(y)).astype(o_ref.dtype)


def _conformer_conv(proj, col0, width, b_glu, dw, dw_b, ln_g, ln_b):
    b, s, _ = proj.shape
    ts = _row_tile(s, 256)
    assert col0 % width == 0 and ts % CONV_HALO == 0
    cb = col0 // width
    hb = ts // CONV_HALO
    cur = lambda off: pl.BlockSpec((None, ts, width), lambda bi, i: (bi, i, cb + off))
    prev = lambda off: pl.BlockSpec((None, CONV_HALO, width),
                                    lambda bi, i: (bi, jnp.maximum(i * hb - 1, 0), cb + off))
    row = lambda n: pl.BlockSpec((1, n), lambda bi, i: (0, 0))
    return pl.pallas_call(
        functools.partial(_conv_kernel, ts=ts, rc=32),
        out_shape=jax.ShapeDtypeStruct((b, s, width), BF16),
        grid=(b, s // ts),
        in_specs=[cur(0), cur(1), prev(0), prev(1), row(2 * width),
                  pl.BlockSpec((CONV_KERNEL, width), lambda bi, i: (0, 0)),
                  row(width), row(width), row(width)],
        out_specs=pl.BlockSpec((None, ts, width), lambda bi, i: (bi, i, 0)),
        scratch_shapes=[pltpu.VMEM((CONV_HALO + ts, width), F32)],
        compiler_params=_cparams("parallel", "parallel"),
        name="conformer_conv",
    )(proj, proj, proj, proj, b_glu.reshape(1, -1), dw, dw_b.reshape(1, -1),
      ln_g.reshape(1, -1), ln_b.reshape(1, -1))


def _diff_kernel(slope_ref, q_ref, k_ref, vt_ref, lq1_ref, lk1_ref, lq2_ref, lk2_ref, subg_ref, o_ref,
                 acc_ref, s_ref, p_ref, *, tq, tk, lam_init):
    h = pl.program_id(1)
    i = pl.program_id(2)
    n_diag = tq // tk
    slope = slope_ref[h]
    lane = lax.broadcasted_iota(I32, (1, LANES), 1)
    key = lax.broadcasted_iota(I32, (tk, tq), 0)
    qry = lax.broadcasted_iota(I32, (tk, tq), 1)
    rel = (qry - key).astype(F32)
    bias_past = -slope * rel
    q = q_ref[...]
    zero = jnp.zeros_like(q)
    qs = (jnp.where(lane < HEAD_DIM, q, zero), jnp.where(lane >= HEAD_DIM, q, zero))

    def scores(slot, j, bias):
        kt = k_ref[pl.ds(pl.multiple_of(j * tk, tk), tk), :]
        tile_max = []
        for n in range(2):
            s = _dot_nt(kt, qs[n]) + bias
            s_ref[slot, n] = s
            tile_max.append(jnp.max(s, axis=0, keepdims=True))
        return tuple(tile_max)

    def weighted_values(slot, j, alpha):
        vt = vt_ref[:, pl.ds(pl.multiple_of(j * tk, tk), tk)]
        for n in range(2):
            acc_ref[n] = alpha[n] * acc_ref[n] + jnp.dot(vt, p_ref[slot, n], preferred_element_type=F32)

    def step(slot, j_next, j_prev, state, bias_next, shift):
        stats, alpha_prev, tile_max = state
        weighted_values(1 - slot, j_prev, alpha_prev)
        new_stats, alphas = [], []
        for n in range(2):
            m, l = stats[n]
            m_new = jnp.maximum(m, tile_max[n] + shift)
            alpha = jnp.exp2(m - m_new)
            p = jnp.exp2(s_ref[slot, n] - (m_new - shift))
            p_ref[slot, n] = p.astype(BF16)
            new_stats.append((m_new, alpha * l + jnp.sum(p, axis=0, keepdims=True)))
            alphas.append(alpha)
        return tuple(new_stats), tuple(alphas), scores(1 - slot, j_next, bias_next)

    def diag_bias(dj):
        allowed = _div_pow2(key + dj * tk, CHUNK) <= _div_pow2(qry, CHUNK)
        return jnp.where(allowed, -slope * jnp.abs(rel - float(dj * tk)), -jnp.inf)

    def shift_of(j):
        return -slope * ((i * n_diag - j) * tk).astype(F32)

    assert n_diag == 2
    top = i * n_diag + 1
    acc_ref[...] = jnp.zeros_like(acc_ref)
    p_ref[1] = jnp.zeros(p_ref.shape[1:], BF16)
    floor = jnp.full((1, tq), MASKED_LOG2, F32)
    zl = jnp.zeros((1, tq), F32)
    one = jnp.ones((1, tq), F32)
    state = (((floor, zl), (floor, zl)), (one, one), scores(0, top, diag_bias(1)))
    state = step(0, top - 1, top, state, diag_bias(0), 0.0)
    state = step(1, jnp.maximum(top - 2, 0), top, state, bias_past, 0.0)

    def body(jj, state):
        ja = top - 2 - 2 * jj
        state = step(0, ja - 1, ja + 1, state, bias_past, shift_of(ja))
        return step(1, jnp.maximum(ja - 2, 0), ja, state, bias_past, shift_of(ja - 1))

    ((m1, l1), (m2, l2)), alpha_last, _ = lax.fori_loop(0, i, body, state)
    weighted_values(1, 0, alpha_last)
    lam = (jnp.exp(jnp.sum(lq1_ref[...] * lk1_ref[...], axis=-1, keepdims=True))
           - jnp.exp(jnp.sum(lq2_ref[...] * lk2_ref[...], axis=-1, keepdims=True)) + lam_init)
    o = acc_ref[0] / l1 - lam * (acc_ref[1] / l2)
    ms = jnp.mean(o * o, axis=0, keepdims=True)
    o = o * lax.rsqrt(ms + EPS) * (1.0 - lam_init)
    o_ref[...] = (o.T * subg_ref[...]).astype(o_ref.dtype)


def _diff_attention(proj, n_heads, lq1, lk1, lq2, lk2, sub_g, lam_init):
    b, s, _ = proj.shape
    tk = _row_tile(s, 256)
    tq = _row_tile(s, 2 * tk)
    assert tk % CHUNK == 0
    v_t = jnp.swapaxes(proj[:, :, 2 * n_heads * LANES:], 1, 2)
    slopes = LOG2E * jnp.exp2(-ALIBI_MAX_EXP * jnp.arange(1, n_heads + 1, dtype=F32) / n_heads)
    vec = lambda n: pl.BlockSpec((1, n), lambda bi, h, i, sl: (0, 0))
    return pl.pallas_call(
        functools.partial(_diff_kernel, tq=tq, tk=tk, lam_init=lam_init),
        out_shape=jax.ShapeDtypeStruct((b, s, n_heads * LANES), BF16),
        grid_spec=pltpu.PrefetchScalarGridSpec(
            num_scalar_prefetch=1,
            grid=(b, n_heads, s // tq),
            in_specs=[
                pl.BlockSpec((None, tq, LANES), lambda bi, h, i, sl: (bi, i, h)),
                pl.BlockSpec((None, s, LANES), lambda bi, h, i, sl: (bi, 0, n_heads + h)),
                pl.BlockSpec((None, LANES, s), lambda bi, h, i, sl: (bi, h, 0)),
                vec(HEAD_DIM), vec(HEAD_DIM), vec(HEAD_DIM), vec(HEAD_DIM), vec(LANES),
            ],
            out_specs=pl.BlockSpec((None, tq, LANES), lambda bi, h, i, sl: (bi, i, h)),
            scratch_shapes=[pltpu.VMEM((2, LANES, tq), F32), pltpu.VMEM((2, 2, tk, tq), F32),
                            pltpu.VMEM((2, 2, tk, tq), BF16)],
        ),
        compiler_params=_cparams("parallel", "parallel", "parallel"),
        name="diff_attention",
    )(slopes, proj, proj, v_t, lq1.reshape(1, -1), lk1.reshape(1, -1), lq2.reshape(1, -1),
      lk2.reshape(1, -1), sub_g.reshape(1, -1))


ROUTE_E0, ROUTE_E1, ROUTE_W0, ROUTE_W1, ROUTE_R0, ROUTE_R1 = range(6)


def _pack_bf16_pair(lo, hi):
    lo_bits = pltpu.bitcast(lo.astype(BF16).astype(F32), U32) >> 16
    hi_bits = pltpu.bitcast(hi.astype(BF16).astype(F32), U32) & jnp.uint32(0xFFFF0000)
    return hi_bits | lo_bits


def _unpack_bf16_pair(u):
    lo = pltpu.bitcast(u << 16, F32).astype(BF16)
    hi = pltpu.bitcast(u & jnp.uint32(0xFFFF0000), F32).astype(BF16)
    return lo, hi


def _router_kernel(x_ref, g_ref, sc_ref, sh_ref, wr_ref, br_ref, hp_ref, rt_ref, cnt_ref, run_ref, *, tm):
    first = (pl.program_id(0) == 0) & (pl.program_id(1) == 0)

    @pl.when(first)
    def _():
        run_ref[...] = jnp.zeros_like(run_ref)

    h = _modulated_norm(x_ref[...], g_ref[...], sc_ref[...], sh_ref[...])
    half = h.shape[-1] // 2
    hp_ref[...] = _pack_bf16_pair(h[:, :half], h[:, half:])

    logits = jnp.dot(h, wr_ref[...], preferred_element_type=F32, precision=HIGHEST) + br_ref[...]
    lane = lax.broadcasted_iota(I32, (tm, LANES), 1)
    lanef = lane.astype(F32)
    ninf = -jnp.inf
    big = float(LANES)

    def first_argmax(vals):
        top = jnp.max(vals, axis=-1, keepdims=True)
        idx = jnp.min(jnp.where(vals == top, lanef, big), axis=-1, keepdims=True)
        return top, idx

    is_group = lane < N_GROUPS
    gl = jnp.where(is_group, logits, ninf)
    gmax, gidx = first_argmax(gl)
    g_w = 1.0 / jnp.sum(jnp.where(is_group, jnp.exp(gl - gmax), 0.0), axis=-1, keepdims=True)
    lane_group = _div_pow2(lane - N_GROUPS, EXPERTS_PER_GROUP).astype(F32)
    in_group = (lane >= N_GROUPS) & (lane < N_GROUPS + N_EXPERTS) & (lane_group == gidx)
    el = jnp.where(in_group, logits, ninf)
    v0, i0 = first_argmax(el)
    v1, i1 = first_argmax(jnp.where(lanef == i0, ninf, el))
    tt = jnp.exp(v1 - v0)
    w0 = g_w / (1.0 + tt)
    w1 = g_w * tt / (1.0 + tt)
    e0 = i0 - N_GROUPS
    e1 = i1 - N_GROUPS

    member = (lanef == e0) | (lanef == e1)
    r = lax.broadcasted_iota(I32, (tm, tm), 0)
    c = lax.broadcasted_iota(I32, (tm, tm), 1)
    earlier = jnp.where(c < r, 1.0, 0.0).astype(BF16)
    before = jnp.dot(earlier, jnp.where(member, 1.0, 0.0).astype(BF16), preferred_element_type=F32)
    before = before + run_ref[...]
    r0 = jnp.sum(jnp.where(lanef == e0, before, 0.0), axis=-1, keepdims=True)
    r1 = jnp.sum(jnp.where(lanef == e1, before, 0.0), axis=-1, keepdims=True)
    run_ref[...] += jnp.sum(jnp.where(member, 1.0, 0.0), axis=0, keepdims=True)
    cnt_ref[...] = run_ref[...]

    slab = jnp.zeros((tm, LANES), F32)
    for colv, val in ((ROUTE_E0, e0), (ROUTE_E1, e1), (ROUTE_W0, w0), (ROUTE_W1, w1), (ROUTE_R0, r0), (ROUTE_R1, r1)):
        slab = jnp.where(lane == colv, val, slab)
    rt_ref[...] = slab


def _router(x, g, sc, sh, w_group, b_group, w_router, b_router):
    b, s, d = x.shape
    tm = _row_tile(s, 256)
    wr = jnp.zeros((d, LANES), F32).at[:, :N_GROUPS].set(w_group).at[:, N_GROUPS:N_GROUPS + N_EXPERTS].set(w_router)
    br = jnp.zeros((1, LANES), F32).at[0, :N_GROUPS].set(b_group).at[0, N_GROUPS:N_GROUPS + N_EXPERTS].set(b_router)
    vec = pl.BlockSpec((None, 1, d), lambda bi, i: (bi, 0, 0))
    return pl.pallas_call(
        functools.partial(_router_kernel, tm=tm),
        out_shape=(jax.ShapeDtypeStruct((b, s, d // 2), U32),
                   jax.ShapeDtypeStruct((b, s, LANES), F32),
                   jax.ShapeDtypeStruct((1, LANES), F32)),
        grid=(b, s // tm),
        in_specs=[
            pl.BlockSpec((None, tm, d), lambda bi, i: (bi, i, 0)),
            pl.BlockSpec((1, d), lambda bi, i: (0, 0)),
            vec, vec,
            pl.BlockSpec((d, LANES), lambda bi, i: (0, 0)),
            pl.BlockSpec((1, LANES), lambda bi, i: (0, 0)),
        ],
        out_specs=(pl.BlockSpec((None, tm, d // 2), lambda bi, i: (bi, i, 0)),
                   pl.BlockSpec((None, tm, LANES), lambda bi, i: (bi, i, 0)),
                   pl.BlockSpec((1, LANES), lambda bi, i: (0, 0))),
        scratch_shapes=[pltpu.VMEM((1, LANES), F32)],
        compiler_params=_cparams("arbitrary", "arbitrary"),
        name="moe_router",
    )(x, g.reshape(1, d), sc, sh, wr, br)


def _zero_kernel(o_ref):
    o_ref[...] = jnp.zeros_like(o_ref)


def _zeros_u32(rows, cols):
    tr = _row_tile(rows, 2048)
    return pl.pallas_call(
        _zero_kernel,
        out_shape=jax.ShapeDtypeStruct((rows, cols), U32),
        grid=(rows // tr,),
        out_specs=pl.BlockSpec((tr, cols), lambda i: (i, 0)),
        compiler_params=_cparams("parallel"),
        name="moe_zero_slots",
    )()


def _dispatch_kernel(dest_ref, h_ref, xs_in_ref, xs_ref, sem, *, tm):
    del xs_in_ref
    base = pl.program_id(0) * tm

    def row_copy(r, slot):
        return pltpu.make_async_copy(h_ref.at[pl.ds(r, 1), :], xs_ref.at[pl.ds(slot, 1), :], sem)

    def body(r, carry):
        row_copy(r, dest_ref[2 * (base + r)]).start()
        row_copy(r, dest_ref[2 * (base + r) + 1]).start()
        return carry

    lax.fori_loop(0, tm, body, 0)

    def drain(r, carry):
        row_copy(r, 0).wait()
        row_copy(r, 0).wait()
        return carry

    lax.fori_loop(0, tm, drain, 0)


def _dispatch(dest, h_packed, n_slots):
    n, w = h_packed.shape
    tm = _row_tile(n, 256)
    return pl.pallas_call(
        functools.partial(_dispatch_kernel, tm=tm),
        out_shape=jax.ShapeDtypeStruct((n_slots, w), U32),
        grid_spec=pltpu.PrefetchScalarGridSpec(
            num_scalar_prefetch=1,
            grid=(n // tm,),
            in_specs=[pl.BlockSpec((tm, w), lambda i, d: (i, 0)),
                      pl.BlockSpec(memory_space=pl.ANY)],
            out_specs=pl.BlockSpec(memory_space=pl.ANY),
            scratch_shapes=[pltpu.SemaphoreType.DMA(())],
        ),
        input_output_aliases={2: 0},
        compiler_params=_cparams("arbitrary"),
        name="moe_dispatch",
    )(dest, h_packed, _zeros_u32(n_slots, w))


def _expert_kernel(be_ref, nv_ref, xs_ref, wg_ref, wu_ref, wd_ref, ys_ref, wg_b, wu_b, wd_b):
    blk = pl.program_id(0)
    valid = blk < nv_ref[0]
    changed = (blk == 0) | (be_ref[blk] != be_ref[jnp.maximum(blk - 1, 0)])

    @pl.when(valid & changed)
    def _():
        wg_b[...] = wg_ref[...].astype(BF16)
        wu_b[...] = wu_ref[...].astype(BF16)
        wd_b[...] = wd_ref[...].astype(BF16)

    @pl.when(valid)
    def _():
        x_lo, x_hi = _unpack_bf16_pair(xs_ref[...])
        half = x_lo.shape[-1]

        def proj(w):
            return (jnp.dot(x_lo, w[:half, :], preferred_element_type=F32)
                    + jnp.dot(x_hi, w[half:, :], preferred_element_type=F32))

        gate = proj(wg_b)
        hid = (gate * jax.nn.sigmoid(gate)) * proj(wu_b)
        ys_ref[...] = jnp.dot(hid.astype(BF16), wd_b[...], preferred_element_type=F32)

    @pl.when(jnp.logical_not(valid))
    def _():
        ys_ref[...] = jnp.zeros_like(ys_ref)


def _experts(block_e, n_valid, xs, w_gate, w_up, w_down, layer):
    n_slots, half = xs.shape
    d = 2 * half
    de = w_gate.shape[-1]
    n_blocks = n_slots // MOE_BLOCK
    last = lambda blk, nv: jnp.minimum(blk, nv[0] - 1)
    w_in = pl.BlockSpec((None, None, d, de), lambda blk, be, nv: (layer, be[last(blk, nv)], 0, 0))
    return pl.pallas_call(
        _expert_kernel,
        out_shape=jax.ShapeDtypeStruct((n_slots, d), F32),
        grid_spec=pltpu.PrefetchScalarGridSpec(
            num_scalar_prefetch=2,
            grid=(n_blocks,),
            in_specs=[pl.BlockSpec((MOE_BLOCK, half), lambda blk, be, nv: (last(blk, nv), 0)),
                      w_in, w_in,
                      pl.BlockSpec((None, None, de, d), lambda blk, be, nv: (layer, be[last(blk, nv)], 0, 0))],
            out_specs=pl.BlockSpec((MOE_BLOCK, d), lambda blk, be, nv: (blk, 0)),
            scratch_shapes=[pltpu.VMEM((d, de), BF16), pltpu.VMEM((d, de), BF16), pltpu.VMEM((de, d), BF16)],
        ),
        compiler_params=_cparams("arbitrary"),
        name="moe_experts",
    )(block_e, n_valid, xs, w_gate, w_up, w_down)


def _combine_kernel(dest_ref, x_ref, rt_ref, g_ref, ys_ref, o_ref, buf, sem, *, tm):
    base = pl.program_id(0) * tm

    def row_copy(k, r, slot):
        return pltpu.make_async_copy(ys_ref.at[pl.ds(slot, 1), :], buf.at[k, pl.ds(r, 1), :], sem)

    def body(r, carry):
        row_copy(0, r, dest_ref[2 * (base + r)]).start()
        row_copy(1, r, dest_ref[2 * (base + r) + 1]).start()
        return carry

    lax.fori_loop(0, tm, body, 0)

    def drain(r, carry):
        row_copy(0, r, 0).wait()
        row_copy(1, r, 0).wait()
        return carry

    lax.fori_loop(0, tm, drain, 0)
    rt = rt_ref[...]
    w0 = rt[:, ROUTE_W0:ROUTE_W0 + 1]
    w1 = rt[:, ROUTE_W1:ROUTE_W1 + 1]
    o_ref[...] = x_ref[...] + g_ref[...] * (w0 * buf[0] + w1 * buf[1])


def _combine(dest, x2, route, gate, ys, tiles_per_batch_of):
    n, d = x2.shape
    tm = _row_tile(n, 256)
    per_batch = tiles_per_batch_of(tm)
    return pl.pallas_call(
        functools.partial(_combine_kernel, tm=tm),
        out_shape=jax.ShapeDtypeStruct((n, d), F32),
        grid_spec=pltpu.PrefetchScalarGridSpec(
            num_scalar_prefetch=1,
            grid=(n // tm,),
            in_specs=[pl.BlockSpec((tm, d), lambda i, de: (i, 0)),
                      pl.BlockSpec((tm, LANES), lambda i, de: (i, 0)),
                      pl.BlockSpec((None, 1, d), lambda i, de: (i // per_batch, 0, 0)),
                      pl.BlockSpec(memory_space=pl.ANY)],
            out_specs=pl.BlockSpec((tm, d), lambda i, de: (i, 0)),
            scratch_shapes=[pltpu.VMEM((2, tm, d), F32), pltpu.SemaphoreType.DMA(())],
        ),
        compiler_params=_cparams("arbitrary"),
        name="moe_combine",
    )(dest, x2, route, gate, ys)


def _moe_layer(x, g, sc, sh, gate, w_group, b_group, w_router, b_router, w_gate, w_up, w_down, layer):
    b, s, d = x.shape
    n = b * s
    h_packed, route, counts = _router(x, g, sc, sh, w_group, b_group, w_router, b_router)
    route = route.reshape(n, LANES)
    counts = counts[0, :N_EXPERTS].astype(I32)
    padded = (counts + MOE_BLOCK - 1) // MOE_BLOCK * MOE_BLOCK
    pad_end = jnp.cumsum(padded)
    pad_start = pad_end - padded
    e_id = route[:, ROUTE_E0:ROUTE_E1 + 1].astype(I32)
    rank = route[:, ROUTE_R0:ROUTE_R1 + 1].astype(I32)
    dest = (pad_start[e_id] + rank).reshape(-1)
    n_blocks = -(-(n * 2) // MOE_BLOCK) + N_EXPERTS
    block_start = jnp.arange(n_blocks, dtype=I32) * MOE_BLOCK
    block_e = jnp.minimum(jnp.sum((pad_end[None, :] <= block_start[:, None]).astype(I32), axis=1), N_EXPERTS - 1)
    n_valid = (pad_end[-1:] // MOE_BLOCK).astype(I32)
    xs = _dispatch(dest, h_packed.reshape(n, d // 2), n_blocks * MOE_BLOCK)
    ys = _experts(block_e, n_valid, xs, w_gate, w_up, w_down, layer)
    out = _combine(dest, x.reshape(n, d), route, gate, ys, lambda tm: s // tm)
    return out.reshape(b, s, d)


def kernel(x, c, w_ada, b_ada, mix_norm_g, ffn_norm_g, even_w_in, conv_b_glu, conv_dw, conv_dw_b, conv_ln_g, conv_ln_b, even_w_out, odd_w_in, qk_norm_q, qk_norm_k, lambda_q1, lambda_k1, lambda_q2, lambda_k2, diff_sub_g, odd_w_out, moe_w_group, moe_b_group, moe_w_router, moe_b_router, moe_w_gate, moe_w_up, moe_w_down):
    depth = w_ada.shape[0]
    d = x.shape[-1]
    sb_width = d // 2
    conv_width = d // 2
    diff_heads = d // (2 * HEAD_DIM)
    qk_width = diff_heads * 2 * HEAD_DIM
    mod = _ada_mod(c, w_ada, b_ada)
    for layer in range(depth):
        sh1, sc1, g1, sh2, sc2, g2 = [m[:, None, :] for m in jnp.split(mod[layer], 6, axis=-1)]
        j = layer // 2
        if layer % 2 == 0:
            q_gain = jnp.full((1, sb_width), LOG2E * HEAD_DIM ** -0.5, F32)
            proj = _inproj(x, mix_norm_g[layer], sc1, sh1, even_w_in[j], q_gain, head_norm=False)
            a_out = _stick_breaking(proj, sb_width)
            b_out = _conformer_conv(proj, 3 * sb_width, conv_width, conv_b_glu[j], conv_dw[j], conv_dw_b[j],
                                    conv_ln_g[j], conv_ln_b[j])
            x = _outproj_residual([a_out, b_out], [even_w_out[j][:sb_width], even_w_out[j][sb_width:]], x, g1)
        else:
            lam_init = 0.8 - 0.6 * math.exp(-0.3 * layer)
            reps = qk_width // HEAD_DIM
            qk_gain = jnp.concatenate([jnp.tile(qk_norm_q[j] * (LOG2E * HEAD_DIM ** -0.5), reps),
                                       jnp.tile(qk_norm_k[j], reps)])[None, :]
            proj = _inproj(x, mix_norm_g[layer], sc1, sh1, odd_w_in[j], qk_gain, head_norm=True)
            o = _diff_attention(proj, diff_heads, lambda_q1[j], lambda_k1[j], lambda_q2[j], lambda_k2[j],
                                diff_sub_g[j], lam_init)
            x = _outproj_residual([o], [odd_w_out[j]], x, g1)
        x = _moe_layer(x, ffn_norm_g[layer], sc2, sh2, g2, moe_w_group[layer], moe_b_group[layer],
                       moe_w_router[layer], moe_b_router[layer], moe_w_gate, moe_w_up, moe_w_down, layer)
    return x
```

```python
import functools
import math

import jax
import jax.numpy as jnp
from jax import lax
from jax.experimental import pallas as pl
from jax.experimental.pallas import tpu as pltpu

F32 = jnp.float32
BF16 = jnp.bfloat16
U32 = jnp.uint32
I32 = jnp.int32

EPS = 1e-6
LANES = 128
HEAD_DIM = 64
CHUNK = 64
CONV_KERNEL = 31
CONV_HALO = 32
N_GROUPS = 4
EXPERTS_PER_GROUP = 8
N_EXPERTS = N_GROUPS * EXPERTS_PER_GROUP
MOE_BLOCK = 256
ALIBI_MAX_EXP = 8.0
VMEM_LIMIT_BYTES = 56 * 1024 * 1024
HIGHEST = lax.Precision.HIGHEST


def _cparams(*sem):
    return pltpu.CompilerParams(dimension_semantics=sem, vmem_limit_bytes=VMEM_LIMIT_BYTES)


def _div_pow2(v, n):
    shift = n.bit_length() - 1
    assert 1 << shift == n
    return lax.shift_right_arithmetic(v, jnp.int32(shift))


def _row_tile(n, want):
    t = min(n, want)
    assert n % t == 0
    return t


def _ada_kernel(c_ref, w_ref, b_ref, o_ref):
    c = c_ref[...]
    c_act = c * jax.nn.sigmoid(c)
    o_ref[...] = jnp.dot(c_act, w_ref[...], preferred_element_type=F32, precision=HIGHEST) + b_ref[...]


def _ada_mod(c, w_ada, b_ada):
    depth, d, d6 = w_ada.shape
    b = c.shape[0]
    rows = 8
    c_pad = jnp.zeros((rows, d), F32).at[:b].set(c)
    tn = _row_tile(d6, 1536)
    out = pl.pallas_call(
        _ada_kernel,
        out_shape=jax.ShapeDtypeStruct((depth, rows, d6), F32),
        grid=(depth, d6 // tn),
        in_specs=[
            pl.BlockSpec((rows, d), lambda l, j: (0, 0)),
            pl.BlockSpec((None, d, tn), lambda l, j: (l, 0, j)),
            pl.BlockSpec((None, 1, tn), lambda l, j: (l, 0, j)),
        ],
        out_specs=pl.BlockSpec((None, rows, tn), lambda l, j: (l, 0, j)),
        compiler_params=_cparams("parallel", "parallel"),
        name="ada_mod",
    )(c_pad, w_ada, b_ada.reshape(depth, 1, d6))
    return out[:, :b]


def _modulated_norm(x, g, sc, sh):
    ms = jnp.mean(x * x, axis=-1, keepdims=True)
    return (x * lax.rsqrt(ms + EPS) * g) * (1.0 + sc) + sh


def _inproj_kernel(x_ref, g_ref, sc_ref, sh_ref, w_ref, gn_ref, o_ref, *, n_gain, head_norm, tn):
    h = _modulated_norm(x_ref[...], g_ref[...], sc_ref[...], sh_ref[...]).astype(BF16)
    n_out = o_ref.shape[-1]
    if head_norm:
        r = _div_pow2(lax.broadcasted_iota(I32, (tn, tn), 0), HEAD_DIM)
        c = _div_pow2(lax.broadcasted_iota(I32, (tn, tn), 1), HEAD_DIM)
        group_ones = jnp.where(r == c, 1.0, 0.0).astype(BF16)
    for c0 in range(0, n_out, tn):
        y = jnp.dot(h, w_ref[:, c0:c0 + tn], preferred_element_type=F32)
        if c0 < n_gain:
            if head_norm:
                gs = jnp.dot((y * y).astype(BF16), group_ones, preferred_element_type=F32)
                y = y * lax.rsqrt(gs * (1.0 / HEAD_DIM) + EPS)
            y = y * gn_ref[:, c0:c0 + tn]
        o_ref[:, c0:c0 + tn] = y.astype(o_ref.dtype)


def _inproj(x, g, sc, sh, w, gain, head_norm):
    b, s, d = x.shape
    n_out = w.shape[1]
    tm = _row_tile(s, 512)
    tn = 256
    n_gain = gain.shape[1]
    assert n_out % tn == 0 and n_gain % tn == 0
    gn = jnp.zeros((1, n_out), F32).at[:, :n_gain].set(gain)
    vec = pl.BlockSpec((None, 1, d), lambda bi, i: (bi, 0, 0))
    return pl.pallas_call(
        functools.partial(_inproj_kernel, n_gain=n_gain, head_norm=head_norm, tn=tn),
        out_shape=jax.ShapeDtypeStruct((b, s, n_out), BF16),
        grid=(b, s // tm),
        in_specs=[
            pl.BlockSpec((None, tm, d), lambda bi, i: (bi, i, 0)),
            pl.BlockSpec((1, d), lambda bi, i: (0, 0)),
            vec, vec,
            pl.BlockSpec((d, n_out), lambda bi, i: (0, 0)),
            pl.BlockSpec((1, n_out), lambda bi, i: (0, 0)),
        ],
        out_specs=pl.BlockSpec((None, tm, n_out), lambda bi, i: (bi, i, 0)),
        compiler_params=_cparams("parallel", "parallel"),
        name="inproj",
    )(x, g.reshape(1, d), sc, sh, w.astype(BF16), gn)


def _outproj_kernel(*refs, n_in):
    ins, ws = refs[:n_in], refs[n_in:2 * n_in]
    x_ref, g_ref, o_ref = refs[2 * n_in:]
    acc = jnp.dot(ins[0][...], ws[0][...], preferred_element_type=F32)
    for a, w in zip(ins[1:], ws[1:]):
        acc += jnp.dot(a[...], w[...], preferred_element_type=F32)
    o_ref[...] = x_ref[...] + g_ref[...] * acc


def _outproj_residual(parts, weights, x, gate):
    b, s, d = x.shape
    tm = _row_tile(s, 512)
    n_in = len(parts)
    in_specs = [pl.BlockSpec((None, tm, p.shape[-1]), lambda bi, i: (bi, i, 0)) for p in parts]
    in_specs += [pl.BlockSpec(w.shape, lambda bi, i: (0, 0)) for w in weights]
    in_specs += [pl.BlockSpec((None, tm, d), lambda bi, i: (bi, i, 0)),
                 pl.BlockSpec((None, 1, d), lambda bi, i: (bi, 0, 0))]
    return pl.pallas_call(
        functools.partial(_outproj_kernel, n_in=n_in),
        out_shape=jax.ShapeDtypeStruct((b, s, d), F32),
        grid=(b, s // tm),
        in_specs=in_specs,
        out_specs=pl.BlockSpec((None, tm, d), lambda bi, i: (bi, i, 0)),
        compiler_params=_cparams("parallel", "parallel"),
        name="outproj",
    )(*parts, *[w.astype(BF16) for w in weights], x, gate)


LOG2E = 1.4426950408889634
MASKED_LOG2 = -1e30


def _dot_nt(a, b):
    return lax.dot_general(a, b, (((1,), (1,)), ((), ())), preferred_element_type=F32)


def _sb_kernel(q_ref, k_ref, v_ref, o_ref, acc_ref, z_ref, w_ref, *, tq, tk):
    i = pl.program_id(2)
    n_heads = LANES // HEAD_DIM
    n_diag = tq // tk
    lane = lax.broadcasted_iota(I32, (1, LANES), 1)
    r = lax.broadcasted_iota(I32, (tk, tk), 0)
    c = lax.broadcasted_iota(I32, (tk, tk), 1)
    later = jnp.where(r >= c, 1.0, 0.0).astype(BF16)
    row = lax.broadcasted_iota(I32, (tq, tk), 0)
    col = lax.broadcasted_iota(I32, (tq, tk), 1)
    q = q_ref[...]
    qs = [jnp.where((lane >= hh * HEAD_DIM) & (lane < (hh + 1) * HEAD_DIM), q, jnp.zeros_like(q))
          for hh in range(n_heads)]

    def key_tile(ref, j):
        return ref[pl.ds(pl.multiple_of(j * tk, tk), tk), :]

    def step(slot, j, j_next, j_prev, state, causal):
        carry, carry_prev = state
        new_carry = []
        for hh in range(n_heads):
            a = jnp.exp2(w_ref[1 - slot, hh] - carry_prev[hh])
            acc_ref[hh] += jnp.dot(a.astype(BF16), key_tile(v_ref, j_prev), preferred_element_type=F32)
        for hh in range(n_heads):
            z = z_ref[slot, hh]
            neg_abs = pltpu.bitcast(pltpu.bitcast(z, U32) | jnp.uint32(0x80000000), F32)
            fail = jnp.maximum(z, 0.0) + jnp.log2(1.0 + jnp.exp2(neg_abs))
            if causal is not None:
                fail = jnp.where(causal, fail, 0.0)
            incl = jnp.dot(fail.astype(BF16), later, preferred_element_type=F32)
            w = z - incl
            if causal is not None:
                w = jnp.where(causal, w, MASKED_LOG2)
            w_ref[slot, hh] = w
            new_carry.append(carry[hh] + incl[:, 0:1])
        for hh in range(n_heads):
            z_ref[1 - slot, hh] = _dot_nt(qs[hh], key_tile(k_ref, j_next))
        return tuple(new_carry), carry

    assert n_diag == 2
    top = i * n_diag + 1
    acc_ref[...] = jnp.zeros_like(acc_ref)
    w_ref[1] = jnp.full(w_ref.shape[1:], MASKED_LOG2, F32)
    for hh in range(n_heads):
        z_ref[0, hh] = _dot_nt(qs[hh], key_tile(k_ref, top))
    zero = tuple(jnp.zeros((tq, 1), F32) for _ in range(n_heads))
    state = (zero, zero)
    state = step(0, top, top - 1, top, state, col + tk < row)
    state = step(1, top - 1, jnp.maximum(top - 2, 0), top, state, col < row)

    def body(jj, state):
        ja = top - 2 - 2 * jj
        state = step(0, ja, ja - 1, ja + 1, state, None)
        return step(1, ja - 1, jnp.maximum(ja - 2, 0), ja, state, None)

    carry, carry_prev = lax.fori_loop(0, i, body, state)
    for hh in range(n_heads):
        a = jnp.exp2(w_ref[1, hh] - carry_prev[hh])
        acc_ref[hh] += jnp.dot(a.astype(BF16), key_tile(v_ref, 0), preferred_element_type=F32)
    o_ref[...] = jnp.where(lane < HEAD_DIM, acc_ref[0], acc_ref[1]).astype(o_ref.dtype)


def _stick_breaking(proj, width):
    b, s, _ = proj.shape
    tk = _row_tile(s, 256)
    tq = _row_tile(s, 2 * tk)
    nblk = width // LANES
    return pl.pallas_call(
        functools.partial(_sb_kernel, tq=tq, tk=tk),
        out_shape=jax.ShapeDtypeStruct((b, s, width), BF16),
        grid=(b, nblk, s // tq),
        in_specs=[
            pl.BlockSpec((None, tq, LANES), lambda bi, h, i: (bi, i, h)),
            pl.BlockSpec((None, s, LANES), lambda bi, h, i: (bi, 0, nblk + h)),
            pl.BlockSpec((None, s, LANES), lambda bi, h, i: (bi, 0, 2 * nblk + h)),
        ],
        out_specs=pl.BlockSpec((None, tq, LANES), lambda bi, h, i: (bi, i, h)),
        scratch_shapes=[pltpu.VMEM((LANES // HEAD_DIM, tq, LANES), F32),
                        pltpu.VMEM((2, LANES // HEAD_DIM, tq, tk), F32),
                        pltpu.VMEM((2, LANES // HEAD_DIM, tq, tk), F32)],
        compiler_params=_cparams("parallel", "parallel", "parallel"),
        name="stick_breaking",
    )(proj, proj, proj)


def _conv_kernel(a_ref, g_ref, pa_ref, pg_ref, bglu_ref, dw_ref, dwb_ref, lng_ref, lnb_ref, o_ref, hbuf, *, ts, rc):
    i = pl.program_id(1)
    w = a_ref.shape[-1]

    def glu(a, g):
        a = a.astype(F32) + bglu_ref[:, :w]
        g = g.astype(F32) + bglu_ref[:, w:]
        return a * jax.nn.sigmoid(g)

    prev = glu(pa_ref[...], pg_ref[...])
    hbuf[0, 0:CONV_HALO] = jnp.where(i > 0, prev, 0.0)
    hbuf[0, CONV_HALO:CONV_HALO + ts] = glu(a_ref[...], g_ref[...])
    off = CONV_HALO - (CONV_KERNEL - 1)
    n_phase = hbuf.shape[0]
    for p in range(1, n_phase):
        hbuf[p, 0:CONV_HALO + ts - n_phase] = hbuf[0, p:p + CONV_HALO + ts - n_phase]
    for r0 in range(0, ts, rc):
        acc = jnp.zeros((rc, w), F32) + dwb_ref[...]
        for k in range(CONV_KERNEL):
            p = (off + k) % n_phase
            start = r0 + off + k - p
            acc = acc + dw_ref[k:k + 1, :] * hbuf[p, start:start + rc, :]
        mu = jnp.mean(acc, axis=-1, keepdims=True)
        cen = acc - mu
        var = jnp.mean(cen * cen, axis=-1, keepdims=True)
        y = cen * lax.rsqrt(var + EPS) * lng_ref[...] + lnb_ref[...]
        o_ref[r0:r0 + rc, :] = (y * jax.nn.sigmoid(y)).astype(o_ref.dtype)


def _conformer_conv(proj, col0, width, b_glu, dw, dw_b, ln_g, ln_b):
    b, s, _ = proj.shape
    ts = _row_tile(s, 256)
    assert col0 % width == 0 and ts % CONV_HALO == 0
    cb = col0 // width
    hb = ts // CONV_HALO
    cur = lambda off: pl.BlockSpec((None, ts, width), lambda bi, i: (bi, i, cb + off))
    prev = lambda off: pl.BlockSpec((None, CONV_HALO, width),
                                    lambda bi, i: (bi, jnp.maximum(i * hb - 1, 0), cb + off))
    row = lambda n: pl.BlockSpec((1, n), lambda bi, i: (0, 0))
    return pl.pallas_call(
        functools.partial(_conv_kernel, ts=ts, rc=32),
        out_shape=jax.ShapeDtypeStruct((b, s, width), BF16),
        grid=(b, s // ts),
        in_specs=[cur(0), cur(1), prev(0), prev(1), row(2 * width),
                  pl.BlockSpec((CONV_KERNEL, width), lambda bi, i: (0, 0)),
                  row(width), row(width), row(width)],
        out_specs=pl.BlockSpec((None, ts, width), lambda bi, i: (bi, i, 0)),
        scratch_shapes=[pltpu.VMEM((8, CONV_HALO + ts, width), F32)],
        compiler_params=_cparams("parallel", "parallel"),
        name="conformer_conv",
    )(proj, proj, proj, proj, b_glu.reshape(1, -1), dw, dw_b.reshape(1, -1),
      ln_g.reshape(1, -1), ln_b.reshape(1, -1))


def _diff_kernel(slope_ref, q_ref, k_ref, vt_ref, lq1_ref, lk1_ref, lq2_ref, lk2_ref, subg_ref, o_ref,
                 acc_ref, s_ref, p_ref, *, tq, tk, lam_init):
    h = pl.program_id(1)
    i = pl.program_id(2)
    n_diag = tq // tk
    slope = slope_ref[h]
    lane = lax.broadcasted_iota(I32, (1, LANES), 1)
    key = lax.broadcasted_iota(I32, (tk, tq), 0)
    qry = lax.broadcasted_iota(I32, (tk, tq), 1)
    rel = (qry - key).astype(F32)
    bias_past = -slope * rel
    q = q_ref[...]
    zero = jnp.zeros_like(q)
    qs = (jnp.where(lane < HEAD_DIM, q, zero), jnp.where(lane >= HEAD_DIM, q, zero))

    def scores(slot, j, bias):
        kt = k_ref[pl.ds(pl.multiple_of(j * tk, tk), tk), :]
        tile_max = []
        for n in range(2):
            s = _dot_nt(kt, qs[n]) + bias
            s_ref[slot, n] = s
            tile_max.append(jnp.max(s, axis=0, keepdims=True))
        return tuple(tile_max)

    def weighted_values(slot, j, alpha):
        vt = vt_ref[:, pl.ds(pl.multiple_of(j * tk, tk), tk)]
        for n in range(2):
            acc_ref[n] = alpha[n] * acc_ref[n] + jnp.dot(vt, p_ref[slot, n], preferred_element_type=F32)

    def step(slot, j_next, j_prev, state, bias_next, shift):
        stats, alpha_prev, tile_max = state
        weighted_values(1 - slot, j_prev, alpha_prev)
        new_stats, alphas = [], []
        for n in range(2):
            m, l = stats[n]
            m_new = jnp.maximum(m, tile_max[n] + shift)
            alpha = jnp.exp2(m - m_new)
            p = jnp.exp2(s_ref[slot, n] - (m_new - shift))
            p_ref[slot, n] = p.astype(BF16)
            new_stats.append((m_new, alpha * l + jnp.sum(p, axis=0, keepdims=True)))
            alphas.append(alpha)
        return tuple(new_stats), tuple(alphas), scores(1 - slot, j_next, bias_next)

    def diag_bias(dj):
        allowed = _div_pow2(key + dj * tk, CHUNK) <= _div_pow2(qry, CHUNK)
        return jnp.where(allowed, -slope * jnp.abs(rel - float(dj * tk)), -jnp.inf)

    def shift_of(j):
        return -slope * ((i * n_diag - j) * tk).astype(F32)

    assert n_diag == 2
    top = i * n_diag + 1
    acc_ref[...] = jnp.zeros_like(acc_ref)
    p_ref[1] = jnp.zeros(p_ref.shape[1:], BF16)
    floor = jnp.full((1, tq), MASKED_LOG2, F32)
    zl = jnp.zeros((1, tq), F32)
    one = jnp.ones((1, tq), F32)
    state = (((floor, zl), (floor, zl)), (one, one), scores(0, top, diag_bias(1)))
    state = step(0, top - 1, top, state, diag_bias(0), 0.0)
    state = step(1, jnp.maximum(top - 2, 0), top, state, bias_past, 0.0)

    def body(jj, state):
        ja = top - 2 - 2 * jj
        state = step(0, ja - 1, ja + 1, state, bias_past, shift_of(ja))
        return step(1, jnp.maximum(ja - 2, 0), ja, state, bias_past, shift_of(ja - 1))

    ((m1, l1), (m2, l2)), alpha_last, _ = lax.fori_loop(0, i, body, state)
    weighted_values(1, 0, alpha_last)
    lam = (jnp.exp(jnp.sum(lq1_ref[...] * lk1_ref[...], axis=-1, keepdims=True))
           - jnp.exp(jnp.sum(lq2_ref[...] * lk2_ref[...], axis=-1, keepdims=True)) + lam_init)
    o = acc_ref[0] / l1 - lam * (acc_ref[1] / l2)
    ms = jnp.mean(o * o, axis=0, keepdims=True)
    o = o * lax.rsqrt(ms + EPS) * (1.0 - lam_init)
    o_ref[...] = (o.T * subg_ref[...]).astype(o_ref.dtype)


def _diff_attention(proj, n_heads, lq1, lk1, lq2, lk2, sub_g, lam_init):
    b, s, _ = proj.shape
    tk = _row_tile(s, 256)
    tq = _row_tile(s, 2 * tk)
    assert tk % CHUNK == 0
    v_t = jnp.swapaxes(proj[:, :, 2 * n_heads * LANES:], 1, 2)
    slopes = LOG2E * jnp.exp2(-ALIBI_MAX_EXP * jnp.arange(1, n_heads + 1, dtype=F32) / n_heads)
    vec = lambda n: pl.BlockSpec((1, n), lambda bi, h, i, sl: (0, 0))
    return pl.pallas_call(
        functools.partial(_diff_kernel, tq=tq, tk=tk, lam_init=lam_init),
        out_shape=jax.ShapeDtypeStruct((b, s, n_heads * LANES), BF16),
        grid_spec=pltpu.PrefetchScalarGridSpec(
            num_scalar_prefetch=1,
            grid=(b, n_heads, s // tq),
            in_specs=[
                pl.BlockSpec((None, tq, LANES), lambda bi, h, i, sl: (bi, i, h)),
                pl.BlockSpec((None, s, LANES), lambda bi, h, i, sl: (bi, 0, n_heads + h)),
                pl.BlockSpec((None, LANES, s), lambda bi, h, i, sl: (bi, h, 0)),
                vec(HEAD_DIM), vec(HEAD_DIM), vec(HEAD_DIM), vec(HEAD_DIM), vec(LANES),
            ],
            out_specs=pl.BlockSpec((None, tq, LANES), lambda bi, h, i, sl: (bi, i, h)),
            scratch_shapes=[pltpu.VMEM((2, LANES, tq), F32), pltpu.VMEM((2, 2, tk, tq), F32),
                            pltpu.VMEM((2, 2, tk, tq), BF16)],
        ),
        compiler_params=_cparams("parallel", "parallel", "parallel"),
        name="diff_attention",
    )(slopes, proj, proj, v_t, lq1.reshape(1, -1), lk1.reshape(1, -1), lq2.reshape(1, -1),
      lk2.reshape(1, -1), sub_g.reshape(1, -1))


ROUTE_E0, ROUTE_E1, ROUTE_W0, ROUTE_W1, ROUTE_R0, ROUTE_R1 = range(6)
DMA_ISSUE_UNROLL = 8


def _pack_bf16_pair(lo, hi):
    lo_bits = pltpu.bitcast(lo.astype(BF16).astype(F32), U32) >> 16
    hi_bits = pltpu.bitcast(hi.astype(BF16).astype(F32), U32) & jnp.uint32(0xFFFF0000)
    return hi_bits | lo_bits


def _unpack_bf16_pair(u):
    lo = pltpu.bitcast(u << 16, F32).astype(BF16)
    hi = pltpu.bitcast(u & jnp.uint32(0xFFFF0000), F32).astype(BF16)
    return lo, hi


def _router_kernel(x_ref, g_ref, sc_ref, sh_ref, wr_ref, br_ref, hp_ref, rt_ref, cnt_ref, run_ref, *, tm):
    first = (pl.program_id(0) == 0) & (pl.program_id(1) == 0)

    @pl.when(first)
    def _():
        run_ref[...] = jnp.zeros_like(run_ref)

    h = _modulated_norm(x_ref[...], g_ref[...], sc_ref[...], sh_ref[...])
    half = h.shape[-1] // 2
    hp_ref[...] = _pack_bf16_pair(h[:, :half], h[:, half:])

    logits = jnp.dot(h, wr_ref[...], preferred_element_type=F32, precision=HIGHEST) + br_ref[...]
    lane = lax.broadcasted_iota(I32, (tm, LANES), 1)
    lanef = lane.astype(F32)
    ninf = -jnp.inf
    big = float(LANES)

    def first_argmax(vals):
        top = jnp.max(vals, axis=-1, keepdims=True)
        idx = jnp.min(jnp.where(vals == top, lanef, big), axis=-1, keepdims=True)
        return top, idx

    is_group = lane < N_GROUPS
    gl = jnp.where(is_group, logits, ninf)
    gmax, gidx = first_argmax(gl)
    g_w = 1.0 / jnp.sum(jnp.where(is_group, jnp.exp(gl - gmax), 0.0), axis=-1, keepdims=True)
    lane_group = _div_pow2(lane - N_GROUPS, EXPERTS_PER_GROUP).astype(F32)
    in_group = (lane >= N_GROUPS) & (lane < N_GROUPS + N_EXPERTS) & (lane_group == gidx)
    el = jnp.where(in_group, logits, ninf)
    v0, i0 = first_argmax(el)
    v1, i1 = first_argmax(jnp.where(lanef == i0, ninf, el))
    tt = jnp.exp(v1 - v0)
    w0 = g_w / (1.0 + tt)
    w1 = g_w * tt / (1.0 + tt)
    e0 = i0 - N_GROUPS
    e1 = i1 - N_GROUPS

    member = (lanef == e0) | (lanef == e1)
    r = lax.broadcasted_iota(I32, (tm, tm), 0)
    c = lax.broadcasted_iota(I32, (tm, tm), 1)
    earlier = jnp.where(c < r, 1.0, 0.0).astype(BF16)
    before = jnp.dot(earlier, jnp.where(member, 1.0, 0.0).astype(BF16), preferred_element_type=F32)
    before = before + run_ref[...]
    r0 = jnp.sum(jnp.where(lanef == e0, before, 0.0), axis=-1, keepdims=True)
    r1 = jnp.sum(jnp.where(lanef == e1, before, 0.0), axis=-1, keepdims=True)
    run_ref[...] += jnp.sum(jnp.where(member, 1.0, 0.0), axis=0, keepdims=True)
    cnt_ref[...] = run_ref[...]

    slab = jnp.zeros((tm, LANES), F32)
    for colv, val in ((ROUTE_E0, e0), (ROUTE_E1, e1), (ROUTE_W0, w0), (ROUTE_W1, w1), (ROUTE_R0, r0), (ROUTE_R1, r1)):
        slab = jnp.where(lane == colv, val, slab)
    rt_ref[...] = slab


def _router(x, g, sc, sh, w_group, b_group, w_router, b_router):
    b, s, d = x.shape
    tm = _row_tile(s, 512)
    wr = jnp.zeros((d, LANES), F32).at[:, :N_GROUPS].set(w_group).at[:, N_GROUPS:N_GROUPS + N_EXPERTS].set(w_router)
    br = jnp.zeros((1, LANES), F32).at[0, :N_GROUPS].set(b_group).at[0, N_GROUPS:N_GROUPS + N_EXPERTS].set(b_router)
    vec = pl.BlockSpec((None, 1, d), lambda bi, i: (bi, 0, 0))
    return pl.pallas_call(
        functools.partial(_router_kernel, tm=tm),
        out_shape=(jax.ShapeDtypeStruct((b, s, d // 2), U32),
                   jax.ShapeDtypeStruct((b, s, LANES), F32),
                   jax.ShapeDtypeStruct((1, LANES), F32)),
        grid=(b, s // tm),
        in_specs=[
            pl.BlockSpec((None, tm, d), lambda bi, i: (bi, i, 0)),
            pl.BlockSpec((1, d), lambda bi, i: (0, 0)),
            vec, vec,
            pl.BlockSpec((d, LANES), lambda bi, i: (0, 0)),
            pl.BlockSpec((1, LANES), lambda bi, i: (0, 0)),
        ],
        out_specs=(pl.BlockSpec((None, tm, d // 2), lambda bi, i: (bi, i, 0)),
                   pl.BlockSpec((None, tm, LANES), lambda bi, i: (bi, i, 0)),
                   pl.BlockSpec((1, LANES), lambda bi, i: (0, 0))),
        scratch_shapes=[pltpu.VMEM((1, LANES), F32)],
        compiler_params=_cparams("arbitrary", "arbitrary"),
        name="moe_router",
    )(x, g.reshape(1, d), sc, sh, wr, br)


def _zero_kernel(o_ref):
    o_ref[...] = jnp.zeros_like(o_ref)


def _zeros_u32(rows, cols):
    tr = _row_tile(rows, 2048)
    return pl.pallas_call(
        _zero_kernel,
        out_shape=jax.ShapeDtypeStruct((rows, cols), U32),
        grid=(rows // tr,),
        out_specs=pl.BlockSpec((tr, cols), lambda i: (i, 0)),
        compiler_params=_cparams("parallel"),
        name="moe_zero_slots",
    )()


def _dispatch_kernel(dest_ref, h_ref, xs_in_ref, xs_ref, sem, *, chunk):
    del xs_in_ref
    c = pl.program_id(0)

    def row_copy(t, slot):
        return pltpu.make_async_copy(h_ref.at[pl.ds(t, 1), :], xs_ref.at[pl.ds(slot, 1), :], sem)

    def wait_chunk():
        for _ in range(2):
            pltpu.make_async_copy(h_ref.at[pl.ds(0, chunk), :], xs_ref.at[pl.ds(0, chunk), :], sem).wait()

    def body(r, carry):
        t = c * chunk + r
        row_copy(t, dest_ref[2 * t]).start()
        row_copy(t, dest_ref[2 * t + 1]).start()
        return carry

    lax.fori_loop(0, chunk, body, 0, unroll=DMA_ISSUE_UNROLL)
    pl.when(c > 0)(wait_chunk)
    pl.when(c == pl.num_programs(0) - 1)(wait_chunk)


def _dispatch(dest, h_packed, n_slots):
    n, w = h_packed.shape
    chunk = _row_tile(n, 256)
    return pl.pallas_call(
        functools.partial(_dispatch_kernel, chunk=chunk),
        out_shape=jax.ShapeDtypeStruct((n_slots, w), U32),
        grid_spec=pltpu.PrefetchScalarGridSpec(
            num_scalar_prefetch=1,
            grid=(n // chunk,),
            in_specs=[pl.BlockSpec(memory_space=pl.ANY), pl.BlockSpec(memory_space=pl.ANY)],
            out_specs=pl.BlockSpec(memory_space=pl.ANY),
            scratch_shapes=[pltpu.SemaphoreType.DMA(())],
        ),
        input_output_aliases={2: 0},
        compiler_params=_cparams("arbitrary"),
        name="moe_dispatch",
    )(dest, h_packed, _zeros_u32(n_slots, w))


def _expert_kernel(be_ref, nv_ref, xs_ref, wg_ref, wu_ref, wd_ref, ys_ref, wg_b, wu_b, wd_b):
    blk = pl.program_id(0)
    valid = blk < nv_ref[0]
    changed = (blk == 0) | (be_ref[blk] != be_ref[jnp.maximum(blk - 1, 0)])

    @pl.when(valid & changed)
    def _():
        wg_b[...] = wg_ref[...].astype(BF16)
        wu_b[...] = wu_ref[...].astype(BF16)
        wd_b[...] = wd_ref[...].astype(BF16)

    @pl.when(valid)
    def _():
        x_lo, x_hi = _unpack_bf16_pair(xs_ref[...])
        half = x_lo.shape[-1]

        def proj(w):
            return (jnp.dot(x_lo, w[:half, :], preferred_element_type=F32)
                    + jnp.dot(x_hi, w[half:, :], preferred_element_type=F32))

        gate = proj(wg_b)
        hid = (gate * jax.nn.sigmoid(gate)) * proj(wu_b)
        ys_ref[...] = jnp.dot(hid.astype(BF16), wd_b[...], preferred_element_type=F32)

    @pl.when(jnp.logical_not(valid))
    def _():
        ys_ref[...] = jnp.zeros_like(ys_ref)


def _experts(block_e, n_valid, xs, w_gate, w_up, w_down, layer):
    n_slots, half = xs.shape
    d = 2 * half
    de = w_gate.shape[-1]
    n_blocks = n_slots // MOE_BLOCK
    last = lambda blk, nv: jnp.minimum(blk, nv[0] - 1)
    w_in = pl.BlockSpec((None, None, d, de), lambda blk, be, nv: (layer, be[last(blk, nv)], 0, 0))
    return pl.pallas_call(
        _expert_kernel,
        out_shape=jax.ShapeDtypeStruct((n_slots, d), F32),
        grid_spec=pltpu.PrefetchScalarGridSpec(
            num_scalar_prefetch=2,
            grid=(n_blocks,),
            in_specs=[pl.BlockSpec((MOE_BLOCK, half), lambda blk, be, nv: (last(blk, nv), 0)),
                      w_in, w_in,
                      pl.BlockSpec((None, None, de, d), lambda blk, be, nv: (layer, be[last(blk, nv)], 0, 0))],
            out_specs=pl.BlockSpec((MOE_BLOCK, d), lambda blk, be, nv: (blk, 0)),
            scratch_shapes=[pltpu.VMEM((d, de), BF16), pltpu.VMEM((d, de), BF16), pltpu.VMEM((de, d), BF16)],
        ),
        compiler_params=_cparams("arbitrary"),
        name="moe_experts",
    )(block_e, n_valid, xs, w_gate, w_up, w_down)


def _combine_kernel(dest_ref, x_ref, rt_ref, g_ref, ys_ref, o_ref, buf, sems, *, tm):
    i = pl.program_id(0)
    cur = lax.rem(i, 2)

    def gather(tile, half):
        base = tile * tm

        def body(r, carry):
            for k in range(2):
                pltpu.make_async_copy(ys_ref.at[pl.ds(dest_ref[2 * (base + r) + k], 1), :],
                                      buf.at[half, k, pl.ds(r, 1), :], sems.at[half]).start()
            return carry

        lax.fori_loop(0, tm, body, 0, unroll=DMA_ISSUE_UNROLL)

    @pl.when(i == 0)
    def _():
        gather(0, 0)

    @pl.when(i + 1 < pl.num_programs(0))
    def _():
        gather(i + 1, 1 - cur)

    for k in range(2):
        pltpu.make_async_copy(ys_ref.at[pl.ds(0, tm), :], buf.at[cur, k], sems.at[cur]).wait()
    rt = rt_ref[...]
    w0 = rt[:, ROUTE_W0:ROUTE_W0 + 1]
    w1 = rt[:, ROUTE_W1:ROUTE_W1 + 1]
    o_ref[...] = x_ref[...] + g_ref[...] * (w0 * buf[cur, 0] + w1 * buf[cur, 1])


def _combine(dest, x2, route, gate, ys, tiles_per_batch_of):
    n, d = x2.shape
    tm = _row_tile(n, 256)
    per_batch = tiles_per_batch_of(tm)
    return pl.pallas_call(
        functools.partial(_combine_kernel, tm=tm),
        out_shape=jax.ShapeDtypeStruct((n, d), F32),
        grid_spec=pltpu.PrefetchScalarGridSpec(
            num_scalar_prefetch=1,
            grid=(n // tm,),
            in_specs=[pl.BlockSpec((tm, d), lambda i, de: (i, 0)),
                      pl.BlockSpec((tm, LANES), lambda i, de: (i, 0)),
                      pl.BlockSpec((None, 1, d), lambda i, de: (i // per_batch, 0, 0)),
                      pl.BlockSpec(memory_space=pl.ANY)],
            out_specs=pl.BlockSpec((tm, d), lambda i, de: (i, 0)),
            scratch_shapes=[pltpu.VMEM((2, 2, tm, d), F32), pltpu.SemaphoreType.DMA((2,))],
        ),
        compiler_params=_cparams("arbitrary"),
        name="moe_combine",
    )(dest, x2, route, gate, ys)


def _moe_layer(x, g, sc, sh, gate, w_group, b_group, w_router, b_router, w_gate, w_up, w_down, layer):
    b, s, d = x.shape
    n = b * s
    h_packed, route, counts = _router(x, g, sc, sh, w_group, b_group, w_router, b_router)
    route = route.reshape(n, LANES)
    counts = counts[0, :N_EXPERTS].astype(I32)
    padded = (counts + MOE_BLOCK - 1) // MOE_BLOCK * MOE_BLOCK
    pad_end = jnp.cumsum(padded)
    pad_start = pad_end - padded
    e_id = route[:, ROUTE_E0:ROUTE_E1 + 1].astype(I32)
    rank = route[:, ROUTE_R0:ROUTE_R1 + 1].astype(I32)
    dest = (pad_start[e_id] + rank).reshape(-1)
    n_blocks = -(-(n * 2) // MOE_BLOCK) + N_EXPERTS
    block_start = jnp.arange(n_blocks, dtype=I32) * MOE_BLOCK
    block_e = jnp.minimum(jnp.sum((pad_end[None, :] <= block_start[:, None]).astype(I32), axis=1), N_EXPERTS - 1)
    n_valid = (pad_end[-1:] // MOE_BLOCK).astype(I32)
    xs = _dispatch(dest, h_packed.reshape(n, d // 2), n_blocks * MOE_BLOCK)
    ys = _experts(block_e, n_valid, xs, w_gate, w_up, w_down, layer)
    out = _combine(dest, x.reshape(n, d), route, gate, ys, lambda tm: s // tm)
    return out.reshape(b, s, d)


def kernel(x, c, w_ada, b_ada, mix_norm_g, ffn_norm_g, even_w_in, conv_b_glu, conv_dw, conv_dw_b, conv_ln_g, conv_ln_b, even_w_out, odd_w_in, qk_norm_q, qk_norm_k, lambda_q1, lambda_k1, lambda_q2, lambda_k2, diff_sub_g, odd_w_out, moe_w_group, moe_b_group, moe_w_router, moe_b_router, moe_w_gate, moe_w_up, moe_w_down):
    depth = w_ada.shape[0]
    d = x.shape[-1]
    sb_width = d // 2
    conv_width = d // 2
    diff_heads = d // (2 * HEAD_DIM)
    qk_width = diff_heads * 2 * HEAD_DIM
    mod = _ada_mod(c, w_ada, b_ada)
    for layer in range(depth):
        sh1, sc1, g1, sh2, sc2, g2 = [m[:, None, :] for m in jnp.split(mod[layer], 6, axis=-1)]
        j = layer // 2
        if layer % 2 == 0:
            q_gain = jnp.full((1, sb_width), LOG2E * HEAD_DIM ** -0.5, F32)
            proj = _inproj(x, mix_norm_g[layer], sc1, sh1, even_w_in[j], q_gain, head_norm=False)
            a_out = _stick_breaking(proj, sb_width)
            b_out = _conformer_conv(proj, 3 * sb_width, conv_width, conv_b_glu[j], conv_dw[j], conv_dw_b[j],
                                    conv_ln_g[j], conv_ln_b[j])
            x = _outproj_residual([a_out, b_out], [even_w_out[j][:sb_width], even_w_out[j][sb_width:]], x, g1)
        else:
            lam_init = 0.8 - 0.6 * math.exp(-0.3 * layer)
            reps = qk_width // HEAD_DIM
            qk_gain = jnp.concatenate([jnp.tile(qk_norm_q[j] * (LOG2E * HEAD_DIM ** -0.5), reps),
                                       jnp.tile(qk_norm_k[j], reps)])[None, :]
            proj = _inproj(x, mix_norm_g[layer], sc1, sh1, odd_w_in[j], qk_gain, head_norm=True)
            o = _diff_attention(proj, diff_heads, lambda_q1[j], lambda_k1[j], lambda_q2[j], lambda_k2[j],
                                diff_sub_g[j], lam_init)
            x = _outproj_residual([o], [odd_w_out[j]], x, g1)
        x = _moe_layer(x, ffn_norm_g[layer], sc2, sh2, g2, moe_w_group[layer], moe_b_group[layer],
                       moe_w_router[layer], moe_b_router[layer], moe_w_gate, moe_w_up, moe_w_down, layer)
    return x
```

```python
import functools
import math

import jax
import jax.numpy as jnp
from jax import lax
from jax.experimental import pallas as pl
from jax.experimental.pallas import tpu as pltpu

F32 = jnp.float32
BF16 = jnp.bfloat16
U32 = jnp.uint32
I32 = jnp.int32

EPS = 1e-6
LANES = 128
HEAD_DIM = 64
CHUNK = 64
CONV_KERNEL = 31
CONV_HALO = 32
N_GROUPS = 4
EXPERTS_PER_GROUP = 8
N_EXPERTS = N_GROUPS * EXPERTS_PER_GROUP
MOE_BLOCK = 512
ALIBI_MAX_EXP = 8.0
VMEM_LIMIT_BYTES = 56 * 1024 * 1024
HIGHEST = lax.Precision.HIGHEST


def _cparams(*sem):
    return pltpu.CompilerParams(dimension_semantics=sem, vmem_limit_bytes=VMEM_LIMIT_BYTES)


def _div_pow2(v, n):
    shift = n.bit_length() - 1
    assert 1 << shift == n
    return lax.shift_right_arithmetic(v, jnp.int32(shift))


def _row_tile(n, want):
    t = min(n, want)
    assert n % t == 0
    return t


def _ada_kernel(c_ref, w_ref, b_ref, o_ref):
    c = c_ref[...]
    c_act = c * jax.nn.sigmoid(c)
    o_ref[...] = jnp.dot(c_act, w_ref[...], preferred_element_type=F32, precision=HIGHEST) + b_ref[...]


def _ada_mod(c, w_ada, b_ada):
    depth, d, d6 = w_ada.shape
    b = c.shape[0]
    rows = 8
    c_pad = jnp.zeros((rows, d), F32).at[:b].set(c)
    tn = _row_tile(d6, 1536)
    out = pl.pallas_call(
        _ada_kernel,
        out_shape=jax.ShapeDtypeStruct((depth, rows, d6), F32),
        grid=(depth, d6 // tn),
        in_specs=[
            pl.BlockSpec((rows, d), lambda l, j: (0, 0)),
            pl.BlockSpec((None, d, tn), lambda l, j: (l, 0, j)),
            pl.BlockSpec((None, 1, tn), lambda l, j: (l, 0, j)),
        ],
        out_specs=pl.BlockSpec((None, rows, tn), lambda l, j: (l, 0, j)),
        compiler_params=_cparams("parallel", "parallel"),
        name="ada_mod",
    )(c_pad, w_ada, b_ada.reshape(depth, 1, d6))
    return out[:, :b]


def _modulated_norm(x, g, sc, sh):
    ms = jnp.mean(x * x, axis=-1, keepdims=True)
    return (x * lax.rsqrt(ms + EPS) * g) * (1.0 + sc) + sh


def _inproj_kernel(x_ref, g_ref, sc_ref, sh_ref, w_ref, gn_ref, o_ref, *, n_gain, head_norm, tn):
    h = _modulated_norm(x_ref[...], g_ref[...], sc_ref[...], sh_ref[...]).astype(BF16)
    n_out = o_ref.shape[-1]
    if head_norm:
        r = _div_pow2(lax.broadcasted_iota(I32, (tn, tn), 0), HEAD_DIM)
        c = _div_pow2(lax.broadcasted_iota(I32, (tn, tn), 1), HEAD_DIM)
        group_ones = jnp.where(r == c, 1.0, 0.0).astype(BF16)
    for c0 in range(0, n_out, tn):
        y = jnp.dot(h, w_ref[:, c0:c0 + tn], preferred_element_type=F32)
        if c0 < n_gain:
            if head_norm:
                gs = jnp.dot((y * y).astype(BF16), group_ones, preferred_element_type=F32)
                y = y * lax.rsqrt(gs * (1.0 / HEAD_DIM) + EPS)
            y = y * gn_ref[:, c0:c0 + tn]
        o_ref[:, c0:c0 + tn] = y.astype(o_ref.dtype)


def _inproj(x, g, sc, sh, w, gain, head_norm):
    b, s, d = x.shape
    n_out = w.shape[1]
    tm = _row_tile(s, 512)
    tn = 256
    n_gain = gain.shape[1]
    assert n_out % tn == 0 and n_gain % tn == 0
    gn = jnp.zeros((1, n_out), F32).at[:, :n_gain].set(gain)
    vec = pl.BlockSpec((None, 1, d), lambda bi, i: (bi, 0, 0))
    return pl.pallas_call(
        functools.partial(_inproj_kernel, n_gain=n_gain, head_norm=head_norm, tn=tn),
        out_shape=jax.ShapeDtypeStruct((b, s, n_out), BF16),
        grid=(b, s // tm),
        in_specs=[
            pl.BlockSpec((None, tm, d), lambda bi, i: (bi, i, 0)),
            pl.BlockSpec((1, d), lambda bi, i: (0, 0)),
            vec, vec,
            pl.BlockSpec((d, n_out), lambda bi, i: (0, 0)),
            pl.BlockSpec((1, n_out), lambda bi, i: (0, 0)),
        ],
        out_specs=pl.BlockSpec((None, tm, n_out), lambda bi, i: (bi, i, 0)),
        compiler_params=_cparams("parallel", "parallel"),
        name="inproj",
    )(x, g.reshape(1, d), sc, sh, w.astype(BF16), gn)


def _outproj_kernel(*refs, n_in):
    ins, ws = refs[:n_in], refs[n_in:2 * n_in]
    x_ref, g_ref, o_ref = refs[2 * n_in:]
    acc = jnp.dot(ins[0][...], ws[0][...], preferred_element_type=F32)
    for a, w in zip(ins[1:], ws[1:]):
        acc += jnp.dot(a[...], w[...], preferred_element_type=F32)
    o_ref[...] = x_ref[...] + g_ref[...] * acc


def _outproj_residual(parts, weights, x, gate):
    b, s, d = x.shape
    tm = _row_tile(s, 512)
    n_in = len(parts)
    in_specs = [pl.BlockSpec((None, tm, p.shape[-1]), lambda bi, i: (bi, i, 0)) for p in parts]
    in_specs += [pl.BlockSpec(w.shape, lambda bi, i: (0, 0)) for w in weights]
    in_specs += [pl.BlockSpec((None, tm, d), lambda bi, i: (bi, i, 0)),
                 pl.BlockSpec((None, 1, d), lambda bi, i: (bi, 0, 0))]
    return pl.pallas_call(
        functools.partial(_outproj_kernel, n_in=n_in),
        out_shape=jax.ShapeDtypeStruct((b, s, d), F32),
        grid=(b, s // tm),
        in_specs=in_specs,
        out_specs=pl.BlockSpec((None, tm, d), lambda bi, i: (bi, i, 0)),
        compiler_params=_cparams("parallel", "parallel"),
        name="outproj",
    )(*parts, *[w.astype(BF16) for w in weights], x, gate)


LOG2E = 1.4426950408889634
MASKED_LOG2 = -1e30


def _dot_nt(a, b):
    return lax.dot_general(a, b, (((1,), (1,)), ((), ())), preferred_element_type=F32)


def _sb_kernel(q_ref, k_ref, v_ref, o_ref, acc_ref, z_ref, w_ref, *, tq, tk):
    i = pl.program_id(2)
    n_heads = LANES // HEAD_DIM
    n_diag = tq // tk
    lane = lax.broadcasted_iota(I32, (1, LANES), 1)
    r = lax.broadcasted_iota(I32, (tk, tk), 0)
    c = lax.broadcasted_iota(I32, (tk, tk), 1)
    later = jnp.where(r >= c, 1.0, 0.0).astype(BF16)
    row = lax.broadcasted_iota(I32, (tq, tk), 0)
    col = lax.broadcasted_iota(I32, (tq, tk), 1)
    q = q_ref[...]
    qs = [jnp.where((lane >= hh * HEAD_DIM) & (lane < (hh + 1) * HEAD_DIM), q, jnp.zeros_like(q))
          for hh in range(n_heads)]

    def key_tile(ref, j):
        return ref[pl.ds(pl.multiple_of(j * tk, tk), tk), :]

    def step(slot, j, j_next, j_prev, state, causal):
        carry, carry_prev = state
        new_carry = []
        for hh in range(n_heads):
            a = jnp.exp2(w_ref[1 - slot, hh] - carry_prev[hh])
            acc_ref[hh] += jnp.dot(a.astype(BF16), key_tile(v_ref, j_prev), preferred_element_type=F32)
        for hh in range(n_heads):
            z = z_ref[slot, hh]
            neg_abs = pltpu.bitcast(pltpu.bitcast(z, U32) | jnp.uint32(0x80000000), F32)
            fail = jnp.maximum(z, 0.0) + jnp.log2(1.0 + jnp.exp2(neg_abs))
            if causal is not None:
                fail = jnp.where(causal, fail, 0.0)
            incl = jnp.dot(fail.astype(BF16), later, preferred_element_type=F32)
            w = z - incl
            if causal is not None:
                w = jnp.where(causal, w, MASKED_LOG2)
            w_ref[slot, hh] = w
            new_carry.append(carry[hh] + incl[:, 0:1])
        for hh in range(n_heads):
            z_ref[1 - slot, hh] = _dot_nt(qs[hh], key_tile(k_ref, j_next))
        return tuple(new_carry), carry

    assert n_diag == 2
    top = i * n_diag + 1
    acc_ref[...] = jnp.zeros_like(acc_ref)
    w_ref[1] = jnp.full(w_ref.shape[1:], MASKED_LOG2, F32)
    for hh in range(n_heads):
        z_ref[0, hh] = _dot_nt(qs[hh], key_tile(k_ref, top))
    zero = tuple(jnp.zeros((tq, 1), F32) for _ in range(n_heads))
    state = (zero, zero)
    state = step(0, top, top - 1, top, state, col + tk < row)
    state = step(1, top - 1, jnp.maximum(top - 2, 0), top, state, col < row)

    def body(jj, state):
        ja = top - 2 - 2 * jj
        state = step(0, ja, ja - 1, ja + 1, state, None)
        return step(1, ja - 1, jnp.maximum(ja - 2, 0), ja, state, None)

    carry, carry_prev = lax.fori_loop(0, i, body, state)
    for hh in range(n_heads):
        a = jnp.exp2(w_ref[1, hh] - carry_prev[hh])
        acc_ref[hh] += jnp.dot(a.astype(BF16), key_tile(v_ref, 0), preferred_element_type=F32)
    o_ref[...] = jnp.where(lane < HEAD_DIM, acc_ref[0], acc_ref[1]).astype(o_ref.dtype)


def _stick_breaking(proj, width):
    b, s, _ = proj.shape
    tk = _row_tile(s, 256)
    tq = _row_tile(s, 2 * tk)
    nblk = width // LANES
    return pl.pallas_call(
        functools.partial(_sb_kernel, tq=tq, tk=tk),
        out_shape=jax.ShapeDtypeStruct((b, s, width), BF16),
        grid=(b, nblk, s // tq),
        in_specs=[
            pl.BlockSpec((None, tq, LANES), lambda bi, h, i: (bi, i, h)),
            pl.BlockSpec((None, s, LANES), lambda bi, h, i: (bi, 0, nblk + h)),
            pl.BlockSpec((None, s, LANES), lambda bi, h, i: (bi, 0, 2 * nblk + h)),
        ],
        out_specs=pl.BlockSpec((None, tq, LANES), lambda bi, h, i: (bi, i, h)),
        scratch_shapes=[pltpu.VMEM((LANES // HEAD_DIM, tq, LANES), F32),
                        pltpu.VMEM((2, LANES // HEAD_DIM, tq, tk), F32),
                        pltpu.VMEM((2, LANES // HEAD_DIM, tq, tk), F32)],
        compiler_params=_cparams("parallel", "parallel", "parallel"),
        name="stick_breaking",
    )(proj, proj, proj)


def _conv_kernel(a_ref, g_ref, pa_ref, pg_ref, bglu_ref, dw_ref, dwb_ref, lng_ref, lnb_ref, o_ref, hbuf, *, ts, rc):
    i = pl.program_id(1)
    w = a_ref.shape[-1]

    def glu(a, g):
        a = a.astype(F32) + bglu_ref[:, :w]
        g = g.astype(F32) + bglu_ref[:, w:]
        return a * jax.nn.sigmoid(g)

    prev = glu(pa_ref[...], pg_ref[...])
    hbuf[0, 0:CONV_HALO] = jnp.where(i > 0, prev, 0.0)
    hbuf[0, CONV_HALO:CONV_HALO + ts] = glu(a_ref[...], g_ref[...])
    off = CONV_HALO - (CONV_KERNEL - 1)
    n_phase = hbuf.shape[0]
    for p in range(1, n_phase):
        hbuf[p, 0:CONV_HALO + ts - n_phase] = hbuf[0, p:p + CONV_HALO + ts - n_phase]
    for r0 in range(0, ts, rc):
        acc = jnp.zeros((rc, w), F32) + dwb_ref[...]
        for k in range(CONV_KERNEL):
            p = (off + k) % n_phase
            start = r0 + off + k - p
            acc = acc + dw_ref[k:k + 1, :] * hbuf[p, start:start + rc, :]
        mu = jnp.mean(acc, axis=-1, keepdims=True)
        cen = acc - mu
        var = jnp.mean(cen * cen, axis=-1, keepdims=True)
        y = cen * lax.rsqrt(var + EPS) * lng_ref[...] + lnb_ref[...]
        o_ref[r0:r0 + rc, :] = (y * jax.nn.sigmoid(y)).astype(o_ref.dtype)


def _conformer_conv(proj, col0, width, b_glu, dw, dw_b, ln_g, ln_b):
    b, s, _ = proj.shape
    ts = _row_tile(s, 256)
    assert col0 % width == 0 and ts % CONV_HALO == 0
    cb = col0 // width
    hb = ts // CONV_HALO
    cur = lambda off: pl.BlockSpec((None, ts, width), lambda bi, i: (bi, i, cb + off))
    prev = lambda off: pl.BlockSpec((None, CONV_HALO, width),
                                    lambda bi, i: (bi, jnp.maximum(i * hb - 1, 0), cb + off))
    row = lambda n: pl.BlockSpec((1, n), lambda bi, i: (0, 0))
    return pl.pallas_call(
        functools.partial(_conv_kernel, ts=ts, rc=32),
        out_shape=jax.ShapeDtypeStruct((b, s, width), BF16),
        grid=(b, s // ts),
        in_specs=[cur(0), cur(1), prev(0), prev(1), row(2 * width),
                  pl.BlockSpec((CONV_KERNEL, width), lambda bi, i: (0, 0)),
                  row(width), row(width), row(width)],
        out_specs=pl.BlockSpec((None, ts, width), lambda bi, i: (bi, i, 0)),
        scratch_shapes=[pltpu.VMEM((8, CONV_HALO + ts, width), F32)],
        compiler_params=_cparams("parallel", "parallel"),
        name="conformer_conv",
    )(proj, proj, proj, proj, b_glu.reshape(1, -1), dw, dw_b.reshape(1, -1),
      ln_g.reshape(1, -1), ln_b.reshape(1, -1))


def _diff_kernel(slope_ref, q_ref, k_ref, vt_ref, lq1_ref, lk1_ref, lq2_ref, lk2_ref, subg_ref, o_ref,
                 acc_ref, s_ref, p_ref, *, tq, tk, lam_init):
    h = pl.program_id(1)
    i = pl.program_id(2)
    n_diag = tq // tk
    slope = slope_ref[h]
    lane = lax.broadcasted_iota(I32, (1, LANES), 1)
    key = lax.broadcasted_iota(I32, (tk, tq), 0)
    qry = lax.broadcasted_iota(I32, (tk, tq), 1)
    rel = (qry - key).astype(F32)
    bias_past = -slope * rel
    q = q_ref[...]
    zero = jnp.zeros_like(q)
    qs = (jnp.where(lane < HEAD_DIM, q, zero), jnp.where(lane >= HEAD_DIM, q, zero))

    def scores(slot, j, bias):
        kt = k_ref[pl.ds(pl.multiple_of(j * tk, tk), tk), :]
        tile_max = []
        for n in range(2):
            s = _dot_nt(kt, qs[n]) + bias
            s_ref[slot, n] = s
            tile_max.append(jnp.max(s, axis=0, keepdims=True))
        return tuple(tile_max)

    def weighted_values(slot, j, alpha):
        vt = vt_ref[:, pl.ds(pl.multiple_of(j * tk, tk), tk)]
        for n in range(2):
            acc_ref[n] = alpha[n] * acc_ref[n] + jnp.dot(vt, p_ref[slot, n], preferred_element_type=F32)

    def step(slot, j_next, j_prev, state, bias_next, shift):
        stats, alpha_prev, tile_max = state
        weighted_values(1 - slot, j_prev, alpha_prev)
        new_stats, alphas = [], []
        for n in range(2):
            m, l = stats[n]
            m_new = jnp.maximum(m, tile_max[n] + shift)
            alpha = jnp.exp2(m - m_new)
            p = jnp.exp2(s_ref[slot, n] - (m_new - shift))
            p_ref[slot, n] = p.astype(BF16)
            new_stats.append((m_new, alpha * l + jnp.sum(p, axis=0, keepdims=True)))
            alphas.append(alpha)
        return tuple(new_stats), tuple(alphas), scores(1 - slot, j_next, bias_next)

    def diag_bias(dj):
        allowed = _div_pow2(key + dj * tk, CHUNK) <= _div_pow2(qry, CHUNK)
        return jnp.where(allowed, -slope * jnp.abs(rel - float(dj * tk)), -jnp.inf)

    def shift_of(j):
        return -slope * ((i * n_diag - j) * tk).astype(F32)

    assert n_diag == 2
    top = i * n_diag + 1
    acc_ref[...] = jnp.zeros_like(acc_ref)
    p_ref[1] = jnp.zeros(p_ref.shape[1:], BF16)
    floor = jnp.full((1, tq), MASKED_LOG2, F32)
    zl = jnp.zeros((1, tq), F32)
    one = jnp.ones((1, tq), F32)
    state = (((floor, zl), (floor, zl)), (one, one), scores(0, top, diag_bias(1)))
    state = step(0, top - 1, top, state, diag_bias(0), 0.0)
    state = step(1, jnp.maximum(top - 2, 0), top, state, bias_past, 0.0)

    def body(jj, state):
        ja = top - 2 - 2 * jj
        state = step(0, ja - 1, ja + 1, state, bias_past, shift_of(ja))
        return step(1, jnp.maximum(ja - 2, 0), ja, state, bias_past, shift_of(ja - 1))

    ((m1, l1), (m2, l2)), alpha_last, _ = lax.fori_loop(0, i, body, state)
    weighted_values(1, 0, alpha_last)
    lam = (jnp.exp(jnp.sum(lq1_ref[...] * lk1_ref[...], axis=-1, keepdims=True))
           - jnp.exp(jnp.sum(lq2_ref[...] * lk2_ref[...], axis=-1, keepdims=True)) + lam_init)
    o = acc_ref[0] / l1 - lam * (acc_ref[1] / l2)
    ms = jnp.mean(o * o, axis=0, keepdims=True)
    o = o * lax.rsqrt(ms + EPS) * (1.0 - lam_init)
    o_ref[...] = (o.T * subg_ref[...]).astype(o_ref.dtype)


def _diff_attention(proj, n_heads, lq1, lk1, lq2, lk2, sub_g, lam_init):
    b, s, _ = proj.shape
    tk = _row_tile(s, 256)
    tq = _row_tile(s, 2 * tk)
    assert tk % CHUNK == 0
    v_t = jnp.swapaxes(proj[:, :, 2 * n_heads * LANES:], 1, 2)
    slopes = LOG2E * jnp.exp2(-ALIBI_MAX_EXP * jnp.arange(1, n_heads + 1, dtype=F32) / n_heads)
    vec = lambda n: pl.BlockSpec((1, n), lambda bi, h, i, sl: (0, 0))
    return pl.pallas_call(
        functools.partial(_diff_kernel, tq=tq, tk=tk, lam_init=lam_init),
        out_shape=jax.ShapeDtypeStruct((b, s, n_heads * LANES), BF16),
        grid_spec=pltpu.PrefetchScalarGridSpec(
            num_scalar_prefetch=1,
            grid=(b, n_heads, s // tq),
            in_specs=[
                pl.BlockSpec((None, tq, LANES), lambda bi, h, i, sl: (bi, i, h)),
                pl.BlockSpec((None, s, LANES), lambda bi, h, i, sl: (bi, 0, n_heads + h)),
                pl.BlockSpec((None, LANES, s), lambda bi, h, i, sl: (bi, h, 0)),
                vec(HEAD_DIM), vec(HEAD_DIM), vec(HEAD_DIM), vec(HEAD_DIM), vec(LANES),
            ],
            out_specs=pl.BlockSpec((None, tq, LANES), lambda bi, h, i, sl: (bi, i, h)),
            scratch_shapes=[pltpu.VMEM((2, LANES, tq), F32), pltpu.VMEM((2, 2, tk, tq), F32),
                            pltpu.VMEM((2, 2, tk, tq), BF16)],
        ),
        compiler_params=_cparams("parallel", "parallel", "parallel"),
        name="diff_attention",
    )(slopes, proj, proj, v_t, lq1.reshape(1, -1), lk1.reshape(1, -1), lq2.reshape(1, -1),
      lk2.reshape(1, -1), sub_g.reshape(1, -1))


ROUTE_E0, ROUTE_E1, ROUTE_W0, ROUTE_W1, ROUTE_R0, ROUTE_R1 = range(6)
DMA_ISSUE_UNROLL = 8


def _pack_bf16_pair(lo, hi):
    lo_bits = pltpu.bitcast(lo.astype(BF16).astype(F32), U32) >> 16
    hi_bits = pltpu.bitcast(hi.astype(BF16).astype(F32), U32) & jnp.uint32(0xFFFF0000)
    return hi_bits | lo_bits


def _unpack_bf16_pair(u):
    lo = pltpu.bitcast(u << 16, F32).astype(BF16)
    hi = pltpu.bitcast(u & jnp.uint32(0xFFFF0000), F32).astype(BF16)
    return lo, hi


def _router_kernel(x_ref, g_ref, sc_ref, sh_ref, wr_ref, br_ref, hp_ref, rt_ref, cnt_ref, run_ref, *, tm):
    first = (pl.program_id(0) == 0) & (pl.program_id(1) == 0)

    @pl.when(first)
    def _():
        run_ref[...] = jnp.zeros_like(run_ref)

    h = _modulated_norm(x_ref[...], g_ref[...], sc_ref[...], sh_ref[...])
    half = h.shape[-1] // 2
    hp_ref[...] = _pack_bf16_pair(h[:, :half], h[:, half:])

    logits = jnp.dot(h, wr_ref[...], preferred_element_type=F32, precision=HIGHEST) + br_ref[...]
    lane = lax.broadcasted_iota(I32, (tm, LANES), 1)
    lanef = lane.astype(F32)
    ninf = -jnp.inf
    big = float(LANES)

    def first_argmax(vals):
        top = jnp.max(vals, axis=-1, keepdims=True)
        idx = jnp.min(jnp.where(vals == top, lanef, big), axis=-1, keepdims=True)
        return top, idx

    is_group = lane < N_GROUPS
    gl = jnp.where(is_group, logits, ninf)
    gmax, gidx = first_argmax(gl)
    g_w = 1.0 / jnp.sum(jnp.where(is_group, jnp.exp(gl - gmax), 0.0), axis=-1, keepdims=True)
    lane_group = _div_pow2(lane - N_GROUPS, EXPERTS_PER_GROUP).astype(F32)
    in_group = (lane >= N_GROUPS) & (lane < N_GROUPS + N_EXPERTS) & (lane_group == gidx)
    el = jnp.where(in_group, logits, ninf)
    v0, i0 = first_argmax(el)
    v1, i1 = first_argmax(jnp.where(lanef == i0, ninf, el))
    tt = jnp.exp(v1 - v0)
    w0 = g_w / (1.0 + tt)
    w1 = g_w * tt / (1.0 + tt)
    e0 = i0 - N_GROUPS
    e1 = i1 - N_GROUPS

    member = (lanef == e0) | (lanef == e1)
    r = lax.broadcasted_iota(I32, (tm, tm), 0)
    c = lax.broadcasted_iota(I32, (tm, tm), 1)
    earlier = jnp.where(c < r, 1.0, 0.0).astype(BF16)
    before = jnp.dot(earlier, jnp.where(member, 1.0, 0.0).astype(BF16), preferred_element_type=F32)
    before = before + run_ref[...]
    r0 = jnp.sum(jnp.where(lanef == e0, before, 0.0), axis=-1, keepdims=True)
    r1 = jnp.sum(jnp.where(lanef == e1, before, 0.0), axis=-1, keepdims=True)
    run_ref[...] += jnp.sum(jnp.where(member, 1.0, 0.0), axis=0, keepdims=True)
    cnt_ref[...] = run_ref[...]

    slab = jnp.zeros((tm, LANES), F32)
    for colv, val in ((ROUTE_E0, e0), (ROUTE_E1, e1), (ROUTE_W0, w0), (ROUTE_W1, w1), (ROUTE_R0, r0), (ROUTE_R1, r1)):
        slab = jnp.where(lane == colv, val, slab)
    rt_ref[...] = slab


def _router(x, g, sc, sh, w_group, b_group, w_router, b_router):
    b, s, d = x.shape
    tm = _row_tile(s, 512)
    wr = jnp.zeros((d, LANES), F32).at[:, :N_GROUPS].set(w_group).at[:, N_GROUPS:N_GROUPS + N_EXPERTS].set(w_router)
    br = jnp.zeros((1, LANES), F32).at[0, :N_GROUPS].set(b_group).at[0, N_GROUPS:N_GROUPS + N_EXPERTS].set(b_router)
    vec = pl.BlockSpec((None, 1, d), lambda bi, i: (bi, 0, 0))
    return pl.pallas_call(
        functools.partial(_router_kernel, tm=tm),
        out_shape=(jax.ShapeDtypeStruct((b, s, d // 2), U32),
                   jax.ShapeDtypeStruct((b, s, LANES), F32),
                   jax.ShapeDtypeStruct((1, LANES), F32)),
        grid=(b, s // tm),
        in_specs=[
            pl.BlockSpec((None, tm, d), lambda bi, i: (bi, i, 0)),
            pl.BlockSpec((1, d), lambda bi, i: (0, 0)),
            vec, vec,
            pl.BlockSpec((d, LANES), lambda bi, i: (0, 0)),
            pl.BlockSpec((1, LANES), lambda bi, i: (0, 0)),
        ],
        out_specs=(pl.BlockSpec((None, tm, d // 2), lambda bi, i: (bi, i, 0)),
                   pl.BlockSpec((None, tm, LANES), lambda bi, i: (bi, i, 0)),
                   pl.BlockSpec((1, LANES), lambda bi, i: (0, 0))),
        scratch_shapes=[pltpu.VMEM((1, LANES), F32)],
        compiler_params=_cparams("arbitrary", "arbitrary"),
        name="moe_router",
    )(x, g.reshape(1, d), sc, sh, wr, br)


def _zero_kernel(o_ref):
    o_ref[...] = jnp.zeros_like(o_ref)


def _zeros_u32(rows, cols):
    tr = _row_tile(rows, 2048)
    return pl.pallas_call(
        _zero_kernel,
        out_shape=jax.ShapeDtypeStruct((rows, cols), U32),
        grid=(rows // tr,),
        out_specs=pl.BlockSpec((tr, cols), lambda i: (i, 0)),
        compiler_params=_cparams("parallel"),
        name="moe_zero_slots",
    )()


def _dispatch_kernel(dest_ref, h_ref, xs_in_ref, xs_ref, sem, *, chunk):
    del xs_in_ref
    base = pl.program_id(0) * chunk

    def body(r, carry):
        for k in range(2):
            pltpu.make_async_copy(h_ref.at[pl.ds(r, 1), :],
                                  xs_ref.at[pl.ds(dest_ref[2 * (base + r) + k], 1), :], sem).start()
        return carry

    lax.fori_loop(0, chunk, body, 0, unroll=DMA_ISSUE_UNROLL)
    for _ in range(2):
        pltpu.make_async_copy(h_ref, xs_ref.at[pl.ds(0, chunk), :], sem).wait()


def _dispatch(dest, h_packed, n_slots):
    n, w = h_packed.shape
    chunk = _row_tile(n, 256)
    return pl.pallas_call(
        functools.partial(_dispatch_kernel, chunk=chunk),
        out_shape=jax.ShapeDtypeStruct((n_slots, w), U32),
        grid_spec=pltpu.PrefetchScalarGridSpec(
            num_scalar_prefetch=1,
            grid=(n // chunk,),
            in_specs=[pl.BlockSpec((chunk, w), lambda i, d: (i, 0)), pl.BlockSpec(memory_space=pl.ANY)],
            out_specs=pl.BlockSpec(memory_space=pl.ANY),
            scratch_shapes=[pltpu.SemaphoreType.DMA(())],
        ),
        input_output_aliases={2: 0},
        compiler_params=_cparams("arbitrary"),
        name="moe_dispatch",
    )(dest, h_packed, _zeros_u32(n_slots, w))


def _expert_kernel(be_ref, nv_ref, xs_ref, wg_ref, wu_ref, wd_ref, ys_ref, wg_b, wu_b, wd_b):
    blk = pl.program_id(0)
    valid = blk < nv_ref[0]
    changed = (blk == 0) | (be_ref[blk] != be_ref[jnp.maximum(blk - 1, 0)])

    @pl.when(valid & changed)
    def _():
        wg_b[...] = wg_ref[...].astype(BF16)
        wu_b[...] = wu_ref[...].astype(BF16)
        wd_b[...] = wd_ref[...].astype(BF16)

    @pl.when(valid)
    def _():
        x_lo, x_hi = _unpack_bf16_pair(xs_ref[...])
        half = x_lo.shape[-1]

        def proj(w):
            return (jnp.dot(x_lo, w[:half, :], preferred_element_type=F32)
                    + jnp.dot(x_hi, w[half:, :], preferred_element_type=F32))

        gate = proj(wg_b)
        hid = (gate * jax.nn.sigmoid(gate)) * proj(wu_b)
        ys_ref[...] = jnp.dot(hid.astype(BF16), wd_b[...], preferred_element_type=F32)

    @pl.when(jnp.logical_not(valid))
    def _():
        ys_ref[...] = jnp.zeros_like(ys_ref)


def _experts(block_e, n_valid, xs, w_gate, w_up, w_down, layer):
    n_slots, half = xs.shape
    d = 2 * half
    de = w_gate.shape[-1]
    n_blocks = n_slots // MOE_BLOCK
    last = lambda blk, nv: jnp.minimum(blk, nv[0] - 1)
    w_in = pl.BlockSpec((None, None, d, de), lambda blk, be, nv: (layer, be[last(blk, nv)], 0, 0))
    return pl.pallas_call(
        _expert_kernel,
        out_shape=jax.ShapeDtypeStruct((n_slots, d), F32),
        grid_spec=pltpu.PrefetchScalarGridSpec(
            num_scalar_prefetch=2,
            grid=(n_blocks,),
            in_specs=[pl.BlockSpec((MOE_BLOCK, half), lambda blk, be, nv: (last(blk, nv), 0)),
                      w_in, w_in,
                      pl.BlockSpec((None, None, de, d), lambda blk, be, nv: (layer, be[last(blk, nv)], 0, 0))],
            out_specs=pl.BlockSpec((MOE_BLOCK, d), lambda blk, be, nv: (blk, 0)),
            scratch_shapes=[pltpu.VMEM((d, de), BF16), pltpu.VMEM((d, de), BF16), pltpu.VMEM((de, d), BF16)],
        ),
        compiler_params=_cparams("arbitrary"),
        name="moe_experts",
    )(block_e, n_valid, xs, w_gate, w_up, w_down)


def _combine_kernel(dest_ref, x_ref, rt_ref, g_ref, ys_ref, o_ref, buf, sems, *, tm):
    i = pl.program_id(0)
    cur = lax.rem(i, 2)

    def gather(tile, half):
        base = tile * tm

        def body(r, carry):
            for k in range(2):
                pltpu.make_async_copy(ys_ref.at[pl.ds(dest_ref[2 * (base + r) + k], 1), :],
                                      buf.at[half, k, pl.ds(r, 1), :], sems.at[half]).start()
            return carry

        lax.fori_loop(0, tm, body, 0, unroll=DMA_ISSUE_UNROLL)

    @pl.when(i == 0)
    def _():
        gather(0, 0)

    @pl.when(i + 1 < pl.num_programs(0))
    def _():
        gather(i + 1, 1 - cur)

    for k in range(2):
        pltpu.make_async_copy(ys_ref.at[pl.ds(0, tm), :], buf.at[cur, k], sems.at[cur]).wait()
    rt = rt_ref[...]
    w0 = rt[:, ROUTE_W0:ROUTE_W0 + 1]
    w1 = rt[:, ROUTE_W1:ROUTE_W1 + 1]
    o_ref[...] = x_ref[...] + g_ref[...] * (w0 * buf[cur, 0] + w1 * buf[cur, 1])


def _combine(dest, x2, route, gate, ys, tiles_per_batch_of):
    n, d = x2.shape
    tm = _row_tile(n, 256)
    per_batch = tiles_per_batch_of(tm)
    return pl.pallas_call(
        functools.partial(_combine_kernel, tm=tm),
        out_shape=jax.ShapeDtypeStruct((n, d), F32),
        grid_spec=pltpu.PrefetchScalarGridSpec(
            num_scalar_prefetch=1,
            grid=(n // tm,),
            in_specs=[pl.BlockSpec((tm, d), lambda i, de: (i, 0)),
                      pl.BlockSpec((tm, LANES), lambda i, de: (i, 0)),
                      pl.BlockSpec((None, 1, d), lambda i, de: (i // per_batch, 0, 0)),
                      pl.BlockSpec(memory_space=pl.ANY)],
            out_specs=pl.BlockSpec((tm, d), lambda i, de: (i, 0)),
            scratch_shapes=[pltpu.VMEM((2, 2, tm, d), F32), pltpu.SemaphoreType.DMA((2,))],
        ),
        compiler_params=_cparams("arbitrary"),
        name="moe_combine",
    )(dest, x2, route, gate, ys)


def _moe_layer(x, g, sc, sh, gate, w_group, b_group, w_router, b_router, w_gate, w_up, w_down, layer):
    b, s, d = x.shape
    n = b * s
    h_packed, route, counts = _router(x, g, sc, sh, w_group, b_group, w_router, b_router)
    route = route.reshape(n, LANES)
    counts = counts[0, :N_EXPERTS].astype(I32)
    padded = (counts + MOE_BLOCK - 1) // MOE_BLOCK * MOE_BLOCK
    pad_end = jnp.cumsum(padded)
    pad_start = pad_end - padded
    e_id = route[:, ROUTE_E0:ROUTE_E1 + 1].astype(I32)
    rank = route[:, ROUTE_R0:ROUTE_R1 + 1].astype(I32)
    is_e = e_id[:, :, None] == jnp.arange(N_EXPERTS, dtype=I32)
    dest = (jnp.sum(jnp.where(is_e, pad_start, 0), axis=-1) + rank).reshape(-1)
    n_blocks = -(-(n * 2) // MOE_BLOCK) + N_EXPERTS
    block_start = jnp.arange(n_blocks, dtype=I32) * MOE_BLOCK
    block_e = jnp.minimum(jnp.sum((pad_end[None, :] <= block_start[:, None]).astype(I32), axis=1), N_EXPERTS - 1)
    n_valid = (pad_end[-1:] // MOE_BLOCK).astype(I32)
    xs = _dispatch(dest, h_packed.reshape(n, d // 2), n_blocks * MOE_BLOCK)
    ys = _experts(block_e, n_valid, xs, w_gate, w_up, w_down, layer)
    out = _combine(dest, x.reshape(n, d), route, gate, ys, lambda tm: s // tm)
    return out.reshape(b, s, d)


def kernel(x, c, w_ada, b_ada, mix_norm_g, ffn_norm_g, even_w_in, conv_b_glu, conv_dw, conv_dw_b, conv_ln_g, conv_ln_b, even_w_out, odd_w_in, qk_norm_q, qk_norm_k, lambda_q1, lambda_k1, lambda_q2, lambda_k2, diff_sub_g, odd_w_out, moe_w_group, moe_b_group, moe_w_router, moe_b_router, moe_w_gate, moe_w_up, moe_w_down):
    depth = w_ada.shape[0]
    d = x.shape[-1]
    sb_width = d // 2
    conv_width = d // 2
    diff_heads = d // (2 * HEAD_DIM)
    qk_width = diff_heads * 2 * HEAD_DIM
    mod = _ada_mod(c, w_ada, b_ada)
    for layer in range(depth):
        sh1, sc1, g1, sh2, sc2, g2 = [m[:, None, :] for m in jnp.split(mod[layer], 6, axis=-1)]
        j = layer // 2
        if layer % 2 == 0:
            q_gain = jnp.full((1, sb_width), LOG2E * HEAD_DIM ** -0.5, F32)
            proj = _inproj(x, mix_norm_g[layer], sc1, sh1, even_w_in[j], q_gain, head_norm=False)
            a_out = _stick_breaking(proj, sb_width)
            b_out = _conformer_conv(proj, 3 * sb_width, conv_width, conv_b_glu[j], conv_dw[j], conv_dw_b[j],
                                    conv_ln_g[j], conv_ln_b[j])
            x = _outproj_residual([a_out, b_out], [even_w_out[j][:sb_width], even_w_out[j][sb_width:]], x, g1)
        else:
            lam_init = 0.8 - 0.6 * math.exp(-0.3 * layer)
            reps = qk_width // HEAD_DIM
            qk_gain = jnp.concatenate([jnp.tile(qk_norm_q[j] * (LOG2E * HEAD_DIM ** -0.5), reps),
                                       jnp.tile(qk_norm_k[j], reps)])[None, :]
            proj = _inproj(x, mix_norm_g[layer], sc1, sh1, odd_w_in[j], qk_gain, head_norm=True)
            o = _diff_attention(proj, diff_heads, lambda_q1[j], lambda_k1[j], lambda_q2[j], lambda_k2[j],
                                diff_sub_g[j], lam_init)
            x = _outproj_residual([o], [odd_w_out[j]], x, g1)
        x = _moe_layer(x, ffn_norm_g[layer], sc2, sh2, g2, moe_w_group[layer], moe_b_group[layer],
                       moe_w_router[layer], moe_b_router[layer], moe_w_gate, moe_w_up, moe_w_down, layer)
    return x
```

```python
import functools
import math

import jax
import jax.numpy as jnp
from jax import lax
from jax.experimental import pallas as pl
from jax.experimental.pallas import tpu as pltpu

F32 = jnp.float32
BF16 = jnp.bfloat16
U32 = jnp.uint32
I32 = jnp.int32

EPS = 1e-6
LANES = 128
HEAD_DIM = 64
CHUNK = 64
CONV_KERNEL = 31
CONV_HALO = 32
N_GROUPS = 4
EXPERTS_PER_GROUP = 8
N_EXPERTS = N_GROUPS * EXPERTS_PER_GROUP
MOE_BLOCK = 512
ALIBI_MAX_EXP = 8.0
VMEM_LIMIT_BYTES = 56 * 1024 * 1024
HIGHEST = lax.Precision.HIGHEST


def _cparams(*sem):
    return pltpu.CompilerParams(dimension_semantics=sem, vmem_limit_bytes=VMEM_LIMIT_BYTES)


def _div_pow2(v, n):
    shift = n.bit_length() - 1
    assert 1 << shift == n
    return lax.shift_right_arithmetic(v, jnp.int32(shift))


def _row_tile(n, want):
    t = min(n, want)
    assert n % t == 0
    return t


def _ada_kernel(c_ref, w_ref, b_ref, o_ref):
    c = c_ref[...]
    c_act = c * jax.nn.sigmoid(c)
    o_ref[...] = jnp.dot(c_act, w_ref[...], preferred_element_type=F32, precision=HIGHEST) + b_ref[...]


def _ada_mod(c, w_ada, b_ada):
    depth, d, d6 = w_ada.shape
    b = c.shape[0]
    rows = 8
    c_pad = jnp.zeros((rows, d), F32).at[:b].set(c)
    tn = _row_tile(d6, 1536)
    out = pl.pallas_call(
        _ada_kernel,
        out_shape=jax.ShapeDtypeStruct((depth, rows, d6), F32),
        grid=(depth, d6 // tn),
        in_specs=[
            pl.BlockSpec((rows, d), lambda l, j: (0, 0)),
            pl.BlockSpec((None, d, tn), lambda l, j: (l, 0, j)),
            pl.BlockSpec((None, 1, tn), lambda l, j: (l, 0, j)),
        ],
        out_specs=pl.BlockSpec((None, rows, tn), lambda l, j: (l, 0, j)),
        compiler_params=_cparams("parallel", "parallel"),
        name="ada_mod",
    )(c_pad, w_ada, b_ada.reshape(depth, 1, d6))
    return out[:, :b]


def _modulated_norm(x, g, sc, sh):
    ms = jnp.mean(x * x, axis=-1, keepdims=True)
    return (x * lax.rsqrt(ms + EPS) * g) * (1.0 + sc) + sh


def _inproj_kernel(x_ref, g_ref, sc_ref, sh_ref, w_ref, gn_ref, o_ref, *, n_gain, head_norm, tn):
    h = _modulated_norm(x_ref[...], g_ref[...], sc_ref[...], sh_ref[...]).astype(BF16)
    n_out = o_ref.shape[-1]
    if head_norm:
        r = _div_pow2(lax.broadcasted_iota(I32, (tn, tn), 0), HEAD_DIM)
        c = _div_pow2(lax.broadcasted_iota(I32, (tn, tn), 1), HEAD_DIM)
        group_ones = jnp.where(r == c, 1.0, 0.0).astype(BF16)
    for c0 in range(0, n_out, tn):
        y = jnp.dot(h, w_ref[:, c0:c0 + tn], preferred_element_type=F32)
        if c0 < n_gain:
            if head_norm:
                gs = jnp.dot((y * y).astype(BF16), group_ones, preferred_element_type=F32)
                y = y * lax.rsqrt(gs * (1.0 / HEAD_DIM) + EPS)
            y = y * gn_ref[:, c0:c0 + tn]
        o_ref[:, c0:c0 + tn] = y.astype(o_ref.dtype)


def _inproj(x, g, sc, sh, w, gain, head_norm):
    b, s, d = x.shape
    n_out = w.shape[1]
    tm = _row_tile(s, 512)
    tn = 256
    n_gain = gain.shape[1]
    assert n_out % tn == 0 and n_gain % tn == 0
    gn = jnp.zeros((1, n_out), F32).at[:, :n_gain].set(gain)
    vec = pl.BlockSpec((None, 1, d), lambda bi, i: (bi, 0, 0))
    return pl.pallas_call(
        functools.partial(_inproj_kernel, n_gain=n_gain, head_norm=head_norm, tn=tn),
        out_shape=jax.ShapeDtypeStruct((b, s, n_out), BF16),
        grid=(b, s // tm),
        in_specs=[
            pl.BlockSpec((None, tm, d), lambda bi, i: (bi, i, 0)),
            pl.BlockSpec((1, d), lambda bi, i: (0, 0)),
            vec, vec,
            pl.BlockSpec((d, n_out), lambda bi, i: (0, 0)),
            pl.BlockSpec((1, n_out), lambda bi, i: (0, 0)),
        ],
        out_specs=pl.BlockSpec((None, tm, n_out), lambda bi, i: (bi, i, 0)),
        compiler_params=_cparams("parallel", "parallel"),
        name="inproj",
    )(x, g.reshape(1, d), sc, sh, w.astype(BF16), gn)


def _outproj_kernel(*refs, n_in):
    ins, ws = refs[:n_in], refs[n_in:2 * n_in]
    x_ref, g_ref, o_ref = refs[2 * n_in:]
    acc = jnp.dot(ins[0][...], ws[0][...], preferred_element_type=F32)
    for a, w in zip(ins[1:], ws[1:]):
        acc += jnp.dot(a[...], w[...], preferred_element_type=F32)
    o_ref[...] = x_ref[...] + g_ref[...] * acc


def _outproj_residual(parts, weights, x, gate):
    b, s, d = x.shape
    tm = _row_tile(s, 512)
    n_in = len(parts)
    in_specs = [pl.BlockSpec((None, tm, p.shape[-1]), lambda bi, i: (bi, i, 0)) for p in parts]
    in_specs += [pl.BlockSpec(w.shape, lambda bi, i: (0, 0)) for w in weights]
    in_specs += [pl.BlockSpec((None, tm, d), lambda bi, i: (bi, i, 0)),
                 pl.BlockSpec((None, 1, d), lambda bi, i: (bi, 0, 0))]
    return pl.pallas_call(
        functools.partial(_outproj_kernel, n_in=n_in),
        out_shape=jax.ShapeDtypeStruct((b, s, d), F32),
        grid=(b, s // tm),
        in_specs=in_specs,
        out_specs=pl.BlockSpec((None, tm, d), lambda bi, i: (bi, i, 0)),
        compiler_params=_cparams("parallel", "parallel"),
        name="outproj",
    )(*parts, *[w.astype(BF16) for w in weights], x, gate)


LOG2E = 1.4426950408889634
MASKED_LOG2 = -1e30
ONES_ROWS = 16


def _dot_nt(a, b):
    return lax.dot_general(a, b, (((1,), (1,)), ((), ())), preferred_element_type=F32)


def _sb_kernel(q_ref, k_ref, v_ref, o_ref, acc_ref, z_ref, w_ref, *, tq, tk):
    i = pl.program_id(2)
    n_heads = LANES // HEAD_DIM
    n_diag = tq // tk
    lane = lax.broadcasted_iota(I32, (1, LANES), 1)
    r = lax.broadcasted_iota(I32, (tk, tk), 0)
    c = lax.broadcasted_iota(I32, (tk, tk), 1)
    later = jnp.where(r >= c, 1.0, 0.0).astype(BF16)
    row = lax.broadcasted_iota(I32, (tq, tk), 0)
    col = lax.broadcasted_iota(I32, (tq, tk), 1)
    q = q_ref[...]
    qs = [jnp.where((lane >= hh * HEAD_DIM) & (lane < (hh + 1) * HEAD_DIM), q, jnp.zeros_like(q))
          for hh in range(n_heads)]

    def key_tile(ref, j):
        return ref[pl.ds(pl.multiple_of(j * tk, tk), tk), :]

    def weighted_values(slot, j, carry_of_tile):
        for hh in range(n_heads):
            a = jnp.exp2((w_ref[slot, hh] - carry_of_tile[hh]).astype(BF16))
            acc_ref[hh] += jnp.dot(a, key_tile(v_ref, j), preferred_element_type=F32)

    def step(slot, j, j_next, j_prev, state, causal):
        carry, carry_prev = state
        new_carry = []
        weighted_values(1 - slot, j_prev, carry_prev)
        for hh in range(n_heads):
            z = z_ref[slot, hh]
            neg_abs = pltpu.bitcast(pltpu.bitcast(z, U32) | jnp.uint32(0x80000000), F32)
            fail = jnp.maximum(z, 0.0) + jnp.log2(1.0 + jnp.exp2(neg_abs))
            if causal is not None:
                fail = jnp.where(causal, fail, 0.0)
            incl = jnp.dot(fail.astype(BF16), later, preferred_element_type=F32)
            w = z - incl
            if causal is not None:
                w = jnp.where(causal, w, MASKED_LOG2)
            w_ref[slot, hh] = w
            new_carry.append(carry[hh] + incl[:, 0:1])
        for hh in range(n_heads):
            z_ref[1 - slot, hh] = _dot_nt(qs[hh], key_tile(k_ref, j_next))
        return tuple(new_carry), carry

    assert n_diag == 2
    top = i * n_diag + 1
    acc_ref[...] = jnp.zeros_like(acc_ref)
    w_ref[1] = jnp.full(w_ref.shape[1:], MASKED_LOG2, F32)
    for hh in range(n_heads):
        z_ref[0, hh] = _dot_nt(qs[hh], key_tile(k_ref, top))
    zero = tuple(jnp.zeros((tq, 1), F32) for _ in range(n_heads))
    state = (zero, zero)
    state = step(0, top, top - 1, top, state, col + tk < row)
    state = step(1, top - 1, jnp.maximum(top - 2, 0), top, state, col < row)

    def body(jj, state):
        ja = top - 2 - 2 * jj
        state = step(0, ja, ja - 1, ja + 1, state, None)
        return step(1, ja - 1, jnp.maximum(ja - 2, 0), ja, state, None)

    carry, carry_prev = lax.fori_loop(0, i, body, state)
    weighted_values(1, 0, carry_prev)
    o_ref[...] = jnp.where(lane < HEAD_DIM, acc_ref[0], acc_ref[1]).astype(o_ref.dtype)


def _stick_breaking(proj, width):
    b, s, _ = proj.shape
    tk = _row_tile(s, 256)
    tq = _row_tile(s, 2 * tk)
    nblk = width // LANES
    return pl.pallas_call(
        functools.partial(_sb_kernel, tq=tq, tk=tk),
        out_shape=jax.ShapeDtypeStruct((b, s, width), BF16),
        grid=(b, nblk, s // tq),
        in_specs=[
            pl.BlockSpec((None, tq, LANES), lambda bi, h, i: (bi, i, h)),
            pl.BlockSpec((None, s, LANES), lambda bi, h, i: (bi, 0, nblk + h)),
            pl.BlockSpec((None, s, LANES), lambda bi, h, i: (bi, 0, 2 * nblk + h)),
        ],
        out_specs=pl.BlockSpec((None, tq, LANES), lambda bi, h, i: (bi, i, h)),
        scratch_shapes=[pltpu.VMEM((LANES // HEAD_DIM, tq, LANES), F32),
                        pltpu.VMEM((2, LANES // HEAD_DIM, tq, tk), F32),
                        pltpu.VMEM((2, LANES // HEAD_DIM, tq, tk), F32)],
        compiler_params=_cparams("parallel", "parallel", "parallel"),
        name="stick_breaking",
    )(proj, proj, proj)


def _conv_kernel(a_ref, g_ref, pa_ref, pg_ref, bglu_ref, dw_ref, dwb_ref, lng_ref, lnb_ref, o_ref, hbuf, *, ts, rc):
    i = pl.program_id(1)
    w = a_ref.shape[-1]

    def glu(a, g):
        a = a.astype(F32) + bglu_ref[:, :w]
        g = g.astype(F32) + bglu_ref[:, w:]
        return a * jax.nn.sigmoid(g)

    prev = glu(pa_ref[...], pg_ref[...])
    hbuf[0, 0:CONV_HALO] = jnp.where(i > 0, prev, 0.0)
    hbuf[0, CONV_HALO:CONV_HALO + ts] = glu(a_ref[...], g_ref[...])
    off = CONV_HALO - (CONV_KERNEL - 1)
    n_phase = hbuf.shape[0]
    for p in range(1, n_phase):
        hbuf[p, 0:CONV_HALO + ts - n_phase] = hbuf[0, p:p + CONV_HALO + ts - n_phase]
    for r0 in range(0, ts, rc):
        acc = jnp.zeros((rc, w), F32) + dwb_ref[...]
        for k in range(CONV_KERNEL):
            p = (off + k) % n_phase
            start = r0 + off + k - p
            acc = acc + dw_ref[k:k + 1, :] * hbuf[p, start:start + rc, :]
        mu = jnp.mean(acc, axis=-1, keepdims=True)
        cen = acc - mu
        var = jnp.mean(cen * cen, axis=-1, keepdims=True)
        y = cen * lax.rsqrt(var + EPS) * lng_ref[...] + lnb_ref[...]
        o_ref[r0:r0 + rc, :] = (y * jax.nn.sigmoid(y)).astype(o_ref.dtype)


def _conformer_conv(proj, col0, width, b_glu, dw, dw_b, ln_g, ln_b):
    b, s, _ = proj.shape
    ts = _row_tile(s, 256)
    assert col0 % width == 0 and ts % CONV_HALO == 0
    cb = col0 // width
    hb = ts // CONV_HALO
    cur = lambda off: pl.BlockSpec((None, ts, width), lambda bi, i: (bi, i, cb + off))
    prev = lambda off: pl.BlockSpec((None, CONV_HALO, width),
                                    lambda bi, i: (bi, jnp.maximum(i * hb - 1, 0), cb + off))
    row = lambda n: pl.BlockSpec((1, n), lambda bi, i: (0, 0))
    return pl.pallas_call(
        functools.partial(_conv_kernel, ts=ts, rc=32),
        out_shape=jax.ShapeDtypeStruct((b, s, width), BF16),
        grid=(b, s // ts),
        in_specs=[cur(0), cur(1), prev(0), prev(1), row(2 * width),
                  pl.BlockSpec((CONV_KERNEL, width), lambda bi, i: (0, 0)),
                  row(width), row(width), row(width)],
        out_specs=pl.BlockSpec((None, ts, width), lambda bi, i: (bi, i, 0)),
        scratch_shapes=[pltpu.VMEM((8, CONV_HALO + ts, width), F32)],
        compiler_params=_cparams("parallel", "parallel"),
        name="conformer_conv",
    )(proj, proj, proj, proj, b_glu.reshape(1, -1), dw, dw_b.reshape(1, -1),
      ln_g.reshape(1, -1), ln_b.reshape(1, -1))


def _diff_kernel(slope_ref, q_ref, k_ref, vt_ref, lq1_ref, lk1_ref, lq2_ref, lk2_ref, subg_ref, o_ref,
                 acc_ref, s_ref, p_ref, *, tq, tk, lam_init):
    h = pl.program_id(1)
    i = pl.program_id(2)
    n_diag = tq // tk
    slope = slope_ref[h]
    lane = lax.broadcasted_iota(I32, (1, LANES), 1)
    key = lax.broadcasted_iota(I32, (tk, tq), 0)
    qry = lax.broadcasted_iota(I32, (tk, tq), 1)
    rel = (qry - key).astype(F32)
    bias_past = -slope * rel
    q = q_ref[...]
    zero = jnp.zeros_like(q)
    qs = (jnp.where(lane < HEAD_DIM, q, zero), jnp.where(lane >= HEAD_DIM, q, zero))

    def scores(slot, j, bias):
        kt = k_ref[pl.ds(pl.multiple_of(j * tk, tk), tk), :]
        tile_max = []
        for n in range(2):
            s = _dot_nt(kt, qs[n]) + bias
            s_ref[slot, n] = s
            tile_max.append(jnp.max(s, axis=0, keepdims=True))
        return tuple(tile_max)

    def weighted_values(slot, j, alpha):
        vt = vt_ref[:, pl.ds(pl.multiple_of(j * tk, tk), tk)]
        for n in range(2):
            acc_ref[n] = alpha[n] * acc_ref[n] + jnp.dot(vt, p_ref[slot, n], preferred_element_type=F32)

    def step(slot, j_next, j_prev, state, bias_next, shift):
        stats, alpha_prev, tile_max = state
        weighted_values(1 - slot, j_prev, alpha_prev)
        new_stats, alphas = [], []
        for n in range(2):
            m_new = jnp.maximum(stats[n], tile_max[n] + shift)
            alphas.append(jnp.exp2(stats[n] - m_new))
            p_ref[slot, n] = jnp.exp2((s_ref[slot, n] - (m_new - shift)).astype(BF16))
            new_stats.append(m_new)
        return tuple(new_stats), tuple(alphas), scores(1 - slot, j_next, bias_next)

    def diag_bias(dj):
        allowed = _div_pow2(key + dj * tk, CHUNK) <= _div_pow2(qry, CHUNK)
        return jnp.where(allowed, -slope * jnp.abs(rel - float(dj * tk)), -jnp.inf)

    def shift_of(j):
        return -slope * ((i * n_diag - j) * tk).astype(F32)

    assert n_diag == 2
    top = i * n_diag + 1
    acc_ref[...] = jnp.zeros_like(acc_ref)
    p_ref[1] = jnp.zeros(p_ref.shape[1:], BF16)
    floor = jnp.full((1, tq), MASKED_LOG2, F32)
    one = jnp.ones((1, tq), F32)
    state = ((floor, floor), (one, one), scores(0, top, diag_bias(1)))
    state = step(0, top - 1, top, state, diag_bias(0), 0.0)
    state = step(1, jnp.maximum(top - 2, 0), top, state, bias_past, 0.0)

    def body(jj, state):
        ja = top - 2 - 2 * jj
        state = step(0, ja - 1, ja + 1, state, bias_past, shift_of(ja))
        return step(1, jnp.maximum(ja - 2, 0), ja, state, bias_past, shift_of(ja - 1))

    _, alpha_last, _ = lax.fori_loop(0, i, body, state)
    weighted_values(1, 0, alpha_last)
    dv = acc_ref.shape[1] - ONES_ROWS
    l1, l2 = acc_ref[0, dv:dv + 1, :], acc_ref[1, dv:dv + 1, :]
    lam = (jnp.exp(jnp.sum(lq1_ref[...] * lk1_ref[...], axis=-1, keepdims=True))
           - jnp.exp(jnp.sum(lq2_ref[...] * lk2_ref[...], axis=-1, keepdims=True)) + lam_init)
    o = acc_ref[0, :dv, :] / l1 - lam * (acc_ref[1, :dv, :] / l2)
    ms = jnp.mean(o * o, axis=0, keepdims=True)
    o = o * lax.rsqrt(ms + EPS) * (1.0 - lam_init)
    o_ref[...] = (o.T * subg_ref[...]).astype(o_ref.dtype)


def _diff_attention(proj, n_heads, lq1, lk1, lq2, lk2, sub_g, lam_init):
    b, s, _ = proj.shape
    tk = _row_tile(s, 256)
    tq = _row_tile(s, 2 * tk)
    assert tk % CHUNK == 0
    v_t = jnp.swapaxes(proj[:, :, 2 * n_heads * LANES:], 1, 2).reshape(b, n_heads, LANES, s)
    v_t = jnp.concatenate([v_t, jnp.ones((b, n_heads, ONES_ROWS, s), BF16)], axis=2)
    slopes = LOG2E * jnp.exp2(-ALIBI_MAX_EXP * jnp.arange(1, n_heads + 1, dtype=F32) / n_heads)
    vec = lambda n: pl.BlockSpec((1, n), lambda bi, h, i, sl: (0, 0))
    return pl.pallas_call(
        functools.partial(_diff_kernel, tq=tq, tk=tk, lam_init=lam_init),
        out_shape=jax.ShapeDtypeStruct((b, s, n_heads * LANES), BF16),
        grid_spec=pltpu.PrefetchScalarGridSpec(
            num_scalar_prefetch=1,
            grid=(b, n_heads, s // tq),
            in_specs=[
                pl.BlockSpec((None, tq, LANES), lambda bi, h, i, sl: (bi, i, h)),
                pl.BlockSpec((None, s, LANES), lambda bi, h, i, sl: (bi, 0, n_heads + h)),
                pl.BlockSpec((None, None, LANES + ONES_ROWS, s), lambda bi, h, i, sl: (bi, h, 0, 0)),
                vec(HEAD_DIM), vec(HEAD_DIM), vec(HEAD_DIM), vec(HEAD_DIM), vec(LANES),
            ],
            out_specs=pl.BlockSpec((None, tq, LANES), lambda bi, h, i, sl: (bi, i, h)),
            scratch_shapes=[pltpu.VMEM((2, LANES + ONES_ROWS, tq), F32), pltpu.VMEM((2, 2, tk, tq), F32),
                            pltpu.VMEM((2, 2, tk, tq), BF16)],
        ),
        compiler_params=_cparams("parallel", "parallel", "parallel"),
        name="diff_attention",
    )(slopes, proj, proj, v_t, lq1.reshape(1, -1), lk1.reshape(1, -1), lq2.reshape(1, -1),
      lk2.reshape(1, -1), sub_g.reshape(1, -1))


ROUTE_E0, ROUTE_E1, ROUTE_W0, ROUTE_W1, ROUTE_R0, ROUTE_R1 = range(6)
DMA_ISSUE_UNROLL = 8


def _pack_bf16_pair(lo, hi):
    lo_bits = pltpu.bitcast(lo.astype(BF16).astype(F32), U32) >> 16
    hi_bits = pltpu.bitcast(hi.astype(BF16).astype(F32), U32) & jnp.uint32(0xFFFF0000)
    return hi_bits | lo_bits


def _unpack_bf16_pair(u):
    lo = pltpu.bitcast(u << 16, F32).astype(BF16)
    hi = pltpu.bitcast(u & jnp.uint32(0xFFFF0000), F32).astype(BF16)
    return lo, hi


def _router_kernel(x_ref, g_ref, sc_ref, sh_ref, wr_ref, br_ref, hp_ref, rt_ref, cnt_ref, run_ref, *, tm):
    first = (pl.program_id(0) == 0) & (pl.program_id(1) == 0)

    @pl.when(first)
    def _():
        run_ref[...] = jnp.zeros_like(run_ref)

    h = _modulated_norm(x_ref[...], g_ref[...], sc_ref[...], sh_ref[...])
    half = h.shape[-1] // 2
    hp_ref[...] = _pack_bf16_pair(h[:, :half], h[:, half:])

    logits = jnp.dot(h, wr_ref[...], preferred_element_type=F32, precision=HIGHEST) + br_ref[...]
    lane = lax.broadcasted_iota(I32, (tm, LANES), 1)
    lanef = lane.astype(F32)
    ninf = -jnp.inf
    big = float(LANES)

    def first_argmax(vals):
        top = jnp.max(vals, axis=-1, keepdims=True)
        idx = jnp.min(jnp.where(vals == top, lanef, big), axis=-1, keepdims=True)
        return top, idx

    is_group = lane < N_GROUPS
    gl = jnp.where(is_group, logits, ninf)
    gmax, gidx = first_argmax(gl)
    g_w = 1.0 / jnp.sum(jnp.where(is_group, jnp.exp(gl - gmax), 0.0), axis=-1, keepdims=True)
    lane_group = _div_pow2(lane - N_GROUPS, EXPERTS_PER_GROUP).astype(F32)
    in_group = (lane >= N_GROUPS) & (lane < N_GROUPS + N_EXPERTS) & (lane_group == gidx)
    el = jnp.where(in_group, logits, ninf)
    v0, i0 = first_argmax(el)
    v1, i1 = first_argmax(jnp.where(lanef == i0, ninf, el))
    tt = jnp.exp(v1 - v0)
    w0 = g_w / (1.0 + tt)
    w1 = g_w * tt / (1.0 + tt)
    e0 = i0 - N_GROUPS
    e1 = i1 - N_GROUPS

    member = (lanef == e0) | (lanef == e1)
    r = lax.broadcasted_iota(I32, (tm, tm), 0)
    c = lax.broadcasted_iota(I32, (tm, tm), 1)
    earlier = jnp.where(c < r, 1.0, 0.0).astype(BF16)
    before = jnp.dot(earlier, jnp.where(member, 1.0, 0.0).astype(BF16), preferred_element_type=F32)
    before = before + run_ref[...]
    r0 = jnp.sum(jnp.where(lanef == e0, before, 0.0), axis=-1, keepdims=True)
    r1 = jnp.sum(jnp.where(lanef == e1, before, 0.0), axis=-1, keepdims=True)
    run_ref[...] += jnp.sum(jnp.where(member, 1.0, 0.0), axis=0, keepdims=True)
    cnt_ref[...] = run_ref[...]

    slab = jnp.zeros((tm, LANES), F32)
    for colv, val in ((ROUTE_E0, e0), (ROUTE_E1, e1), (ROUTE_W0, w0), (ROUTE_W1, w1), (ROUTE_R0, r0), (ROUTE_R1, r1)):
        slab = jnp.where(lane == colv, val, slab)
    rt_ref[...] = slab


def _router(x, g, sc, sh, w_group, b_group, w_router, b_router):
    b, s, d = x.shape
    tm = _row_tile(s, 512)
    wr = jnp.zeros((d, LANES), F32).at[:, :N_GROUPS].set(w_group).at[:, N_GROUPS:N_GROUPS + N_EXPERTS].set(w_router)
    br = jnp.zeros((1, LANES), F32).at[0, :N_GROUPS].set(b_group).at[0, N_GROUPS:N_GROUPS + N_EXPERTS].set(b_router)
    vec = pl.BlockSpec((None, 1, d), lambda bi, i: (bi, 0, 0))
    return pl.pallas_call(
        functools.partial(_router_kernel, tm=tm),
        out_shape=(jax.ShapeDtypeStruct((b, s, d // 2), U32),
                   jax.ShapeDtypeStruct((b, s, LANES), F32),
                   jax.ShapeDtypeStruct((1, LANES), F32)),
        grid=(b, s // tm),
        in_specs=[
            pl.BlockSpec((None, tm, d), lambda bi, i: (bi, i, 0)),
            pl.BlockSpec((1, d), lambda bi, i: (0, 0)),
            vec, vec,
            pl.BlockSpec((d, LANES), lambda bi, i: (0, 0)),
            pl.BlockSpec((1, LANES), lambda bi, i: (0, 0)),
        ],
        out_specs=(pl.BlockSpec((None, tm, d // 2), lambda bi, i: (bi, i, 0)),
                   pl.BlockSpec((None, tm, LANES), lambda bi, i: (bi, i, 0)),
                   pl.BlockSpec((1, LANES), lambda bi, i: (0, 0))),
        scratch_shapes=[pltpu.VMEM((1, LANES), F32)],
        compiler_params=_cparams("arbitrary", "arbitrary"),
        name="moe_router",
    )(x, g.reshape(1, d), sc, sh, wr, br)


def _zero_kernel(o_ref):
    o_ref[...] = jnp.zeros_like(o_ref)


def _zeros_u32(rows, cols):
    tr = _row_tile(rows, 2048)
    return pl.pallas_call(
        _zero_kernel,
        out_shape=jax.ShapeDtypeStruct((rows, cols), U32),
        grid=(rows // tr,),
        out_specs=pl.BlockSpec((tr, cols), lambda i: (i, 0)),
        compiler_params=_cparams("parallel"),
        name="moe_zero_slots",
    )()


def _dispatch_kernel(dest_ref, h_ref, xs_in_ref, xs_ref, sem, *, chunk):
    del xs_in_ref
    base = pl.program_id(0) * chunk

    def body(r, carry):
        for k in range(2):
            pltpu.make_async_copy(h_ref.at[pl.ds(r, 1), :],
                                  xs_ref.at[pl.ds(dest_ref[2 * (base + r) + k], 1), :], sem).start()
        return carry

    lax.fori_loop(0, chunk, body, 0, unroll=DMA_ISSUE_UNROLL)
    for _ in range(2):
        pltpu.make_async_copy(h_ref, xs_ref.at[pl.ds(0, chunk), :], sem).wait()


def _dispatch(dest, h_packed, n_slots):
    n, w = h_packed.shape
    chunk = _row_tile(n, 256)
    return pl.pallas_call(
        functools.partial(_dispatch_kernel, chunk=chunk),
        out_shape=jax.ShapeDtypeStruct((n_slots, w), U32),
        grid_spec=pltpu.PrefetchScalarGridSpec(
            num_scalar_prefetch=1,
            grid=(n // chunk,),
            in_specs=[pl.BlockSpec((chunk, w), lambda i, d: (i, 0)), pl.BlockSpec(memory_space=pl.ANY)],
            out_specs=pl.BlockSpec(memory_space=pl.ANY),
            scratch_shapes=[pltpu.SemaphoreType.DMA(())],
        ),
        input_output_aliases={2: 0},
        compiler_params=_cparams("arbitrary"),
        name="moe_dispatch",
    )(dest, h_packed, _zeros_u32(n_slots, w))


def _expert_kernel(be_ref, nv_ref, xs_ref, wg_ref, wu_ref, wd_ref, ys_ref, wg_b, wu_b, wd_b):
    blk = pl.program_id(0)
    valid = blk < nv_ref[0]
    changed = (blk == 0) | (be_ref[blk] != be_ref[jnp.maximum(blk - 1, 0)])

    @pl.when(valid & changed)
    def _():
        wg_b[...] = wg_ref[...].astype(BF16)
        wu_b[...] = wu_ref[...].astype(BF16)
        wd_b[...] = wd_ref[...].astype(BF16)

    @pl.when(valid)
    def _():
        x_lo, x_hi = _unpack_bf16_pair(xs_ref[...])
        half = x_lo.shape[-1]

        def proj(w):
            return (jnp.dot(x_lo, w[:half, :], preferred_element_type=F32)
                    + jnp.dot(x_hi, w[half:, :], preferred_element_type=F32))

        gate = proj(wg_b)
        hid = (gate * jax.nn.sigmoid(gate)) * proj(wu_b)
        ys_ref[...] = jnp.dot(hid.astype(BF16), wd_b[...], preferred_element_type=F32)

    @pl.when(jnp.logical_not(valid))
    def _():
        ys_ref[...] = jnp.zeros_like(ys_ref)


def _experts(block_e, n_valid, xs, w_gate, w_up, w_down, layer):
    n_slots, half = xs.shape
    d = 2 * half
    de = w_gate.shape[-1]
    n_blocks = n_slots // MOE_BLOCK
    last = lambda blk, nv: jnp.minimum(blk, nv[0] - 1)
    w_in = pl.BlockSpec((None, None, d, de), lambda blk, be, nv: (layer, be[last(blk, nv)], 0, 0))
    return pl.pallas_call(
        _expert_kernel,
        out_shape=jax.ShapeDtypeStruct((n_slots, d), F32),
        grid_spec=pltpu.PrefetchScalarGridSpec(
            num_scalar_prefetch=2,
            grid=(n_blocks,),
            in_specs=[pl.BlockSpec((MOE_BLOCK, half), lambda blk, be, nv: (last(blk, nv), 0)),
                      w_in, w_in,
                      pl.BlockSpec((None, None, de, d), lambda blk, be, nv: (layer, be[last(blk, nv)], 0, 0))],
            out_specs=pl.BlockSpec((MOE_BLOCK, d), lambda blk, be, nv: (blk, 0)),
            scratch_shapes=[pltpu.VMEM((d, de), BF16), pltpu.VMEM((d, de), BF16), pltpu.VMEM((de, d), BF16)],
        ),
        compiler_params=_cparams("arbitrary"),
        name="moe_experts",
    )(block_e, n_valid, xs, w_gate, w_up, w_down)


def _combine_kernel(dest_ref, x_ref, rt_ref, g_ref, ys_ref, o_ref, buf, sems, *, tm):
    i = pl.program_id(0)
    cur = lax.rem(i, 2)

    def gather(tile, half):
        base = tile * tm

        def body(r, carry):
            for k in range(2):
                pltpu.make_async_copy(ys_ref.at[pl.ds(dest_ref[2 * (base + r) + k], 1), :],
                                      buf.at[half, k, pl.ds(r, 1), :], sems.at[half]).start()
            return carry

        lax.fori_loop(0, tm, body, 0, unroll=DMA_ISSUE_UNROLL)

    @pl.when(i == 0)
    def _():
        gather(0, 0)

    @pl.when(i + 1 < pl.num_programs(0))
    def _():
        gather(i + 1, 1 - cur)

    for k in range(2):
        pltpu.make_async_copy(ys_ref.at[pl.ds(0, tm), :], buf.at[cur, k], sems.at[cur]).wait()
    rt = rt_ref[...]
    w0 = rt[:, ROUTE_W0:ROUTE_W0 + 1]
    w1 = rt[:, ROUTE_W1:ROUTE_W1 + 1]
    o_ref[...] = x_ref[...] + g_ref[...] * (w0 * buf[cur, 0] + w1 * buf[cur, 1])


def _combine(dest, x2, route, gate, ys, tiles_per_batch_of):
    n, d = x2.shape
    tm = _row_tile(n, 256)
    per_batch = tiles_per_batch_of(tm)
    return pl.pallas_call(
        functools.partial(_combine_kernel, tm=tm),
        out_shape=jax.ShapeDtypeStruct((n, d), F32),
        grid_spec=pltpu.PrefetchScalarGridSpec(
            num_scalar_prefetch=1,
            grid=(n // tm,),
            in_specs=[pl.BlockSpec((tm, d), lambda i, de: (i, 0)),
                      pl.BlockSpec((tm, LANES), lambda i, de: (i, 0)),
                      pl.BlockSpec((None, 1, d), lambda i, de: (i // per_batch, 0, 0)),
                      pl.BlockSpec(memory_space=pl.ANY)],
            out_specs=pl.BlockSpec((tm, d), lambda i, de: (i, 0)),
            scratch_shapes=[pltpu.VMEM((2, 2, tm, d), F32), pltpu.SemaphoreType.DMA((2,))],
        ),
        compiler_params=_cparams("arbitrary"),
        name="moe_combine",
    )(dest, x2, route, gate, ys)


def _moe_layer(x, g, sc, sh, gate, w_group, b_group, w_router, b_router, w_gate, w_up, w_down, layer):
    b, s, d = x.shape
    n = b * s
    h_packed, route, counts = _router(x, g, sc, sh, w_group, b_group, w_router, b_router)
    route = route.reshape(n, LANES)
    counts = counts[0, :N_EXPERTS].astype(I32)
    padded = (counts + MOE_BLOCK - 1) // MOE_BLOCK * MOE_BLOCK
    pad_end = jnp.cumsum(padded)
    pad_start = pad_end - padded
    e_id = route[:, ROUTE_E0:ROUTE_E1 + 1].astype(I32)
    rank = route[:, ROUTE_R0:ROUTE_R1 + 1].astype(I32)
    is_e = e_id[:, :, None] == jnp.arange(N_EXPERTS, dtype=I32)
    dest = (jnp.sum(jnp.where(is_e, pad_start, 0), axis=-1) + rank).reshape(-1)
    n_blocks = -(-(n * 2) // MOE_BLOCK) + N_EXPERTS
    block_start = jnp.arange(n_blocks, dtype=I32) * MOE_BLOCK
    block_e = jnp.minimum(jnp.sum((pad_end[None, :] <= block_start[:, None]).astype(I32), axis=1), N_EXPERTS - 1)
    n_valid = (pad_end[-1:] // MOE_BLOCK).astype(I32)
    xs = _dispatch(dest, h_packed.reshape(n, d // 2), n_blocks * MOE_BLOCK)
    ys = _experts(block_e, n_valid, xs, w_gate, w_up, w_down, layer)
    out = _combine(dest, x.reshape(n, d), route, gate, ys, lambda tm: s // tm)
    return out.reshape(b, s, d)


def kernel(x, c, w_ada, b_ada, mix_norm_g, ffn_norm_g, even_w_in, conv_b_glu, conv_dw, conv_dw_b, conv_ln_g, conv_ln_b, even_w_out, odd_w_in, qk_norm_q, qk_norm_k, lambda_q1, lambda_k1, lambda_q2, lambda_k2, diff_sub_g, odd_w_out, moe_w_group, moe_b_group, moe_w_router, moe_b_router, moe_w_gate, moe_w_up, moe_w_down):
    depth = w_ada.shape[0]
    d = x.shape[-1]
    sb_width = d // 2
    conv_width = d // 2
    diff_heads = d // (2 * HEAD_DIM)
    qk_width = diff_heads * 2 * HEAD_DIM
    mod = _ada_mod(c, w_ada, b_ada)
    for layer in range(depth):
        sh1, sc1, g1, sh2, sc2, g2 = [m[:, None, :] for m in jnp.split(mod[layer], 6, axis=-1)]
        j = layer // 2
        if layer % 2 == 0:
            q_gain = jnp.full((1, sb_width), LOG2E * HEAD_DIM ** -0.5, F32)
            proj = _inproj(x, mix_norm_g[layer], sc1, sh1, even_w_in[j], q_gain, head_norm=False)
            a_out = _stick_breaking(proj, sb_width)
            b_out = _conformer_conv(proj, 3 * sb_width, conv_width, conv_b_glu[j], conv_dw[j], conv_dw_b[j],
                                    conv_ln_g[j], conv_ln_b[j])
            x = _outproj_residual([a_out, b_out], [even_w_out[j][:sb_width], even_w_out[j][sb_width:]], x, g1)
        else:
            lam_init = 0.8 - 0.6 * math.exp(-0.3 * layer)
            reps = qk_width // HEAD_DIM
            qk_gain = jnp.concatenate([jnp.tile(qk_norm_q[j] * (LOG2E * HEAD_DIM ** -0.5), reps),
                                       jnp.tile(qk_norm_k[j], reps)])[None, :]
            proj = _inproj(x, mix_norm_g[layer], sc1, sh1, odd_w_in[j], qk_gain, head_norm=True)
            o = _diff_attention(proj, diff_heads, lambda_q1[j], lambda_k1[j], lambda_q2[j], lambda_k2[j],
                                diff_sub_g[j], lam_init)
            x = _outproj_residual([o], [odd_w_out[j]], x, g1)
        x = _moe_layer(x, ffn_norm_g[layer], sc2, sh2, g2, moe_w_group[layer], moe_b_group[layer],
                       moe_w_router[layer], moe_b_router[layer], moe_w_gate, moe_w_up, moe_w_down, layer)
    return x
```

```python
import functools
import math

import jax
import jax.numpy as jnp
from jax import lax
from jax.experimental import pallas as pl
from jax.experimental.pallas import tpu as pltpu

F32 = jnp.float32
BF16 = jnp.bfloat16
U32 = jnp.uint32
I32 = jnp.int32

EPS = 1e-6
LANES = 128
HEAD_DIM = 64
CHUNK = 64
CONV_KERNEL = 31
CONV_HALO = 32
N_GROUPS = 4
EXPERTS_PER_GROUP = 8
N_EXPERTS = N_GROUPS * EXPERTS_PER_GROUP
MOE_BLOCK = 512
ALIBI_MAX_EXP = 8.0
VMEM_LIMIT_BYTES = 56 * 1024 * 1024
HIGHEST = lax.Precision.HIGHEST


def _cparams(*sem):
    return pltpu.CompilerParams(dimension_semantics=sem, vmem_limit_bytes=VMEM_LIMIT_BYTES)


def _div_pow2(v, n):
    shift = n.bit_length() - 1
    assert 1 << shift == n
    return lax.shift_right_arithmetic(v, jnp.int32(shift))


def _row_tile(n, want):
    t = min(n, want)
    assert n % t == 0
    return t


def _ada_kernel(c_ref, w_ref, b_ref, o_ref):
    c = c_ref[...]
    c_act = c * jax.nn.sigmoid(c)
    o_ref[...] = jnp.dot(c_act, w_ref[...], preferred_element_type=F32, precision=HIGHEST) + b_ref[...]


def _ada_mod(c, w_ada, b_ada):
    depth, d, d6 = w_ada.shape
    b = c.shape[0]
    rows = 8
    c_pad = jnp.zeros((rows, d), F32).at[:b].set(c)
    tn = _row_tile(d6, 1536)
    out = pl.pallas_call(
        _ada_kernel,
        out_shape=jax.ShapeDtypeStruct((depth, rows, d6), F32),
        grid=(depth, d6 // tn),
        in_specs=[
            pl.BlockSpec((rows, d), lambda l, j: (0, 0)),
            pl.BlockSpec((None, d, tn), lambda l, j: (l, 0, j)),
            pl.BlockSpec((None, 1, tn), lambda l, j: (l, 0, j)),
        ],
        out_specs=pl.BlockSpec((None, rows, tn), lambda l, j: (l, 0, j)),
        compiler_params=_cparams("parallel", "parallel"),
        name="ada_mod",
    )(c_pad, w_ada, b_ada.reshape(depth, 1, d6))
    return out[:, :b]


def _modulated_norm(x, g, sc, sh):
    ms = jnp.mean(x * x, axis=-1, keepdims=True)
    return (x * lax.rsqrt(ms + EPS) * g) * (1.0 + sc) + sh


def _inproj_kernel(x_ref, g_ref, sc_ref, sh_ref, w_ref, gn_ref, o_ref, *, n_gain, head_norm, tn):
    h = _modulated_norm(x_ref[...], g_ref[...], sc_ref[...], sh_ref[...]).astype(BF16)
    n_out = o_ref.shape[-1]
    if head_norm:
        r = _div_pow2(lax.broadcasted_iota(I32, (tn, tn), 0), HEAD_DIM)
        c = _div_pow2(lax.broadcasted_iota(I32, (tn, tn), 1), HEAD_DIM)
        group_ones = jnp.where(r == c, 1.0, 0.0).astype(BF16)
    for c0 in range(0, n_out, tn):
        y = jnp.dot(h, w_ref[:, c0:c0 + tn], preferred_element_type=F32)
        if c0 < n_gain:
            if head_norm:
                gs = jnp.dot((y * y).astype(BF16), group_ones, preferred_element_type=F32)
                y = y * lax.rsqrt(gs * (1.0 / HEAD_DIM) + EPS)
            y = y * gn_ref[:, c0:c0 + tn]
        o_ref[:, c0:c0 + tn] = y.astype(o_ref.dtype)


def _inproj(x, g, sc, sh, w, gain, head_norm):
    b, s, d = x.shape
    n_out = w.shape[1]
    tm = _row_tile(s, 512)
    tn = 256
    n_gain = gain.shape[1]
    assert n_out % tn == 0 and n_gain % tn == 0
    gn = jnp.zeros((1, n_out), F32).at[:, :n_gain].set(gain)
    vec = pl.BlockSpec((None, 1, d), lambda bi, i: (bi, 0, 0))
    return pl.pallas_call(
        functools.partial(_inproj_kernel, n_gain=n_gain, head_norm=head_norm, tn=tn),
        out_shape=jax.ShapeDtypeStruct((b, s, n_out), BF16),
        grid=(b, s // tm),
        in_specs=[
            pl.BlockSpec((None, tm, d), lambda bi, i: (bi, i, 0)),
            pl.BlockSpec((1, d), lambda bi, i: (0, 0)),
            vec, vec,
            pl.BlockSpec((d, n_out), lambda bi, i: (0, 0)),
            pl.BlockSpec((1, n_out), lambda bi, i: (0, 0)),
        ],
        out_specs=pl.BlockSpec((None, tm, n_out), lambda bi, i: (bi, i, 0)),
        compiler_params=_cparams("parallel", "parallel"),
        name="inproj",
    )(x, g.reshape(1, d), sc, sh, w.astype(BF16), gn)


def _outproj_kernel(*refs, n_in):
    ins, ws = refs[:n_in], refs[n_in:2 * n_in]
    x_ref, g_ref, o_ref = refs[2 * n_in:]
    acc = jnp.dot(ins[0][...], ws[0][...], preferred_element_type=F32)
    for a, w in zip(ins[1:], ws[1:]):
        acc += jnp.dot(a[...], w[...], preferred_element_type=F32)
    o_ref[...] = x_ref[...] + g_ref[...] * acc


def _outproj_residual(parts, weights, x, gate):
    b, s, d = x.shape
    tm = _row_tile(s, 512)
    n_in = len(parts)
    in_specs = [pl.BlockSpec((None, tm, p.shape[-1]), lambda bi, i: (bi, i, 0)) for p in parts]
    in_specs += [pl.BlockSpec(w.shape, lambda bi, i: (0, 0)) for w in weights]
    in_specs += [pl.BlockSpec((None, tm, d), lambda bi, i: (bi, i, 0)),
                 pl.BlockSpec((None, 1, d), lambda bi, i: (bi, 0, 0))]
    return pl.pallas_call(
        functools.partial(_outproj_kernel, n_in=n_in),
        out_shape=jax.ShapeDtypeStruct((b, s, d), F32),
        grid=(b, s // tm),
        in_specs=in_specs,
        out_specs=pl.BlockSpec((None, tm, d), lambda bi, i: (bi, i, 0)),
        compiler_params=_cparams("parallel", "parallel"),
        name="outproj",
    )(*parts, *[w.astype(BF16) for w in weights], x, gate)


LOG2E = 1.4426950408889634
MASKED_LOG2 = -1e30
ONES_ROWS = 16


def _dot_nt(a, b):
    return lax.dot_general(a, b, (((1,), (1,)), ((), ())), preferred_element_type=F32)


def _sb_kernel(q_ref, k_ref, v_ref, o_ref, acc_ref, z_ref, w_ref, *, tq, tk):
    i = pl.program_id(2)
    n_heads = LANES // HEAD_DIM
    n_diag = tq // tk
    lane = lax.broadcasted_iota(I32, (1, LANES), 1)
    r = lax.broadcasted_iota(I32, (tk, tk), 0)
    c = lax.broadcasted_iota(I32, (tk, tk), 1)
    later = jnp.where(r >= c, 1.0, 0.0).astype(BF16)
    row = lax.broadcasted_iota(I32, (tq, tk), 0)
    col = lax.broadcasted_iota(I32, (tq, tk), 1)
    q = q_ref[...]
    qs = [jnp.where((lane >= hh * HEAD_DIM) & (lane < (hh + 1) * HEAD_DIM), q, jnp.zeros_like(q))
          for hh in range(n_heads)]

    def key_tile(ref, j):
        return ref[pl.ds(pl.multiple_of(j * tk, tk), tk), :]

    def weighted_values(slot, j, carry_of_tile):
        for hh in range(n_heads):
            a = jnp.exp2((w_ref[slot, hh] - carry_of_tile[hh]).astype(BF16))
            acc_ref[hh] += jnp.dot(a, key_tile(v_ref, j), preferred_element_type=F32)

    def step(slot, j, j_next, j_prev, state, causal):
        carry, carry_prev = state
        new_carry = []
        weighted_values(1 - slot, j_prev, carry_prev)
        for hh in range(n_heads):
            z = z_ref[slot, hh]
            neg_abs = pltpu.bitcast(pltpu.bitcast(z, U32) | jnp.uint32(0x80000000), F32)
            fail = jnp.maximum(z, 0.0) + jnp.log2(1.0 + jnp.exp2(neg_abs))
            if causal is not None:
                fail = jnp.where(causal, fail, 0.0)
            incl = jnp.dot(fail.astype(BF16), later, preferred_element_type=F32)
            w = z - incl
            if causal is not None:
                w = jnp.where(causal, w, MASKED_LOG2)
            w_ref[slot, hh] = w
            new_carry.append(carry[hh] + incl[:, 0:1])
        for hh in range(n_heads):
            z_ref[1 - slot, hh] = _dot_nt(qs[hh], key_tile(k_ref, j_next))
        return tuple(new_carry), carry

    assert n_diag == 2
    top = i * n_diag + 1
    acc_ref[...] = jnp.zeros_like(acc_ref)
    w_ref[1] = jnp.full(w_ref.shape[1:], MASKED_LOG2, F32)
    for hh in range(n_heads):
        z_ref[0, hh] = _dot_nt(qs[hh], key_tile(k_ref, top))
    zero = tuple(jnp.zeros((tq, 1), F32) for _ in range(n_heads))
    state = (zero, zero)
    state = step(0, top, top - 1, top, state, col + tk < row)
    state = step(1, top - 1, jnp.maximum(top - 2, 0), top, state, col < row)

    def pair(ja, state):
        state = step(0, ja, ja - 1, ja + 1, state, None)
        return step(1, ja - 1, jnp.maximum(ja - 2, 0), ja, state, None)

    state = lax.fori_loop(0, i // 2, lambda jj, st: pair(top - 4 - 4 * jj, pair(top - 2 - 4 * jj, st)), state)
    carry, carry_prev = lax.fori_loop(0, i % 2, lambda jj, st: pair(top - 2 - 4 * (i // 2), st), state)
    weighted_values(1, 0, carry_prev)
    o_ref[...] = jnp.where(lane < HEAD_DIM, acc_ref[0], acc_ref[1]).astype(o_ref.dtype)


def _stick_breaking(proj, width):
    b, s, _ = proj.shape
    tk = _row_tile(s, 256)
    tq = _row_tile(s, 2 * tk)
    nblk = width // LANES
    return pl.pallas_call(
        functools.partial(_sb_kernel, tq=tq, tk=tk),
        out_shape=jax.ShapeDtypeStruct((b, s, width), BF16),
        grid=(b, nblk, s // tq),
        in_specs=[
            pl.BlockSpec((None, tq, LANES), lambda bi, h, i: (bi, i, h)),
            pl.BlockSpec((None, s, LANES), lambda bi, h, i: (bi, 0, nblk + h)),
            pl.BlockSpec((None, s, LANES), lambda bi, h, i: (bi, 0, 2 * nblk + h)),
        ],
        out_specs=pl.BlockSpec((None, tq, LANES), lambda bi, h, i: (bi, i, h)),
        scratch_shapes=[pltpu.VMEM((LANES // HEAD_DIM, tq, LANES), F32),
                        pltpu.VMEM((2, LANES // HEAD_DIM, tq, tk), F32),
                        pltpu.VMEM((2, LANES // HEAD_DIM, tq, tk), F32)],
        compiler_params=_cparams("parallel", "parallel", "parallel"),
        name="stick_breaking",
    )(proj, proj, proj)


def _conv_kernel(a_ref, g_ref, pa_ref, pg_ref, bglu_ref, dw_ref, dwb_ref, lng_ref, lnb_ref, o_ref, hbuf, *, ts, rc):
    i = pl.program_id(1)
    w = a_ref.shape[-1]

    def glu(a, g):
        a = a.astype(F32) + bglu_ref[:, :w]
        g = g.astype(F32) + bglu_ref[:, w:]
        return a * jax.nn.sigmoid(g)

    prev = glu(pa_ref[...], pg_ref[...])
    hbuf[0, 0:CONV_HALO] = jnp.where(i > 0, prev, 0.0)
    hbuf[0, CONV_HALO:CONV_HALO + ts] = glu(a_ref[...], g_ref[...])
    off = CONV_HALO - (CONV_KERNEL - 1)
    n_phase = hbuf.shape[0]
    for p in range(1, n_phase):
        hbuf[p, 0:CONV_HALO + ts - n_phase] = hbuf[0, p:p + CONV_HALO + ts - n_phase]
    for r0 in range(0, ts, rc):
        acc = jnp.zeros((rc, w), F32) + dwb_ref[...]
        for k in range(CONV_KERNEL):
            p = (off + k) % n_phase
            start = r0 + off + k - p
            acc = acc + dw_ref[k:k + 1, :] * hbuf[p, start:start + rc, :]
        mu = jnp.mean(acc, axis=-1, keepdims=True)
        cen = acc - mu
        var = jnp.mean(cen * cen, axis=-1, keepdims=True)
        y = cen * lax.rsqrt(var + EPS) * lng_ref[...] + lnb_ref[...]
        o_ref[r0:r0 + rc, :] = (y * jax.nn.sigmoid(y)).astype(o_ref.dtype)


def _conformer_conv(proj, col0, width, b_glu, dw, dw_b, ln_g, ln_b):
    b, s, _ = proj.shape
    ts = _row_tile(s, 256)
    assert col0 % width == 0 and ts % CONV_HALO == 0
    cb = col0 // width
    hb = ts // CONV_HALO
    cur = lambda off: pl.BlockSpec((None, ts, width), lambda bi, i: (bi, i, cb + off))
    prev = lambda off: pl.BlockSpec((None, CONV_HALO, width),
                                    lambda bi, i: (bi, jnp.maximum(i * hb - 1, 0), cb + off))
    row = lambda n: pl.BlockSpec((1, n), lambda bi, i: (0, 0))
    return pl.pallas_call(
        functools.partial(_conv_kernel, ts=ts, rc=32),
        out_shape=jax.ShapeDtypeStruct((b, s, width), BF16),
        grid=(b, s // ts),
        in_specs=[cur(0), cur(1), prev(0), prev(1), row(2 * width),
                  pl.BlockSpec((CONV_KERNEL, width), lambda bi, i: (0, 0)),
                  row(width), row(width), row(width)],
        out_specs=pl.BlockSpec((None, ts, width), lambda bi, i: (bi, i, 0)),
        scratch_shapes=[pltpu.VMEM((8, CONV_HALO + ts, width), F32)],
        compiler_params=_cparams("parallel", "parallel"),
        name="conformer_conv",
    )(proj, proj, proj, proj, b_glu.reshape(1, -1), dw, dw_b.reshape(1, -1),
      ln_g.reshape(1, -1), ln_b.reshape(1, -1))


def _diff_kernel(slope_ref, q_ref, k_ref, vt_ref, lq1_ref, lk1_ref, lq2_ref, lk2_ref, subg_ref, o_ref,
                 acc_ref, s_ref, p_ref, *, tq, tk, lam_init):
    h = pl.program_id(1)
    i = pl.program_id(2)
    n_diag = tq // tk
    slope = slope_ref[h]
    lane = lax.broadcasted_iota(I32, (1, LANES), 1)
    key = lax.broadcasted_iota(I32, (tk, tq), 0)
    qry = lax.broadcasted_iota(I32, (tk, tq), 1)
    rel = (qry - key).astype(F32)
    bias_past = -slope * rel
    q = q_ref[...]
    zero = jnp.zeros_like(q)
    qs = (jnp.where(lane < HEAD_DIM, q, zero), jnp.where(lane >= HEAD_DIM, q, zero))

    def scores(slot, j, bias):
        kt = k_ref[pl.ds(pl.multiple_of(j * tk, tk), tk), :]
        tile_max = []
        for n in range(2):
            s = _dot_nt(kt, qs[n]) + bias
            s_ref[slot, n] = s
            tile_max.append(jnp.max(s, axis=0, keepdims=True))
        return tuple(tile_max)

    def weighted_values(slot, j, alpha):
        vt = vt_ref[:, pl.ds(pl.multiple_of(j * tk, tk), tk)]
        for n in range(2):
            acc_ref[n] = alpha[n] * acc_ref[n] + jnp.dot(vt, p_ref[slot, n], preferred_element_type=F32)

    def step(slot, j_next, j_prev, state, bias_next, shift):
        stats, alpha_prev, tile_max = state
        weighted_values(1 - slot, j_prev, alpha_prev)
        new_stats, alphas = [], []
        for n in range(2):
            m_new = jnp.maximum(stats[n], tile_max[n] + shift)
            alphas.append(jnp.exp2(stats[n] - m_new))
            p_ref[slot, n] = jnp.exp2((s_ref[slot, n] - (m_new - shift)).astype(BF16))
            new_stats.append(m_new)
        return tuple(new_stats), tuple(alphas), scores(1 - slot, j_next, bias_next)

    def diag_bias(dj):
        allowed = _div_pow2(key + dj * tk, CHUNK) <= _div_pow2(qry, CHUNK)
        return jnp.where(allowed, -slope * jnp.abs(rel - float(dj * tk)), -jnp.inf)

    def shift_of(j):
        return -slope * ((i * n_diag - j) * tk).astype(F32)

    assert n_diag == 2
    top = i * n_diag + 1
    acc_ref[...] = jnp.zeros_like(acc_ref)
    p_ref[1] = jnp.zeros(p_ref.shape[1:], BF16)
    floor = jnp.full((1, tq), MASKED_LOG2, F32)
    one = jnp.ones((1, tq), F32)
    state = ((floor, floor), (one, one), scores(0, top, diag_bias(1)))
    state = step(0, top - 1, top, state, diag_bias(0), 0.0)
    state = step(1, jnp.maximum(top - 2, 0), top, state, bias_past, 0.0)

    def pair(ja, state):
        state = step(0, ja - 1, ja + 1, state, bias_past, shift_of(ja))
        return step(1, jnp.maximum(ja - 2, 0), ja, state, bias_past, shift_of(ja - 1))

    state = lax.fori_loop(0, i // 2, lambda jj, st: pair(top - 4 - 4 * jj, pair(top - 2 - 4 * jj, st)), state)
    _, alpha_last, _ = lax.fori_loop(0, i % 2, lambda jj, st: pair(top - 2 - 4 * (i // 2), st), state)
    weighted_values(1, 0, alpha_last)
    dv = acc_ref.shape[1] - ONES_ROWS
    l1, l2 = acc_ref[0, dv:dv + 1, :], acc_ref[1, dv:dv + 1, :]
    lam = (jnp.exp(jnp.sum(lq1_ref[...] * lk1_ref[...], axis=-1, keepdims=True))
           - jnp.exp(jnp.sum(lq2_ref[...] * lk2_ref[...], axis=-1, keepdims=True)) + lam_init)
    o = acc_ref[0, :dv, :] / l1 - lam * (acc_ref[1, :dv, :] / l2)
    ms = jnp.mean(o * o, axis=0, keepdims=True)
    o = o * lax.rsqrt(ms + EPS) * (1.0 - lam_init)
    o_ref[...] = (o.T * subg_ref[...]).astype(o_ref.dtype)


def _diff_attention(proj, n_heads, lq1, lk1, lq2, lk2, sub_g, lam_init):
    b, s, _ = proj.shape
    tk = _row_tile(s, 256)
    tq = _row_tile(s, 2 * tk)
    assert tk % CHUNK == 0
    v_t = jnp.swapaxes(proj[:, :, 2 * n_heads * LANES:], 1, 2).reshape(b, n_heads, LANES, s)
    v_t = jnp.concatenate([v_t, jnp.ones((b, n_heads, ONES_ROWS, s), BF16)], axis=2)
    slopes = LOG2E * jnp.exp2(-ALIBI_MAX_EXP * jnp.arange(1, n_heads + 1, dtype=F32) / n_heads)
    vec = lambda n: pl.BlockSpec((1, n), lambda bi, h, i, sl: (0, 0))
    return pl.pallas_call(
        functools.partial(_diff_kernel, tq=tq, tk=tk, lam_init=lam_init),
        out_shape=jax.ShapeDtypeStruct((b, s, n_heads * LANES), BF16),
        grid_spec=pltpu.PrefetchScalarGridSpec(
            num_scalar_prefetch=1,
            grid=(b, n_heads, s // tq),
            in_specs=[
                pl.BlockSpec((None, tq, LANES), lambda bi, h, i, sl: (bi, i, h)),
                pl.BlockSpec((None, s, LANES), lambda bi, h, i, sl: (bi, 0, n_heads + h)),
                pl.BlockSpec((None, None, LANES + ONES_ROWS, s), lambda bi, h, i, sl: (bi, h, 0, 0)),
                vec(HEAD_DIM), vec(HEAD_DIM), vec(HEAD_DIM), vec(HEAD_DIM), vec(LANES),
            ],
            out_specs=pl.BlockSpec((None, tq, LANES), lambda bi, h, i, sl: (bi, i, h)),
            scratch_shapes=[pltpu.VMEM((2, LANES + ONES_ROWS, tq), F32), pltpu.VMEM((2, 2, tk, tq), F32),
                            pltpu.VMEM((2, 2, tk, tq), BF16)],
        ),
        compiler_params=_cparams("parallel", "parallel", "parallel"),
        name="diff_attention",
    )(slopes, proj, proj, v_t, lq1.reshape(1, -1), lk1.reshape(1, -1), lq2.reshape(1, -1),
      lk2.reshape(1, -1), sub_g.reshape(1, -1))


ROUTE_E0, ROUTE_E1, ROUTE_W0, ROUTE_W1, ROUTE_R0, ROUTE_R1 = range(6)
DMA_ISSUE_UNROLL = 8


def _pack_bf16_pair(lo, hi):
    lo_bits = pltpu.bitcast(lo.astype(BF16).astype(F32), U32) >> 16
    hi_bits = pltpu.bitcast(hi.astype(BF16).astype(F32), U32) & jnp.uint32(0xFFFF0000)
    return hi_bits | lo_bits


def _unpack_bf16_pair(u):
    lo = pltpu.bitcast(u << 16, F32).astype(BF16)
    hi = pltpu.bitcast(u & jnp.uint32(0xFFFF0000), F32).astype(BF16)
    return lo, hi


def _router_kernel(x_ref, g_ref, sc_ref, sh_ref, wr_ref, br_ref, hp_ref, rt_ref, cnt_ref, run_ref, *, tm):
    first = (pl.program_id(0) == 0) & (pl.program_id(1) == 0)

    @pl.when(first)
    def _():
        run_ref[...] = jnp.zeros_like(run_ref)

    h = _modulated_norm(x_ref[...], g_ref[...], sc_ref[...], sh_ref[...])
    half = h.shape[-1] // 2
    hp_ref[...] = _pack_bf16_pair(h[:, :half], h[:, half:])

    logits = jnp.dot(h, wr_ref[...], preferred_element_type=F32, precision=HIGHEST) + br_ref[...]
    lane = lax.broadcasted_iota(I32, (tm, LANES), 1)
    lanef = lane.astype(F32)
    ninf = -jnp.inf
    big = float(LANES)

    def first_argmax(vals):
        top = jnp.max(vals, axis=-1, keepdims=True)
        idx = jnp.min(jnp.where(vals == top, lanef, big), axis=-1, keepdims=True)
        return top, idx

    is_group = lane < N_GROUPS
    gl = jnp.where(is_group, logits, ninf)
    gmax, gidx = first_argmax(gl)
    g_w = 1.0 / jnp.sum(jnp.where(is_group, jnp.exp(gl - gmax), 0.0), axis=-1, keepdims=True)
    lane_group = _div_pow2(lane - N_GROUPS, EXPERTS_PER_GROUP).astype(F32)
    in_group = (lane >= N_GROUPS) & (lane < N_GROUPS + N_EXPERTS) & (lane_group == gidx)
    el = jnp.where(in_group, logits, ninf)
    v0, i0 = first_argmax(el)
    v1, i1 = first_argmax(jnp.where(lanef == i0, ninf, el))
    tt = jnp.exp(v1 - v0)
    w0 = g_w / (1.0 + tt)
    w1 = g_w * tt / (1.0 + tt)
    e0 = i0 - N_GROUPS
    e1 = i1 - N_GROUPS

    member = (lanef == e0) | (lanef == e1)
    r = lax.broadcasted_iota(I32, (tm, tm), 0)
    c = lax.broadcasted_iota(I32, (tm, tm), 1)
    earlier = jnp.where(c < r, 1.0, 0.0).astype(BF16)
    before = jnp.dot(earlier, jnp.where(member, 1.0, 0.0).astype(BF16), preferred_element_type=F32)
    before = before + run_ref[...]
    r0 = jnp.sum(jnp.where(lanef == e0, before, 0.0), axis=-1, keepdims=True)
    r1 = jnp.sum(jnp.where(lanef == e1, before, 0.0), axis=-1, keepdims=True)
    run_ref[...] += jnp.sum(jnp.where(member, 1.0, 0.0), axis=0, keepdims=True)
    cnt_ref[...] = run_ref[...]

    slab = jnp.zeros((tm, LANES), F32)
    for colv, val in ((ROUTE_E0, e0), (ROUTE_E1, e1), (ROUTE_W0, w0), (ROUTE_W1, w1), (ROUTE_R0, r0), (ROUTE_R1, r1)):
        slab = jnp.where(lane == colv, val, slab)
    rt_ref[...] = slab


def _router(x, g, sc, sh, w_group, b_group, w_router, b_router):
    b, s, d = x.shape
    tm = _row_tile(s, 512)
    wr = jnp.zeros((d, LANES), F32).at[:, :N_GROUPS].set(w_group).at[:, N_GROUPS:N_GROUPS + N_EXPERTS].set(w_router)
    br = jnp.zeros((1, LANES), F32).at[0, :N_GROUPS].set(b_group).at[0, N_GROUPS:N_GROUPS + N_EXPERTS].set(b_router)
    vec = pl.BlockSpec((None, 1, d), lambda bi, i: (bi, 0, 0))
    return pl.pallas_call(
        functools.partial(_router_kernel, tm=tm),
        out_shape=(jax.ShapeDtypeStruct((b, s, d // 2), U32),
                   jax.ShapeDtypeStruct((b, s, LANES), F32),
                   jax.ShapeDtypeStruct((1, LANES), F32)),
        grid=(b, s // tm),
        in_specs=[
            pl.BlockSpec((None, tm, d), lambda bi, i: (bi, i, 0)),
            pl.BlockSpec((1, d), lambda bi, i: (0, 0)),
            vec, vec,
            pl.BlockSpec((d, LANES), lambda bi, i: (0, 0)),
            pl.BlockSpec((1, LANES), lambda bi, i: (0, 0)),
        ],
        out_specs=(pl.BlockSpec((None, tm, d // 2), lambda bi, i: (bi, i, 0)),
                   pl.BlockSpec((None, tm, LANES), lambda bi, i: (bi, i, 0)),
                   pl.BlockSpec((1, LANES), lambda bi, i: (0, 0))),
        scratch_shapes=[pltpu.VMEM((1, LANES), F32)],
        compiler_params=_cparams("arbitrary", "arbitrary"),
        name="moe_router",
    )(x, g.reshape(1, d), sc, sh, wr, br)


def _zero_kernel(o_ref):
    o_ref[...] = jnp.zeros_like(o_ref)


def _zeros_u32(rows, cols):
    tr = _row_tile(rows, 2048)
    return pl.pallas_call(
        _zero_kernel,
        out_shape=jax.ShapeDtypeStruct((rows, cols), U32),
        grid=(rows // tr,),
        out_specs=pl.BlockSpec((tr, cols), lambda i: (i, 0)),
        compiler_params=_cparams("parallel"),
        name="moe_zero_slots",
    )()


def _dispatch_kernel(dest_ref, h_ref, xs_in_ref, xs_ref, sem, *, chunk):
    del xs_in_ref
    base = pl.program_id(0) * chunk

    def body(r, carry):
        for k in range(2):
            pltpu.make_async_copy(h_ref.at[pl.ds(r, 1), :],
                                  xs_ref.at[pl.ds(dest_ref[2 * (base + r) + k], 1), :], sem).start()
        return carry

    lax.fori_loop(0, chunk, body, 0, unroll=DMA_ISSUE_UNROLL)
    for _ in range(2):
        pltpu.make_async_copy(h_ref, xs_ref.at[pl.ds(0, chunk), :], sem).wait()


def _dispatch(dest, h_packed, n_slots):
    n, w = h_packed.shape
    chunk = _row_tile(n, 256)
    return pl.pallas_call(
        functools.partial(_dispatch_kernel, chunk=chunk),
        out_shape=jax.ShapeDtypeStruct((n_slots, w), U32),
        grid_spec=pltpu.PrefetchScalarGridSpec(
            num_scalar_prefetch=1,
            grid=(n // chunk,),
            in_specs=[pl.BlockSpec((chunk, w), lambda i, d: (i, 0)), pl.BlockSpec(memory_space=pl.ANY)],
            out_specs=pl.BlockSpec(memory_space=pl.ANY),
            scratch_shapes=[pltpu.SemaphoreType.DMA(())],
        ),
        input_output_aliases={2: 0},
        compiler_params=_cparams("arbitrary"),
        name="moe_dispatch",
    )(dest, h_packed, _zeros_u32(n_slots, w))


def _expert_kernel(be_ref, nv_ref, xs_ref, wg_ref, wu_ref, wd_ref, ys_ref, wg_b, wu_b, wd_b):
    blk = pl.program_id(0)
    valid = blk < nv_ref[0]
    changed = (blk == 0) | (be_ref[blk] != be_ref[jnp.maximum(blk - 1, 0)])

    @pl.when(valid & changed)
    def _():
        wg_b[...] = wg_ref[...].astype(BF16)
        wu_b[...] = wu_ref[...].astype(BF16)
        wd_b[...] = wd_ref[...].astype(BF16)

    @pl.when(valid)
    def _():
        x_lo, x_hi = _unpack_bf16_pair(xs_ref[...])
        half = x_lo.shape[-1]

        def proj(w):
            return (jnp.dot(x_lo, w[:half, :], preferred_element_type=F32)
                    + jnp.dot(x_hi, w[half:, :], preferred_element_type=F32))

        gate = proj(wg_b)
        hid = (gate * jax.nn.sigmoid(gate)) * proj(wu_b)
        ys_ref[...] = jnp.dot(hid.astype(BF16), wd_b[...], preferred_element_type=F32)

    @pl.when(jnp.logical_not(valid))
    def _():
        ys_ref[...] = jnp.zeros_like(ys_ref)


def _experts(block_e, n_valid, xs, w_gate, w_up, w_down, layer):
    n_slots, half = xs.shape
    d = 2 * half
    de = w_gate.shape[-1]
    n_blocks = n_slots // MOE_BLOCK
    last = lambda blk, nv: jnp.minimum(blk, nv[0] - 1)
    w_in = pl.BlockSpec((None, None, d, de), lambda blk, be, nv: (layer, be[last(blk, nv)], 0, 0))
    return pl.pallas_call(
        _expert_kernel,
        out_shape=jax.ShapeDtypeStruct((n_slots, d), F32),
        grid_spec=pltpu.PrefetchScalarGridSpec(
            num_scalar_prefetch=2,
            grid=(n_blocks,),
            in_specs=[pl.BlockSpec((MOE_BLOCK, half), lambda blk, be, nv: (last(blk, nv), 0)),
                      w_in, w_in,
                      pl.BlockSpec((None, None, de, d), lambda blk, be, nv: (layer, be[last(blk, nv)], 0, 0))],
            out_specs=pl.BlockSpec((MOE_BLOCK, d), lambda blk, be, nv: (blk, 0)),
            scratch_shapes=[pltpu.VMEM((d, de), BF16), pltpu.VMEM((d, de), BF16), pltpu.VMEM((de, d), BF16)],
        ),
        compiler_params=_cparams("arbitrary"),
        name="moe_experts",
    )(block_e, n_valid, xs, w_gate, w_up, w_down)


def _combine_kernel(dest_ref, x_ref, rt_ref, g_ref, ys_ref, o_ref, buf, sems, *, tm):
    i = pl.program_id(0)
    cur = lax.rem(i, 2)

    def gather(tile, half):
        base = tile * tm

        def body(r, carry):
            for k in range(2):
                pltpu.make_async_copy(ys_ref.at[pl.ds(dest_ref[2 * (base + r) + k], 1), :],
                                      buf.at[half, k, pl.ds(r, 1), :], sems.at[half]).start()
            return carry

        lax.fori_loop(0, tm, body, 0, unroll=DMA_ISSUE_UNROLL)

    @pl.when(i == 0)
    def _():
        gather(0, 0)

    @pl.when(i + 1 < pl.num_programs(0))
    def _():
        gather(i + 1, 1 - cur)

    for k in range(2):
        pltpu.make_async_copy(ys_ref.at[pl.ds(0, tm), :], buf.at[cur, k], sems.at[cur]).wait()
    rt = rt_ref[...]
    w0 = rt[:, ROUTE_W0:ROUTE_W0 + 1]
    w1 = rt[:, ROUTE_W1:ROUTE_W1 + 1]
    o_ref[...] = x_ref[...] + g_ref[...] * (w0 * buf[cur, 0] + w1 * buf[cur, 1])


def _combine(dest, x2, route, gate, ys, tiles_per_batch_of):
    n, d = x2.shape
    tm = _row_tile(n, 256)
    per_batch = tiles_per_batch_of(tm)
    return pl.pallas_call(
        functools.partial(_combine_kernel, tm=tm),
        out_shape=jax.ShapeDtypeStruct((n, d), F32),
        grid_spec=pltpu.PrefetchScalarGridSpec(
            num_scalar_prefetch=1,
            grid=(n // tm,),
            in_specs=[pl.BlockSpec((tm, d), lambda i, de: (i, 0)),
                      pl.BlockSpec((tm, LANES), lambda i, de: (i, 0)),
                      pl.BlockSpec((None, 1, d), lambda i, de: (i // per_batch, 0, 0)),
                      pl.BlockSpec(memory_space=pl.ANY)],
            out_specs=pl.BlockSpec((tm, d), lambda i, de: (i, 0)),
            scratch_shapes=[pltpu.VMEM((2, 2, tm, d), F32), pltpu.SemaphoreType.DMA((2,))],
        ),
        compiler_params=_cparams("arbitrary"),
        name="moe_combine",
    )(dest, x2, route, gate, ys)


def _moe_layer(x, g, sc, sh, gate, w_group, b_group, w_router, b_router, w_gate, w_up, w_down, layer):
    b, s, d = x.shape
    n = b * s
    h_packed, route, counts = _router(x, g, sc, sh, w_group, b_group, w_router, b_router)
    route = route.reshape(n, LANES)
    counts = counts[0, :N_EXPERTS].astype(I32)
    padded = (counts + MOE_BLOCK - 1) // MOE_BLOCK * MOE_BLOCK
    pad_end = jnp.cumsum(padded)
    pad_start = pad_end - padded
    e_id = route[:, ROUTE_E0:ROUTE_E1 + 1].astype(I32)
    rank = route[:, ROUTE_R0:ROUTE_R1 + 1].astype(I32)
    is_e = e_id[:, :, None] == jnp.arange(N_EXPERTS, dtype=I32)
    dest = (jnp.sum(jnp.where(is_e, pad_start, 0), axis=-1) + rank).reshape(-1)
    n_blocks = -(-(n * 2) // MOE_BLOCK) + N_EXPERTS
    block_start = jnp.arange(n_blocks, dtype=I32) * MOE_BLOCK
    block_e = jnp.minimum(jnp.sum((pad_end[None, :] <= block_start[:, None]).astype(I32), axis=1), N_EXPERTS - 1)
    n_valid = (pad_end[-1:] // MOE_BLOCK).astype(I32)
    xs = _dispatch(dest, h_packed.reshape(n, d // 2), n_blocks * MOE_BLOCK)
    ys = _experts(block_e, n_valid, xs, w_gate, w_up, w_down, layer)
    out = _combine(dest, x.reshape(n, d), route, gate, ys, lambda tm: s // tm)
    return out.reshape(b, s, d)


def kernel(x, c, w_ada, b_ada, mix_norm_g, ffn_norm_g, even_w_in, conv_b_glu, conv_dw, conv_dw_b, conv_ln_g, conv_ln_b, even_w_out, odd_w_in, qk_norm_q, qk_norm_k, lambda_q1, lambda_k1, lambda_q2, lambda_k2, diff_sub_g, odd_w_out, moe_w_group, moe_b_group, moe_w_router, moe_b_router, moe_w_gate, moe_w_up, moe_w_down):
    depth = w_ada.shape[0]
    d = x.shape[-1]
    sb_width = d // 2
    conv_width = d // 2
    diff_heads = d // (2 * HEAD_DIM)
    qk_width = diff_heads * 2 * HEAD_DIM
    mod = _ada_mod(c, w_ada, b_ada)
    for layer in range(depth):
        sh1, sc1, g1, sh2, sc2, g2 = [m[:, None, :] for m in jnp.split(mod[layer], 6, axis=-1)]
        j = layer // 2
        if layer % 2 == 0:
            q_gain = jnp.full((1, sb_width), LOG2E * HEAD_DIM ** -0.5, F32)
            proj = _inproj(x, mix_norm_g[layer], sc1, sh1, even_w_in[j], q_gain, head_norm=False)
            a_out = _stick_breaking(proj, sb_width)
            b_out = _conformer_conv(proj, 3 * sb_width, conv_width, conv_b_glu[j], conv_dw[j], conv_dw_b[j],
                                    conv_ln_g[j], conv_ln_b[j])
            x = _outproj_residual([a_out, b_out], [even_w_out[j][:sb_width], even_w_out[j][sb_width:]], x, g1)
        else:
            lam_init = 0.8 - 0.6 * math.exp(-0.3 * layer)
            reps = qk_width // HEAD_DIM
            qk_gain = jnp.concatenate([jnp.tile(qk_norm_q[j] * (LOG2E * HEAD_DIM ** -0.5), reps),
                                       jnp.tile(qk_norm_k[j], reps)])[None, :]
            proj = _inproj(x, mix_norm_g[layer], sc1, sh1, odd_w_in[j], qk_gain, head_norm=True)
            o = _diff_attention(proj, diff_heads, lambda_q1[j], lambda_k1[j], lambda_q2[j], lambda_k2[j],
                                diff_sub_g[j], lam_init)
            x = _outproj_residual([o], [odd_w_out[j]], x, g1)
        x = _moe_layer(x, ffn_norm_g[layer], sc2, sh2, g2, moe_w_group[layer], moe_b_group[layer],
                       moe_w_router[layer], moe_b_router[layer], moe_w_gate, moe_w_up, moe_w_down, layer)
    return x
```

```python
import functools
import math

import jax
import jax.numpy as jnp
from jax import lax
from jax.experimental import pallas as pl
from jax.experimental.pallas import tpu as pltpu

F32 = jnp.float32
BF16 = jnp.bfloat16
U32 = jnp.uint32
I32 = jnp.int32

EPS = 1e-6
LANES = 128
HEAD_DIM = 64
CHUNK = 64
CONV_KERNEL = 31
CONV_HALO = 32
N_GROUPS = 4
EXPERTS_PER_GROUP = 8
N_EXPERTS = N_GROUPS * EXPERTS_PER_GROUP
MOE_BLOCK = 512
ALIBI_MAX_EXP = 8.0
VMEM_LIMIT_BYTES = 56 * 1024 * 1024
HIGHEST = lax.Precision.HIGHEST


def _cparams(*sem):
    return pltpu.CompilerParams(dimension_semantics=sem, vmem_limit_bytes=VMEM_LIMIT_BYTES)


def _div_pow2(v, n):
    shift = n.bit_length() - 1
    assert 1 << shift == n
    return lax.shift_right_arithmetic(v, jnp.int32(shift))


def _row_tile(n, want):
    t = min(n, want)
    assert n % t == 0
    return t


def _ada_kernel(c_ref, w_ref, b_ref, o_ref):
    c = c_ref[...]
    c_act = c * jax.nn.sigmoid(c)
    o_ref[...] = jnp.dot(c_act, w_ref[...], preferred_element_type=F32, precision=HIGHEST) + b_ref[...]


def _ada_mod(c, w_ada, b_ada):
    depth, d, d6 = w_ada.shape
    b = c.shape[0]
    rows = 8
    c_pad = jnp.zeros((rows, d), F32).at[:b].set(c)
    tn = _row_tile(d6, 1536)
    out = pl.pallas_call(
        _ada_kernel,
        out_shape=jax.ShapeDtypeStruct((depth, rows, d6), F32),
        grid=(depth, d6 // tn),
        in_specs=[
            pl.BlockSpec((rows, d), lambda l, j: (0, 0)),
            pl.BlockSpec((None, d, tn), lambda l, j: (l, 0, j)),
            pl.BlockSpec((None, 1, tn), lambda l, j: (l, 0, j)),
        ],
        out_specs=pl.BlockSpec((None, rows, tn), lambda l, j: (l, 0, j)),
        compiler_params=_cparams("parallel", "parallel"),
        name="ada_mod",
    )(c_pad, w_ada, b_ada.reshape(depth, 1, d6))
    return out[:, :b]


def _modulated_norm(x, g, sc, sh):
    ms = jnp.mean(x * x, axis=-1, keepdims=True)
    return (x * lax.rsqrt(ms + EPS) * g) * (1.0 + sc) + sh


def _inproj_kernel(x_ref, g_ref, sc_ref, sh_ref, w_ref, gn_ref, o_ref, *scratch, n_gain, head_norm, tn):
    h = _modulated_norm(x_ref[...], g_ref[...], sc_ref[...], sh_ref[...]).astype(BF16)
    n_out = o_ref.shape[-1]
    for c0 in range(0, n_out, tn):
        y = jnp.dot(h, w_ref[:, c0:c0 + tn], preferred_element_type=F32)
        if c0 < n_gain and head_norm:
            scratch[0][:, c0:c0 + tn] = y
        elif c0 < n_gain:
            o_ref[:, c0:c0 + tn] = (y * gn_ref[:, c0:c0 + tn]).astype(o_ref.dtype)
        else:
            o_ref[:, c0:c0 + tn] = y.astype(o_ref.dtype)
    if head_norm:
        r = _div_pow2(lax.broadcasted_iota(I32, (tn, tn), 0), HEAD_DIM)
        c = _div_pow2(lax.broadcasted_iota(I32, (tn, tn), 1), HEAD_DIM)
        group_ones = jnp.where(r == c, 1.0, 0.0).astype(BF16)
        for c0 in range(0, n_gain, tn):
            y = scratch[0][:, c0:c0 + tn]
            gs = jnp.dot((y * y).astype(BF16), group_ones, preferred_element_type=F32)
            y = y * lax.rsqrt(gs * (1.0 / HEAD_DIM) + EPS) * gn_ref[:, c0:c0 + tn]
            o_ref[:, c0:c0 + tn] = y.astype(o_ref.dtype)


def _inproj(x, g, sc, sh, w, gain, head_norm):
    b, s, d = x.shape
    n_out = w.shape[1]
    tm = _row_tile(s, 512)
    tn = 256
    n_gain = gain.shape[1]
    assert n_out % tn == 0 and n_gain % tn == 0
    gn = jnp.zeros((1, n_out), F32).at[:, :n_gain].set(gain)
    vec = pl.BlockSpec((None, 1, d), lambda bi, i: (bi, 0, 0))
    return pl.pallas_call(
        functools.partial(_inproj_kernel, n_gain=n_gain, head_norm=head_norm, tn=tn),
        out_shape=jax.ShapeDtypeStruct((b, s, n_out), BF16),
        grid=(b, s // tm),
        in_specs=[
            pl.BlockSpec((None, tm, d), lambda bi, i: (bi, i, 0)),
            pl.BlockSpec((1, d), lambda bi, i: (0, 0)),
            vec, vec,
            pl.BlockSpec((d, n_out), lambda bi, i: (0, 0)),
            pl.BlockSpec((1, n_out), lambda bi, i: (0, 0)),
        ],
        out_specs=pl.BlockSpec((None, tm, n_out), lambda bi, i: (bi, i, 0)),
        scratch_shapes=[pltpu.VMEM((tm, n_gain), F32)] if head_norm else [],
        compiler_params=_cparams("parallel", "parallel"),
        name="inproj",
    )(x, g.reshape(1, d), sc, sh, w.astype(BF16), gn)


def _outproj_kernel(*refs, n_in):
    ins, ws = refs[:n_in], refs[n_in:2 * n_in]
    x_ref, g_ref, o_ref = refs[2 * n_in:]
    acc = jnp.dot(ins[0][...], ws[0][...], preferred_element_type=F32)
    for a, w in zip(ins[1:], ws[1:]):
        acc += jnp.dot(a[...], w[...], preferred_element_type=F32)
    o_ref[...] = x_ref[...] + g_ref[...] * acc


def _outproj_residual(parts, weights, x, gate):
    b, s, d = x.shape
    tm = _row_tile(s, 512)
    n_in = len(parts)
    in_specs = [pl.BlockSpec((None, tm, p.shape[-1]), lambda bi, i: (bi, i, 0)) for p in parts]
    in_specs += [pl.BlockSpec(w.shape, lambda bi, i: (0, 0)) for w in weights]
    in_specs += [pl.BlockSpec((None, tm, d), lambda bi, i: (bi, i, 0)),
                 pl.BlockSpec((None, 1, d), lambda bi, i: (bi, 0, 0))]
    return pl.pallas_call(
        functools.partial(_outproj_kernel, n_in=n_in),
        out_shape=jax.ShapeDtypeStruct((b, s, d), F32),
        grid=(b, s // tm),
        in_specs=in_specs,
        out_specs=pl.BlockSpec((None, tm, d), lambda bi, i: (bi, i, 0)),
        compiler_params=_cparams("parallel", "parallel"),
        name="outproj",
    )(*parts, *[w.astype(BF16) for w in weights], x, gate)


LOG2E = 1.4426950408889634
MASKED_LOG2 = -1e30
ONES_ROWS = 16
BIAS_PAST, BIAS_DIAG0, BIAS_DIAG1 = range(3)


def _dot_nt(a, b):
    return lax.dot_general(a, b, (((1,), (1,)), ((), ())), preferred_element_type=F32)


def _sb_kernel(q_ref, k_ref, v_ref, o_ref, acc_ref, z_ref, w_ref, *, tq, tk):
    i = pl.program_id(2)
    n_heads = LANES // HEAD_DIM
    n_diag = tq // tk
    lane = lax.broadcasted_iota(I32, (1, LANES), 1)
    r = lax.broadcasted_iota(I32, (tk, tk), 0)
    c = lax.broadcasted_iota(I32, (tk, tk), 1)
    later = jnp.where(r >= c, 1.0, 0.0).astype(BF16)
    row = lax.broadcasted_iota(I32, (tq, tk), 0)
    col = lax.broadcasted_iota(I32, (tq, tk), 1)
    q = q_ref[...]
    qs = [jnp.where((lane >= hh * HEAD_DIM) & (lane < (hh + 1) * HEAD_DIM), q, jnp.zeros_like(q))
          for hh in range(n_heads)]

    def key_tile(ref, j):
        return ref[pl.ds(pl.multiple_of(j * tk, tk), tk), :]

    def weighted_values(slot, j, carry_of_tile):
        for hh in range(n_heads):
            a = jnp.exp2((w_ref[slot, hh] - carry_of_tile[hh]).astype(BF16))
            acc_ref[hh] += jnp.dot(a, key_tile(v_ref, j), preferred_element_type=F32)

    def step(slot, j, j_next, j_prev, state, causal):
        carry, carry_prev = state
        new_carry = []
        weighted_values(1 - slot, j_prev, carry_prev)
        for hh in range(n_heads):
            z = z_ref[slot, hh]
            neg_abs = pltpu.bitcast(pltpu.bitcast(z, U32) | jnp.uint32(0x80000000), F32)
            fail = jnp.maximum(z, 0.0) + jnp.log2(1.0 + jnp.exp2(neg_abs))
            if causal is not None:
                fail = jnp.where(causal, fail, 0.0)
            incl = jnp.dot(fail.astype(BF16), later, preferred_element_type=F32)
            w = z - incl
            if causal is not None:
                w = jnp.where(causal, w, MASKED_LOG2)
            w_ref[slot, hh] = w
            new_carry.append(carry[hh] + incl[:, 0:1])
        for hh in range(n_heads):
            z_ref[1 - slot, hh] = _dot_nt(qs[hh], key_tile(k_ref, j_next))
        return tuple(new_carry), carry

    assert n_diag == 2
    top = i * n_diag + 1
    acc_ref[...] = jnp.zeros_like(acc_ref)
    w_ref[1] = jnp.full(w_ref.shape[1:], MASKED_LOG2, F32)
    for hh in range(n_heads):
        z_ref[0, hh] = _dot_nt(qs[hh], key_tile(k_ref, top))
    zero = tuple(jnp.zeros((tq, 1), F32) for _ in range(n_heads))
    state = (zero, zero)
    state = step(0, top, top - 1, top, state, col + tk < row)
    state = step(1, top - 1, jnp.maximum(top - 2, 0), top, state, col < row)

    def pair(ja, state):
        state = step(0, ja, ja - 1, ja + 1, state, None)
        return step(1, ja - 1, jnp.maximum(ja - 2, 0), ja, state, None)

    state = lax.fori_loop(0, i // 2, lambda jj, st: pair(top - 4 - 4 * jj, pair(top - 2 - 4 * jj, st)), state)
    carry, carry_prev = lax.fori_loop(0, i % 2, lambda jj, st: pair(top - 2 - 4 * (i // 2), st), state)
    weighted_values(1, 0, carry_prev)
    o_ref[...] = jnp.where(lane < HEAD_DIM, acc_ref[0], acc_ref[1]).astype(o_ref.dtype)


def _stick_breaking(proj, width):
    b, s, _ = proj.shape
    tk = _row_tile(s, 256)
    tq = _row_tile(s, 2 * tk)
    nblk = width // LANES
    return pl.pallas_call(
        functools.partial(_sb_kernel, tq=tq, tk=tk),
        out_shape=jax.ShapeDtypeStruct((b, s, width), BF16),
        grid=(b, nblk, s // tq),
        in_specs=[
            pl.BlockSpec((None, tq, LANES), lambda bi, h, i: (bi, i, h)),
            pl.BlockSpec((None, s, LANES), lambda bi, h, i: (bi, 0, nblk + h)),
            pl.BlockSpec((None, s, LANES), lambda bi, h, i: (bi, 0, 2 * nblk + h)),
        ],
        out_specs=pl.BlockSpec((None, tq, LANES), lambda bi, h, i: (bi, i, h)),
        scratch_shapes=[pltpu.VMEM((LANES // HEAD_DIM, tq, LANES), F32),
                        pltpu.VMEM((2, LANES // HEAD_DIM, tq, tk), F32),
                        pltpu.VMEM((2, LANES // HEAD_DIM, tq, tk), F32)],
        compiler_params=_cparams("parallel", "parallel", "parallel"),
        name="stick_breaking",
    )(proj, proj, proj)


def _conv_kernel(a_ref, g_ref, pa_ref, pg_ref, bglu_ref, dw_ref, dwb_ref, lng_ref, lnb_ref, o_ref, hbuf, *, ts, rc):
    i = pl.program_id(1)
    w = a_ref.shape[-1]

    def glu(a, g):
        a = a.astype(F32) + bglu_ref[:, :w]
        g = g.astype(F32) + bglu_ref[:, w:]
        return a * jax.nn.sigmoid(g)

    prev = glu(pa_ref[...], pg_ref[...])
    hbuf[0, 0:CONV_HALO] = jnp.where(i > 0, prev, 0.0)
    hbuf[0, CONV_HALO:CONV_HALO + ts] = glu(a_ref[...], g_ref[...])
    off = CONV_HALO - (CONV_KERNEL - 1)
    n_phase = hbuf.shape[0]
    for p in range(1, n_phase):
        hbuf[p, 0:CONV_HALO + ts - n_phase] = hbuf[0, p:p + CONV_HALO + ts - n_phase]
    for r0 in range(0, ts, rc):
        acc = jnp.zeros((rc, w), F32) + dwb_ref[...]
        for k in range(CONV_KERNEL):
            p = (off + k) % n_phase
            start = r0 + off + k - p
            acc = acc + dw_ref[k:k + 1, :] * hbuf[p, start:start + rc, :]
        mu = jnp.mean(acc, axis=-1, keepdims=True)
        cen = acc - mu
        var = jnp.mean(cen * cen, axis=-1, keepdims=True)
        y = cen * lax.rsqrt(var + EPS) * lng_ref[...] + lnb_ref[...]
        o_ref[r0:r0 + rc, :] = (y * jax.nn.sigmoid(y)).astype(o_ref.dtype)


def _conformer_conv(proj, col0, width, b_glu, dw, dw_b, ln_g, ln_b):
    b, s, _ = proj.shape
    ts = _row_tile(s, 256)
    assert col0 % width == 0 and ts % CONV_HALO == 0
    cb = col0 // width
    hb = ts // CONV_HALO
    cur = lambda off: pl.BlockSpec((None, ts, width), lambda bi, i: (bi, i, cb + off))
    prev = lambda off: pl.BlockSpec((None, CONV_HALO, width),
                                    lambda bi, i: (bi, jnp.maximum(i * hb - 1, 0), cb + off))
    row = lambda n: pl.BlockSpec((1, n), lambda bi, i: (0, 0))
    return pl.pallas_call(
        functools.partial(_conv_kernel, ts=ts, rc=32),
        out_shape=jax.ShapeDtypeStruct((b, s, width), BF16),
        grid=(b, s // ts),
        in_specs=[cur(0), cur(1), prev(0), prev(1), row(2 * width),
                  pl.BlockSpec((CONV_KERNEL, width), lambda bi, i: (0, 0)),
                  row(width), row(width), row(width)],
        out_specs=pl.BlockSpec((None, ts, width), lambda bi, i: (bi, i, 0)),
        scratch_shapes=[pltpu.VMEM((8, CONV_HALO + ts, width), F32)],
        compiler_params=_cparams("parallel", "parallel"),
        name="conformer_conv",
    )(proj, proj, proj, proj, b_glu.reshape(1, -1), dw, dw_b.reshape(1, -1),
      ln_g.reshape(1, -1), ln_b.reshape(1, -1))


def _diff_kernel(slope_ref, q_ref, k_ref, vt_ref, bias_ref, lq1_ref, lk1_ref, lq2_ref, lk2_ref, subg_ref, o_ref,
                 acc_ref, s_ref, p_ref, *, tq, tk, lam_init):
    h = pl.program_id(1)
    i = pl.program_id(2)
    n_diag = tq // tk
    slope = slope_ref[h]
    lane = lax.broadcasted_iota(I32, (1, LANES), 1)
    q = q_ref[...]
    zero = jnp.zeros_like(q)
    qs = (jnp.where(lane < HEAD_DIM, q, zero), jnp.where(lane >= HEAD_DIM, q, zero))

    def scores(slot, j, bias):
        kt = k_ref[pl.ds(pl.multiple_of(j * tk, tk), tk), :]
        tile_max = []
        for n in range(2):
            s = _dot_nt(kt, qs[n]) + bias
            s_ref[slot, n] = s
            tile_max.append(jnp.max(s, axis=0, keepdims=True))
        return tuple(tile_max)

    def weighted_values(slot, j, alpha):
        vt = vt_ref[:, pl.ds(pl.multiple_of(j * tk, tk), tk)]
        for n in range(2):
            acc_ref[n] = alpha[n] * acc_ref[n] + jnp.dot(vt, p_ref[slot, n], preferred_element_type=F32)

    def step(slot, j_next, j_prev, state, bias_next, shift):
        stats, alpha_prev, tile_max = state
        weighted_values(1 - slot, j_prev, alpha_prev)
        new_stats, alphas = [], []
        for n in range(2):
            m_new = jnp.maximum(stats[n], tile_max[n] + shift)
            alphas.append(jnp.exp2(stats[n] - m_new))
            p_ref[slot, n] = jnp.exp2((s_ref[slot, n] - (m_new - shift)).astype(BF16))
            new_stats.append(m_new)
        return tuple(new_stats), tuple(alphas), scores(1 - slot, j_next, bias_next)

    def shift_of(j):
        return -slope * ((i * n_diag - j) * tk).astype(F32)

    assert n_diag == 2
    top = i * n_diag + 1
    acc_ref[...] = jnp.zeros_like(acc_ref)
    p_ref[1] = jnp.zeros(p_ref.shape[1:], BF16)
    floor = jnp.full((1, tq), MASKED_LOG2, F32)
    one = jnp.ones((1, tq), F32)
    state = ((floor, floor), (one, one), scores(0, top, bias_ref[BIAS_DIAG1]))
    state = step(0, top - 1, top, state, bias_ref[BIAS_DIAG0], 0.0)
    state = step(1, jnp.maximum(top - 2, 0), top, state, bias_ref[BIAS_PAST], 0.0)

    def pair(ja, state):
        state = step(0, ja - 1, ja + 1, state, bias_ref[BIAS_PAST], shift_of(ja))
        return step(1, jnp.maximum(ja - 2, 0), ja, state, bias_ref[BIAS_PAST], shift_of(ja - 1))

    state = lax.fori_loop(0, i // 2, lambda jj, st: pair(top - 4 - 4 * jj, pair(top - 2 - 4 * jj, st)), state)
    _, alpha_last, _ = lax.fori_loop(0, i % 2, lambda jj, st: pair(top - 2 - 4 * (i // 2), st), state)
    weighted_values(1, 0, alpha_last)
    dv = acc_ref.shape[1] - ONES_ROWS
    l1, l2 = acc_ref[0, dv:dv + 1, :], acc_ref[1, dv:dv + 1, :]
    lam = (jnp.exp(jnp.sum(lq1_ref[...] * lk1_ref[...], axis=-1, keepdims=True))
           - jnp.exp(jnp.sum(lq2_ref[...] * lk2_ref[...], axis=-1, keepdims=True)) + lam_init)
    o = acc_ref[0, :dv, :] / l1 - lam * (acc_ref[1, :dv, :] / l2)
    ms = jnp.mean(o * o, axis=0, keepdims=True)
    o = o * lax.rsqrt(ms + EPS) * (1.0 - lam_init)
    o_ref[...] = (o.T * subg_ref[...]).astype(o_ref.dtype)


def _diff_attention(proj, n_heads, lq1, lk1, lq2, lk2, sub_g, lam_init):
    b, s, _ = proj.shape
    tk = _row_tile(s, 256)
    tq = _row_tile(s, 2 * tk)
    assert tk % CHUNK == 0
    v_t = jnp.swapaxes(proj[:, :, 2 * n_heads * LANES:], 1, 2).reshape(b, n_heads, LANES, s)
    v_t = jnp.concatenate([v_t, jnp.ones((b, n_heads, ONES_ROWS, s), BF16)], axis=2)
    slopes = LOG2E * jnp.exp2(-ALIBI_MAX_EXP * jnp.arange(1, n_heads + 1, dtype=F32) / n_heads)
    key = jnp.arange(tk, dtype=I32)[:, None]
    qry = jnp.arange(tq, dtype=I32)[None, :]
    rel = (qry - key).astype(F32)

    def diag(dj):
        allowed = (key + dj * tk) // CHUNK <= qry // CHUNK
        return jnp.where(allowed, -jnp.abs(rel - float(dj * tk)), -jnp.inf)

    bias = slopes[:, None, None, None] * jnp.stack([-rel, diag(0), diag(1)])
    vec = lambda n: pl.BlockSpec((1, n), lambda bi, h, i, sl: (0, 0))
    return pl.pallas_call(
        functools.partial(_diff_kernel, tq=tq, tk=tk, lam_init=lam_init),
        out_shape=jax.ShapeDtypeStruct((b, s, n_heads * LANES), BF16),
        grid_spec=pltpu.PrefetchScalarGridSpec(
            num_scalar_prefetch=1,
            grid=(b, n_heads, s // tq),
            in_specs=[
                pl.BlockSpec((None, tq, LANES), lambda bi, h, i, sl: (bi, i, h)),
                pl.BlockSpec((None, s, LANES), lambda bi, h, i, sl: (bi, 0, n_heads + h)),
                pl.BlockSpec((None, None, LANES + ONES_ROWS, s), lambda bi, h, i, sl: (bi, h, 0, 0)),
                pl.BlockSpec((None, 3, tk, tq), lambda bi, h, i, sl: (h, 0, 0, 0)),
                vec(HEAD_DIM), vec(HEAD_DIM), vec(HEAD_DIM), vec(HEAD_DIM), vec(LANES),
            ],
            out_specs=pl.BlockSpec((None, tq, LANES), lambda bi, h, i, sl: (bi, i, h)),
            scratch_shapes=[pltpu.VMEM((2, LANES + ONES_ROWS, tq), F32), pltpu.VMEM((2, 2, tk, tq), F32),
                            pltpu.VMEM((2, 2, tk, tq), BF16)],
        ),
        compiler_params=_cparams("parallel", "parallel", "parallel"),
        name="diff_attention",
    )(slopes, proj, proj, v_t, bias, lq1.reshape(1, -1), lk1.reshape(1, -1), lq2.reshape(1, -1),
      lk2.reshape(1, -1), sub_g.reshape(1, -1))


ROUTE_E0, ROUTE_E1, ROUTE_W0, ROUTE_W1, ROUTE_R0, ROUTE_R1 = range(6)
DMA_ISSUE_UNROLL = 8


def _pack_bf16_pair(lo, hi):
    lo_bits = pltpu.bitcast(lo.astype(BF16).astype(F32), U32) >> 16
    hi_bits = pltpu.bitcast(hi.astype(BF16).astype(F32), U32) & jnp.uint32(0xFFFF0000)
    return hi_bits | lo_bits


def _unpack_bf16_pair(u):
    lo = pltpu.bitcast(u << 16, F32).astype(BF16)
    hi = pltpu.bitcast(u & jnp.uint32(0xFFFF0000), F32).astype(BF16)
    return lo, hi


def _router_kernel(x_ref, g_ref, sc_ref, sh_ref, wr_ref, br_ref, hp_ref, rt_ref, cnt_ref, run_ref, *, tm):
    first = (pl.program_id(0) == 0) & (pl.program_id(1) == 0)

    @pl.when(first)
    def _():
        run_ref[...] = jnp.zeros_like(run_ref)

    h = _modulated_norm(x_ref[...], g_ref[...], sc_ref[...], sh_ref[...])
    half = h.shape[-1] // 2
    hp_ref[...] = _pack_bf16_pair(h[:, :half], h[:, half:])

    logits = jnp.dot(h, wr_ref[...], preferred_element_type=F32, precision=HIGHEST) + br_ref[...]
    lane = lax.broadcasted_iota(I32, (tm, LANES), 1)
    lanef = lane.astype(F32)
    ninf = -jnp.inf
    big = float(LANES)

    def first_argmax(vals):
        top = jnp.max(vals, axis=-1, keepdims=True)
        idx = jnp.min(jnp.where(vals == top, lanef, big), axis=-1, keepdims=True)
        return top, idx

    is_group = lane < N_GROUPS
    gl = jnp.where(is_group, logits, ninf)
    gmax, gidx = first_argmax(gl)
    g_w = 1.0 / jnp.sum(jnp.where(is_group, jnp.exp(gl - gmax), 0.0), axis=-1, keepdims=True)
    lane_group = _div_pow2(lane - N_GROUPS, EXPERTS_PER_GROUP).astype(F32)
    in_group = (lane >= N_GROUPS) & (lane < N_GROUPS + N_EXPERTS) & (lane_group == gidx)
    el = jnp.where(in_group, logits, ninf)
    v0, i0 = first_argmax(el)
    v1, i1 = first_argmax(jnp.where(lanef == i0, ninf, el))
    tt = jnp.exp(v1 - v0)
    w0 = g_w / (1.0 + tt)
    w1 = g_w * tt / (1.0 + tt)
    e0 = i0 - N_GROUPS
    e1 = i1 - N_GROUPS

    member = (lanef == e0) | (lanef == e1)
    r = lax.broadcasted_iota(I32, (tm, tm), 0)
    c = lax.broadcasted_iota(I32, (tm, tm), 1)
    earlier = jnp.where(c < r, 1.0, 0.0).astype(BF16)
    before = jnp.dot(earlier, jnp.where(member, 1.0, 0.0).astype(BF16), preferred_element_type=F32)
    before = before + run_ref[...]
    r0 = jnp.sum(jnp.where(lanef == e0, before, 0.0), axis=-1, keepdims=True)
    r1 = jnp.sum(jnp.where(lanef == e1, before, 0.0), axis=-1, keepdims=True)
    run_ref[...] += jnp.sum(jnp.where(member, 1.0, 0.0), axis=0, keepdims=True)
    cnt_ref[...] = run_ref[...]

    slab = jnp.zeros((tm, LANES), F32)
    for colv, val in ((ROUTE_E0, e0), (ROUTE_E1, e1), (ROUTE_W0, w0), (ROUTE_W1, w1), (ROUTE_R0, r0), (ROUTE_R1, r1)):
        slab = jnp.where(lane == colv, val, slab)
    rt_ref[...] = slab


def _router(x, g, sc, sh, w_group, b_group, w_router, b_router):
    b, s, d = x.shape
    tm = _row_tile(s, 512)
    wr = jnp.zeros((d, LANES), F32).at[:, :N_GROUPS].set(w_group).at[:, N_GROUPS:N_GROUPS + N_EXPERTS].set(w_router)
    br = jnp.zeros((1, LANES), F32).at[0, :N_GROUPS].set(b_group).at[0, N_GROUPS:N_GROUPS + N_EXPERTS].set(b_router)
    vec = pl.BlockSpec((None, 1, d), lambda bi, i: (bi, 0, 0))
    return pl.pallas_call(
        functools.partial(_router_kernel, tm=tm),
        out_shape=(jax.ShapeDtypeStruct((b, s, d // 2), U32),
                   jax.ShapeDtypeStruct((b, s, LANES), F32),
                   jax.ShapeDtypeStruct((1, LANES), F32)),
        grid=(b, s // tm),
        in_specs=[
            pl.BlockSpec((None, tm, d), lambda bi, i: (bi, i, 0)),
            pl.BlockSpec((1, d), lambda bi, i: (0, 0)),
            vec, vec,
            pl.BlockSpec((d, LANES), lambda bi, i: (0, 0)),
            pl.BlockSpec((1, LANES), lambda bi, i: (0, 0)),
        ],
        out_specs=(pl.BlockSpec((None, tm, d // 2), lambda bi, i: (bi, i, 0)),
                   pl.BlockSpec((None, tm, LANES), lambda bi, i: (bi, i, 0)),
                   pl.BlockSpec((1, LANES), lambda bi, i: (0, 0))),
        scratch_shapes=[pltpu.VMEM((1, LANES), F32)],
        compiler_params=_cparams("arbitrary", "arbitrary"),
        name="moe_router",
    )(x, g.reshape(1, d), sc, sh, wr, br)


def _zero_kernel(o_ref):
    o_ref[...] = jnp.zeros_like(o_ref)


def _zeros_u32(rows, cols):
    tr = _row_tile(rows, 2048)
    return pl.pallas_call(
        _zero_kernel,
        out_shape=jax.ShapeDtypeStruct((rows, cols), U32),
        grid=(rows // tr,),
        out_specs=pl.BlockSpec((tr, cols), lambda i: (i, 0)),
        compiler_params=_cparams("parallel"),
        name="moe_zero_slots",
    )()


def _dispatch_kernel(dest_ref, h_ref, xs_in_ref, xs_ref, sem, *, chunk):
    del xs_in_ref
    base = pl.program_id(0) * chunk

    def body(r, carry):
        for k in range(2):
            pltpu.make_async_copy(h_ref.at[pl.ds(r, 1), :],
                                  xs_ref.at[pl.ds(dest_ref[2 * (base + r) + k], 1), :], sem).start()
        return carry

    lax.fori_loop(0, chunk, body, 0, unroll=DMA_ISSUE_UNROLL)
    for _ in range(2):
        pltpu.make_async_copy(h_ref, xs_ref.at[pl.ds(0, chunk), :], sem).wait()


def _dispatch(dest, h_packed, n_slots):
    n, w = h_packed.shape
    chunk = _row_tile(n, 256)
    return pl.pallas_call(
        functools.partial(_dispatch_kernel, chunk=chunk),
        out_shape=jax.ShapeDtypeStruct((n_slots, w), U32),
        grid_spec=pltpu.PrefetchScalarGridSpec(
            num_scalar_prefetch=1,
            grid=(n // chunk,),
            in_specs=[pl.BlockSpec((chunk, w), lambda i, d: (i, 0)), pl.BlockSpec(memory_space=pl.ANY)],
            out_specs=pl.BlockSpec(memory_space=pl.ANY),
            scratch_shapes=[pltpu.SemaphoreType.DMA(())],
        ),
        input_output_aliases={2: 0},
        compiler_params=_cparams("arbitrary"),
        name="moe_dispatch",
    )(dest, h_packed, _zeros_u32(n_slots, w))


def _expert_kernel(be_ref, nv_ref, xs_ref, wg_ref, wu_ref, wd_ref, ys_ref, wg_b, wu_b, wd_b):
    blk = pl.program_id(0)
    valid = blk < nv_ref[0]
    changed = (blk == 0) | (be_ref[blk] != be_ref[jnp.maximum(blk - 1, 0)])

    @pl.when(valid & changed)
    def _():
        wg_b[...] = wg_ref[...].astype(BF16)
        wu_b[...] = wu_ref[...].astype(BF16)
        wd_b[...] = wd_ref[...].astype(BF16)

    @pl.when(valid)
    def _():
        x_lo, x_hi = _unpack_bf16_pair(xs_ref[...])
        half = x_lo.shape[-1]

        def proj(w):
            return (jnp.dot(x_lo, w[:half, :], preferred_element_type=F32)
                    + jnp.dot(x_hi, w[half:, :], preferred_element_type=F32))

        gate = proj(wg_b)
        hid = (gate * jax.nn.sigmoid(gate)) * proj(wu_b)
        ys_ref[...] = jnp.dot(hid.astype(BF16), wd_b[...], preferred_element_type=F32)

    @pl.when(jnp.logical_not(valid))
    def _():
        ys_ref[...] = jnp.zeros_like(ys_ref)


def _experts(block_e, n_valid, xs, w_gate, w_up, w_down, layer):
    n_slots, half = xs.shape
    d = 2 * half
    de = w_gate.shape[-1]
    n_blocks = n_slots // MOE_BLOCK
    last = lambda blk, nv: jnp.minimum(blk, nv[0] - 1)
    w_in = pl.BlockSpec((None, None, d, de), lambda blk, be, nv: (layer, be[last(blk, nv)], 0, 0))
    return pl.pallas_call(
        _expert_kernel,
        out_shape=jax.ShapeDtypeStruct((n_slots, d), F32),
        grid_spec=pltpu.PrefetchScalarGridSpec(
            num_scalar_prefetch=2,
            grid=(n_blocks,),
            in_specs=[pl.BlockSpec((MOE_BLOCK, half), lambda blk, be, nv: (last(blk, nv), 0)),
                      w_in, w_in,
                      pl.BlockSpec((None, None, de, d), lambda blk, be, nv: (layer, be[last(blk, nv)], 0, 0))],
            out_specs=pl.BlockSpec((MOE_BLOCK, d), lambda blk, be, nv: (blk, 0)),
            scratch_shapes=[pltpu.VMEM((d, de), BF16), pltpu.VMEM((d, de), BF16), pltpu.VMEM((de, d), BF16)],
        ),
        compiler_params=_cparams("arbitrary"),
        name="moe_experts",
    )(block_e, n_valid, xs, w_gate, w_up, w_down)


def _combine_kernel(dest_ref, x_ref, rt_ref, g_ref, ys_ref, o_ref, buf, sems, *, tm):
    i = pl.program_id(0)
    cur = lax.rem(i, 2)

    def gather(tile, half):
        base = tile * tm

        def body(r, carry):
            for k in range(2):
                pltpu.make_async_copy(ys_ref.at[pl.ds(dest_ref[2 * (base + r) + k], 1), :],
                                      buf.at[half, k, pl.ds(r, 1), :], sems.at[half]).start()
            return carry

        lax.fori_loop(0, tm, body, 0, unroll=DMA_ISSUE_UNROLL)

    @pl.when(i == 0)
    def _():
        gather(0, 0)

    @pl.when(i + 1 < pl.num_programs(0))
    def _():
        gather(i + 1, 1 - cur)

    for k in range(2):
        pltpu.make_async_copy(ys_ref.at[pl.ds(0, tm), :], buf.at[cur, k], sems.at[cur]).wait()
    rt = rt_ref[...]
    w0 = rt[:, ROUTE_W0:ROUTE_W0 + 1]
    w1 = rt[:, ROUTE_W1:ROUTE_W1 + 1]
    o_ref[...] = x_ref[...] + g_ref[...] * (w0 * buf[cur, 0] + w1 * buf[cur, 1])


def _combine(dest, x2, route, gate, ys, tiles_per_batch_of):
    n, d = x2.shape
    tm = _row_tile(n, 256)
    per_batch = tiles_per_batch_of(tm)
    return pl.pallas_call(
        functools.partial(_combine_kernel, tm=tm),
        out_shape=jax.ShapeDtypeStruct((n, d), F32),
        grid_spec=pltpu.PrefetchScalarGridSpec(
            num_scalar_prefetch=1,
            grid=(n // tm,),
            in_specs=[pl.BlockSpec((tm, d), lambda i, de: (i, 0)),
                      pl.BlockSpec((tm, LANES), lambda i, de: (i, 0)),
                      pl.BlockSpec((None, 1, d), lambda i, de: (i // per_batch, 0, 0)),
                      pl.BlockSpec(memory_space=pl.ANY)],
            out_specs=pl.BlockSpec((tm, d), lambda i, de: (i, 0)),
            scratch_shapes=[pltpu.VMEM((2, 2, tm, d), F32), pltpu.SemaphoreType.DMA((2,))],
        ),
        compiler_params=_cparams("arbitrary"),
        name="moe_combine",
    )(dest, x2, route, gate, ys)


def _moe_layer(x, g, sc, sh, gate, w_group, b_group, w_router, b_router, w_gate, w_up, w_down, layer):
    b, s, d = x.shape
    n = b * s
    h_packed, route, counts = _router(x, g, sc, sh, w_group, b_group, w_router, b_router)
    route = route.reshape(n, LANES)
    counts = counts[0, :N_EXPERTS].astype(I32)
    padded = (counts + MOE_BLOCK - 1) // MOE_BLOCK * MOE_BLOCK
    pad_end = jnp.cumsum(padded)
    pad_start = pad_end - padded
    e_id = route[:, ROUTE_E0:ROUTE_E1 + 1].astype(I32)
    rank = route[:, ROUTE_R0:ROUTE_R1 + 1].astype(I32)
    is_e = e_id[:, :, None] == jnp.arange(N_EXPERTS, dtype=I32)
    dest = (jnp.sum(jnp.where(is_e, pad_start, 0), axis=-1) + rank).reshape(-1)
    n_blocks = -(-(n * 2) // MOE_BLOCK) + N_EXPERTS
    block_start = jnp.arange(n_blocks, dtype=I32) * MOE_BLOCK
    block_e = jnp.minimum(jnp.sum((pad_end[None, :] <= block_start[:, None]).astype(I32), axis=1), N_EXPERTS - 1)
    n_valid = (pad_end[-1:] // MOE_BLOCK).astype(I32)
    xs = _dispatch(dest, h_packed.reshape(n, d // 2), n_blocks * MOE_BLOCK)
    ys = _experts(block_e, n_valid, xs, w_gate, w_up, w_down, layer)
    out = _combine(dest, x.reshape(n, d), route, gate, ys, lambda tm: s // tm)
    return out.reshape(b, s, d)


def kernel(x, c, w_ada, b_ada, mix_norm_g, ffn_norm_g, even_w_in, conv_b_glu, conv_dw, conv_dw_b, conv_ln_g, conv_ln_b, even_w_out, odd_w_in, qk_norm_q, qk_norm_k, lambda_q1, lambda_k1, lambda_q2, lambda_k2, diff_sub_g, odd_w_out, moe_w_group, moe_b_group, moe_w_router, moe_b_router, moe_w_gate, moe_w_up, moe_w_down):
    depth = w_ada.shape[0]
    d = x.shape[-1]
    sb_width = d // 2
    conv_width = d // 2
    diff_heads = d // (2 * HEAD_DIM)
    qk_width = diff_heads * 2 * HEAD_DIM
    mod = _ada_mod(c, w_ada, b_ada)
    for layer in range(depth):
        sh1, sc1, g1, sh2, sc2, g2 = [m[:, None, :] for m in jnp.split(mod[layer], 6, axis=-1)]
        j = layer // 2
        if layer % 2 == 0:
            q_gain = jnp.full((1, sb_width), LOG2E * HEAD_DIM ** -0.5, F32)
            proj = _inproj(x, mix_norm_g[layer], sc1, sh1, even_w_in[j], q_gain, head_norm=False)
            a_out = _stick_breaking(proj, sb_width)
            b_out = _conformer_conv(proj, 3 * sb_width, conv_width, conv_b_glu[j], conv_dw[j], conv_dw_b[j],
                                    conv_ln_g[j], conv_ln_b[j])
            x = _outproj_residual([a_out, b_out], [even_w_out[j][:sb_width], even_w_out[j][sb_width:]], x, g1)
        else:
            lam_init = 0.8 - 0.6 * math.exp(-0.3 * layer)
            reps = qk_width // HEAD_DIM
            qk_gain = jnp.concatenate([jnp.tile(qk_norm_q[j] * (LOG2E * HEAD_DIM ** -0.5), reps),
                                       jnp.tile(qk_norm_k[j], reps)])[None, :]
            proj = _inproj(x, mix_norm_g[layer], sc1, sh1, odd_w_in[j], qk_gain, head_norm=True)
            o = _diff_attention(proj, diff_heads, lambda_q1[j], lambda_k1[j], lambda_q2[j], lambda_k2[j],
                                diff_sub_g[j], lam_init)
            x = _outproj_residual([o], [odd_w_out[j]], x, g1)
        x = _moe_layer(x, ffn_norm_g[layer], sc2, sh2, g2, moe_w_group[layer], moe_b_group[layer],
                       moe_w_router[layer], moe_b_router[layer], moe_w_gate, moe_w_up, moe_w_down, layer)
    return x
```

```python
import functools
import math

import jax
import jax.numpy as jnp
from jax import lax
from jax.experimental import pallas as pl
from jax.experimental.pallas import tpu as pltpu

F32 = jnp.float32
BF16 = jnp.bfloat16
U32 = jnp.uint32
I32 = jnp.int32

EPS = 1e-6
LANES = 128
HEAD_DIM = 64
CHUNK = 64
CONV_KERNEL = 31
CONV_HALO = 32
N_GROUPS = 4
EXPERTS_PER_GROUP = 8
N_EXPERTS = N_GROUPS * EXPERTS_PER_GROUP
MOE_BLOCK = 512
ALIBI_MAX_EXP = 8.0
VMEM_LIMIT_BYTES = 56 * 1024 * 1024
HIGHEST = lax.Precision.HIGHEST


def _cparams(*sem):
    return pltpu.CompilerParams(dimension_semantics=sem, vmem_limit_bytes=VMEM_LIMIT_BYTES)


def _div_pow2(v, n):
    shift = n.bit_length() - 1
    assert 1 << shift == n
    return lax.shift_right_arithmetic(v, jnp.int32(shift))


def _row_tile(n, want):
    t = min(n, want)
    assert n % t == 0
    return t


def _ada_kernel(c_ref, w_ref, b_ref, o_ref):
    c = c_ref[...]
    c_act = c * jax.nn.sigmoid(c)
    o_ref[...] = jnp.dot(c_act, w_ref[...], preferred_element_type=F32, precision=HIGHEST) + b_ref[...]


def _ada_mod(c, w_ada, b_ada):
    depth, d, d6 = w_ada.shape
    b = c.shape[0]
    rows = 8
    c_pad = jnp.zeros((rows, d), F32).at[:b].set(c)
    tn = _row_tile(d6, 1536)
    out = pl.pallas_call(
        _ada_kernel,
        out_shape=jax.ShapeDtypeStruct((depth, rows, d6), F32),
        grid=(depth, d6 // tn),
        in_specs=[
            pl.BlockSpec((rows, d), lambda l, j: (0, 0)),
            pl.BlockSpec((None, d, tn), lambda l, j: (l, 0, j)),
            pl.BlockSpec((None, 1, tn), lambda l, j: (l, 0, j)),
        ],
        out_specs=pl.BlockSpec((None, rows, tn), lambda l, j: (l, 0, j)),
        compiler_params=_cparams("parallel", "parallel"),
        name="ada_mod",
    )(c_pad, w_ada, b_ada.reshape(depth, 1, d6))
    return out[:, :b]


def _modulated_norm(x, g, sc, sh):
    ms = jnp.mean(x * x, axis=-1, keepdims=True)
    return (x * lax.rsqrt(ms + EPS) * g) * (1.0 + sc) + sh


def _inproj_kernel(x_ref, g_ref, sc_ref, sh_ref, w_ref, gn_ref, o_ref, *scratch, n_gain, head_norm, tn):
    h = _modulated_norm(x_ref[...], g_ref[...], sc_ref[...], sh_ref[...]).astype(BF16)
    n_out = o_ref.shape[-1]
    for c0 in range(0, n_out, tn):
        y = jnp.dot(h, w_ref[:, c0:c0 + tn], preferred_element_type=F32)
        if c0 < n_gain and head_norm:
            scratch[0][:, c0:c0 + tn] = y
        elif c0 < n_gain:
            o_ref[:, c0:c0 + tn] = (y * gn_ref[:, c0:c0 + tn]).astype(o_ref.dtype)
        else:
            o_ref[:, c0:c0 + tn] = y.astype(o_ref.dtype)
    if head_norm:
        r = _div_pow2(lax.broadcasted_iota(I32, (tn, tn), 0), HEAD_DIM)
        c = _div_pow2(lax.broadcasted_iota(I32, (tn, tn), 1), HEAD_DIM)
        group_ones = jnp.where(r == c, 1.0, 0.0).astype(BF16)
        for c0 in range(0, n_gain, tn):
            y = scratch[0][:, c0:c0 + tn]
            gs = jnp.dot((y * y).astype(BF16), group_ones, preferred_element_type=F32)
            y = y * lax.rsqrt(gs * (1.0 / HEAD_DIM) + EPS) * gn_ref[:, c0:c0 + tn]
            o_ref[:, c0:c0 + tn] = y.astype(o_ref.dtype)


def _inproj(x, g, sc, sh, w, gain, head_norm):
    b, s, d = x.shape
    n_out = w.shape[1]
    tm = _row_tile(s, 512)
    tn = 256
    n_gain = gain.shape[1]
    assert n_out % tn == 0 and n_gain % tn == 0
    gn = jnp.zeros((1, n_out), F32).at[:, :n_gain].set(gain)
    vec = pl.BlockSpec((None, 1, d), lambda bi, i: (bi, 0, 0))
    return pl.pallas_call(
        functools.partial(_inproj_kernel, n_gain=n_gain, head_norm=head_norm, tn=tn),
        out_shape=jax.ShapeDtypeStruct((b, s, n_out), BF16),
        grid=(b, s // tm),
        in_specs=[
            pl.BlockSpec((None, tm, d), lambda bi, i: (bi, i, 0)),
            pl.BlockSpec((1, d), lambda bi, i: (0, 0)),
            vec, vec,
            pl.BlockSpec((d, n_out), lambda bi, i: (0, 0)),
            pl.BlockSpec((1, n_out), lambda bi, i: (0, 0)),
        ],
        out_specs=pl.BlockSpec((None, tm, n_out), lambda bi, i: (bi, i, 0)),
        scratch_shapes=[pltpu.VMEM((tm, n_gain), F32)] if head_norm else [],
        compiler_params=_cparams("parallel", "parallel"),
        name="inproj",
    )(x, g.reshape(1, d), sc, sh, w.astype(BF16), gn)


def _outproj_kernel(*refs, n_in):
    ins, ws = refs[:n_in], refs[n_in:2 * n_in]
    x_ref, g_ref, o_ref = refs[2 * n_in:]
    acc = jnp.dot(ins[0][...], ws[0][...], preferred_element_type=F32)
    for a, w in zip(ins[1:], ws[1:]):
        acc += jnp.dot(a[...], w[...], preferred_element_type=F32)
    o_ref[...] = x_ref[...] + g_ref[...] * acc


def _outproj_residual(parts, weights, x, gate):
    b, s, d = x.shape
    tm = _row_tile(s, 512)
    n_in = len(parts)
    in_specs = [pl.BlockSpec((None, tm, p.shape[-1]), lambda bi, i: (bi, i, 0)) for p in parts]
    in_specs += [pl.BlockSpec(w.shape, lambda bi, i: (0, 0)) for w in weights]
    in_specs += [pl.BlockSpec((None, tm, d), lambda bi, i: (bi, i, 0)),
                 pl.BlockSpec((None, 1, d), lambda bi, i: (bi, 0, 0))]
    return pl.pallas_call(
        functools.partial(_outproj_kernel, n_in=n_in),
        out_shape=jax.ShapeDtypeStruct((b, s, d), F32),
        grid=(b, s // tm),
        in_specs=in_specs,
        out_specs=pl.BlockSpec((None, tm, d), lambda bi, i: (bi, i, 0)),
        compiler_params=_cparams("parallel", "parallel"),
        name="outproj",
    )(*parts, *[w.astype(BF16) for w in weights], x, gate)


LOG2E = 1.4426950408889634
MASKED_LOG2 = -1e30
ONES_ROWS = 16
BIAS_PAST, BIAS_DIAG0, BIAS_DIAG1 = range(3)


def _dot_nt(a, b):
    return lax.dot_general(a, b, (((1,), (1,)), ((), ())), preferred_element_type=F32)


def _sb_kernel(q_ref, k_ref, v_ref, o_ref, acc_ref, z_ref, w_ref, *, tq, tk):
    i = pl.program_id(2)
    n_heads = LANES // HEAD_DIM
    n_diag = tq // tk
    lane = lax.broadcasted_iota(I32, (1, LANES), 1)
    r = lax.broadcasted_iota(I32, (tk, tk), 0)
    c = lax.broadcasted_iota(I32, (tk, tk), 1)
    later = jnp.where(r >= c, 1.0, 0.0).astype(BF16)
    row = lax.broadcasted_iota(I32, (tq, tk), 0)
    col = lax.broadcasted_iota(I32, (tq, tk), 1)
    q = q_ref[...]
    qs = [jnp.where((lane >= hh * HEAD_DIM) & (lane < (hh + 1) * HEAD_DIM), q, jnp.zeros_like(q))
          for hh in range(n_heads)]

    def key_tile(ref, j):
        return ref[pl.ds(pl.multiple_of(j * tk, tk), tk), :]

    def weighted_values(slot, j, carry_of_tile):
        for hh in range(n_heads):
            a = jnp.exp2((w_ref[slot, hh] - carry_of_tile[hh]).astype(BF16))
            acc_ref[hh] += jnp.dot(a, key_tile(v_ref, j), preferred_element_type=F32)

    def step(slot, j, j_next, j_prev, state, causal):
        carry, carry_prev = state
        new_carry = []
        weighted_values(1 - slot, j_prev, carry_prev)
        for hh in range(n_heads):
            z = z_ref[slot, hh]
            neg_abs = pltpu.bitcast(pltpu.bitcast(z, U32) | jnp.uint32(0x80000000), F32)
            fail = jnp.maximum(z, 0.0) + jnp.log2(1.0 + jnp.exp2(neg_abs))
            if causal is not None:
                fail = jnp.where(causal, fail, 0.0)
            incl = jnp.dot(fail.astype(BF16), later, preferred_element_type=F32)
            w = z - incl
            if causal is not None:
                w = jnp.where(causal, w, MASKED_LOG2)
            w_ref[slot, hh] = w
            new_carry.append(carry[hh] + incl[:, 0:1])
        for hh in range(n_heads):
            z_ref[1 - slot, hh] = _dot_nt(qs[hh], key_tile(k_ref, j_next))
        return tuple(new_carry), carry

    assert n_diag == 2
    top = i * n_diag + 1
    acc_ref[...] = jnp.zeros_like(acc_ref)
    w_ref[1] = jnp.full(w_ref.shape[1:], MASKED_LOG2, F32)
    for hh in range(n_heads):
        z_ref[0, hh] = _dot_nt(qs[hh], key_tile(k_ref, top))
    zero = tuple(jnp.zeros((tq, 1), F32) for _ in range(n_heads))
    state = (zero, zero)
    state = step(0, top, top - 1, top, state, col + tk < row)
    state = step(1, top - 1, jnp.maximum(top - 2, 0), top, state, col < row)

    def pair(ja, state):
        state = step(0, ja, ja - 1, ja + 1, state, None)
        return step(1, ja - 1, jnp.maximum(ja - 2, 0), ja, state, None)

    state = lax.fori_loop(0, i // 2, lambda jj, st: pair(top - 4 - 4 * jj, pair(top - 2 - 4 * jj, st)), state)
    carry, carry_prev = lax.fori_loop(0, i % 2, lambda jj, st: pair(top - 2 - 4 * (i // 2), st), state)
    weighted_values(1, 0, carry_prev)
    o_ref[...] = jnp.where(lane < HEAD_DIM, acc_ref[0], acc_ref[1]).astype(o_ref.dtype)


def _stick_breaking(proj, width):
    b, s, _ = proj.shape
    tk = _row_tile(s, 256)
    tq = _row_tile(s, 2 * tk)
    nblk = width // LANES
    return pl.pallas_call(
        functools.partial(_sb_kernel, tq=tq, tk=tk),
        out_shape=jax.ShapeDtypeStruct((b, s, width), BF16),
        grid=(b, nblk, s // tq),
        in_specs=[
            pl.BlockSpec((None, tq, LANES), lambda bi, h, i: (bi, i, h)),
            pl.BlockSpec((None, s, LANES), lambda bi, h, i: (bi, 0, nblk + h)),
            pl.BlockSpec((None, s, LANES), lambda bi, h, i: (bi, 0, 2 * nblk + h)),
        ],
        out_specs=pl.BlockSpec((None, tq, LANES), lambda bi, h, i: (bi, i, h)),
        scratch_shapes=[pltpu.VMEM((LANES // HEAD_DIM, tq, LANES), F32),
                        pltpu.VMEM((2, LANES // HEAD_DIM, tq, tk), F32),
                        pltpu.VMEM((2, LANES // HEAD_DIM, tq, tk), F32)],
        compiler_params=_cparams("parallel", "parallel", "parallel"),
        name="stick_breaking",
    )(proj, proj, proj)


def _conv_kernel(a_ref, g_ref, pa_ref, pg_ref, bglu_ref, dw_ref, dwb_ref, lng_ref, lnb_ref, o_ref, hbuf, *, ts, rc):
    i = pl.program_id(1)
    w = a_ref.shape[-1]

    def glu(a, g):
        a = a.astype(F32) + bglu_ref[:, :w]
        g = g.astype(F32) + bglu_ref[:, w:]
        return a * jax.nn.sigmoid(g)

    prev = glu(pa_ref[...], pg_ref[...])
    hbuf[0, 0:CONV_HALO] = jnp.where(i > 0, prev, 0.0)
    hbuf[0, CONV_HALO:CONV_HALO + ts] = glu(a_ref[...], g_ref[...])
    off = CONV_HALO - (CONV_KERNEL - 1)
    n_phase = hbuf.shape[0]
    for p in range(1, n_phase):
        hbuf[p, 0:CONV_HALO + ts - n_phase] = hbuf[0, p:p + CONV_HALO + ts - n_phase]
    for r0 in range(0, ts, rc):
        acc = jnp.zeros((rc, w), F32) + dwb_ref[...]
        for k in range(CONV_KERNEL):
            p = (off + k) % n_phase
            start = r0 + off + k - p
            acc = acc + dw_ref[k:k + 1, :] * hbuf[p, start:start + rc, :]
        mu = jnp.mean(acc, axis=-1, keepdims=True)
        cen = acc - mu
        var = jnp.mean(cen * cen, axis=-1, keepdims=True)
        y = cen * lax.rsqrt(var + EPS) * lng_ref[...] + lnb_ref[...]
        o_ref[r0:r0 + rc, :] = (y * jax.nn.sigmoid(y)).astype(o_ref.dtype)


def _conformer_conv(proj, col0, width, b_glu, dw, dw_b, ln_g, ln_b):
    b, s, _ = proj.shape
    ts = _row_tile(s, 256)
    assert col0 % width == 0 and ts % CONV_HALO == 0
    cb = col0 // width
    hb = ts // CONV_HALO
    cur = lambda off: pl.BlockSpec((None, ts, width), lambda bi, i: (bi, i, cb + off))
    prev = lambda off: pl.BlockSpec((None, CONV_HALO, width),
                                    lambda bi, i: (bi, jnp.maximum(i * hb - 1, 0), cb + off))
    row = lambda n: pl.BlockSpec((1, n), lambda bi, i: (0, 0))
    return pl.pallas_call(
        functools.partial(_conv_kernel, ts=ts, rc=32),
        out_shape=jax.ShapeDtypeStruct((b, s, width), BF16),
        grid=(b, s // ts),
        in_specs=[cur(0), cur(1), prev(0), prev(1), row(2 * width),
                  pl.BlockSpec((CONV_KERNEL, width), lambda bi, i: (0, 0)),
                  row(width), row(width), row(width)],
        out_specs=pl.BlockSpec((None, ts, width), lambda bi, i: (bi, i, 0)),
        scratch_shapes=[pltpu.VMEM((8, CONV_HALO + ts, width), F32)],
        compiler_params=_cparams("parallel", "parallel"),
        name="conformer_conv",
    )(proj, proj, proj, proj, b_glu.reshape(1, -1), dw, dw_b.reshape(1, -1),
      ln_g.reshape(1, -1), ln_b.reshape(1, -1))


def _diff_kernel(slope_ref, q_ref, k_ref, vt_ref, bias_ref, lq1_ref, lk1_ref, lq2_ref, lk2_ref, subg_ref, o_ref,
                 acc_ref, s_ref, p_ref, *, tq, tk, lam_init):
    h = pl.program_id(1)
    i = pl.program_id(2)
    n_diag = tq // tk
    slope = slope_ref[h]
    lane = lax.broadcasted_iota(I32, (1, LANES), 1)
    q = q_ref[...]
    zero = jnp.zeros_like(q)
    qs = (jnp.where(lane < HEAD_DIM, q, zero), jnp.where(lane >= HEAD_DIM, q, zero))

    def scores(slot, j, bias):
        kt = k_ref[pl.ds(pl.multiple_of(j * tk, tk), tk), :]
        tile_max = []
        for n in range(2):
            s = _dot_nt(kt, qs[n]) + bias
            s_ref[slot, n] = s
            tile_max.append(jnp.max(s, axis=0, keepdims=True))
        return tuple(tile_max)

    def weighted_values(slot, j, alpha):
        vt = vt_ref[:, pl.ds(pl.multiple_of(j * tk, tk), tk)]
        for n in range(2):
            acc_ref[n] = alpha[n] * acc_ref[n] + jnp.dot(vt, p_ref[slot, n], preferred_element_type=F32)

    def step(slot, j_next, j_prev, state, bias_next, shift):
        stats, alpha_prev, tile_max = state
        weighted_values(1 - slot, j_prev, alpha_prev)
        new_stats, alphas = [], []
        for n in range(2):
            m_new = jnp.maximum(stats[n], tile_max[n] + shift)
            alphas.append(jnp.exp2(stats[n] - m_new))
            p_ref[slot, n] = jnp.exp2((s_ref[slot, n] - (m_new - shift)).astype(BF16))
            new_stats.append(m_new)
        return tuple(new_stats), tuple(alphas), scores(1 - slot, j_next, bias_next)

    def shift_of(j):
        return -slope * ((i * n_diag - j) * tk).astype(F32)

    assert n_diag == 2
    top = i * n_diag + 1
    acc_ref[...] = jnp.zeros_like(acc_ref)
    p_ref[1] = jnp.zeros(p_ref.shape[1:], BF16)
    floor = jnp.full((1, tq), MASKED_LOG2, F32)
    one = jnp.ones((1, tq), F32)
    state = ((floor, floor), (one, one), scores(0, top, bias_ref[BIAS_DIAG1]))
    state = step(0, top - 1, top, state, bias_ref[BIAS_DIAG0], 0.0)
    state = step(1, jnp.maximum(top - 2, 0), top, state, bias_ref[BIAS_PAST], 0.0)

    def pair(ja, state):
        state = step(0, ja - 1, ja + 1, state, bias_ref[BIAS_PAST], shift_of(ja))
        return step(1, jnp.maximum(ja - 2, 0), ja, state, bias_ref[BIAS_PAST], shift_of(ja - 1))

    state = lax.fori_loop(0, i // 2, lambda jj, st: pair(top - 4 - 4 * jj, pair(top - 2 - 4 * jj, st)), state)
    _, alpha_last, _ = lax.fori_loop(0, i % 2, lambda jj, st: pair(top - 2 - 4 * (i // 2), st), state)
    weighted_values(1, 0, alpha_last)
    dv = acc_ref.shape[1] - ONES_ROWS
    l1, l2 = acc_ref[0, dv:dv + 1, :], acc_ref[1, dv:dv + 1, :]
    lam = (jnp.exp(jnp.sum(lq1_ref[...] * lk1_ref[...], axis=-1, keepdims=True))
           - jnp.exp(jnp.sum(lq2_ref[...] * lk2_ref[...], axis=-1, keepdims=True)) + lam_init)
    o = acc_ref[0, :dv, :] / l1 - lam * (acc_ref[1, :dv, :] / l2)
    ms = jnp.mean(o * o, axis=0, keepdims=True)
    o = o * lax.rsqrt(ms + EPS) * (1.0 - lam_init)
    o_ref[...] = (o.T * subg_ref[...]).astype(o_ref.dtype)


def _diff_attention(proj, n_heads, lq1, lk1, lq2, lk2, sub_g, lam_init):
    b, s, _ = proj.shape
    tk = _row_tile(s, 256)
    tq = _row_tile(s, 2 * tk)
    assert tk % CHUNK == 0
    v_t = jnp.swapaxes(proj[:, :, 2 * n_heads * LANES:], 1, 2).reshape(b, n_heads, LANES, s)
    v_t = jnp.concatenate([v_t, jnp.ones((b, n_heads, ONES_ROWS, s), BF16)], axis=2)
    slopes = LOG2E * jnp.exp2(-ALIBI_MAX_EXP * jnp.arange(1, n_heads + 1, dtype=F32) / n_heads)
    key = jnp.arange(tk, dtype=I32)[:, None]
    qry = jnp.arange(tq, dtype=I32)[None, :]
    rel = (qry - key).astype(F32)

    def diag(dj):
        allowed = (key + dj * tk) // CHUNK <= qry // CHUNK
        return jnp.where(allowed, -jnp.abs(rel - float(dj * tk)), -jnp.inf)

    bias = slopes[:, None, None, None] * jnp.stack([-rel, diag(0), diag(1)])
    vec = lambda n: pl.BlockSpec((1, n), lambda bi, h, i, sl: (0, 0))
    return pl.pallas_call(
        functools.partial(_diff_kernel, tq=tq, tk=tk, lam_init=lam_init),
        out_shape=jax.ShapeDtypeStruct((b, s, n_heads * LANES), BF16),
        grid_spec=pltpu.PrefetchScalarGridSpec(
            num_scalar_prefetch=1,
            grid=(b, n_heads, s // tq),
            in_specs=[
                pl.BlockSpec((None, tq, LANES), lambda bi, h, i, sl: (bi, i, h)),
                pl.BlockSpec((None, s, LANES), lambda bi, h, i, sl: (bi, 0, n_heads + h)),
                pl.BlockSpec((None, None, LANES + ONES_ROWS, s), lambda bi, h, i, sl: (bi, h, 0, 0)),
                pl.BlockSpec((None, 3, tk, tq), lambda bi, h, i, sl: (h, 0, 0, 0)),
                vec(HEAD_DIM), vec(HEAD_DIM), vec(HEAD_DIM), vec(HEAD_DIM), vec(LANES),
            ],
            out_specs=pl.BlockSpec((None, tq, LANES), lambda bi, h, i, sl: (bi, i, h)),
            scratch_shapes=[pltpu.VMEM((2, LANES + ONES_ROWS, tq), F32), pltpu.VMEM((2, 2, tk, tq), F32),
                            pltpu.VMEM((2, 2, tk, tq), BF16)],
        ),
        compiler_params=_cparams("parallel", "parallel", "parallel"),
        name="diff_attention",
    )(slopes, proj, proj, v_t, bias, lq1.reshape(1, -1), lk1.reshape(1, -1), lq2.reshape(1, -1),
      lk2.reshape(1, -1), sub_g.reshape(1, -1))


ROUTE_E0, ROUTE_E1, ROUTE_W0, ROUTE_W1, ROUTE_R0, ROUTE_R1 = range(6)
DMA_ISSUE_UNROLL = 8


def _pack_bf16_pair(lo, hi):
    lo_bits = pltpu.bitcast(lo.astype(BF16).astype(F32), U32) >> 16
    hi_bits = pltpu.bitcast(hi.astype(BF16).astype(F32), U32) & jnp.uint32(0xFFFF0000)
    return hi_bits | lo_bits


def _unpack_bf16_pair(u):
    lo = pltpu.bitcast(u << 16, F32).astype(BF16)
    hi = pltpu.bitcast(u & jnp.uint32(0xFFFF0000), F32).astype(BF16)
    return lo, hi


def _router_kernel(x_ref, g_ref, sc_ref, sh_ref, wr_ref, br_ref, hp_ref, rt_ref, cnt_ref, run_ref, *, tm):
    first = (pl.program_id(0) == 0) & (pl.program_id(1) == 0)

    @pl.when(first)
    def _():
        run_ref[...] = jnp.zeros_like(run_ref)

    h = _modulated_norm(x_ref[...], g_ref[...], sc_ref[...], sh_ref[...])
    half = h.shape[-1] // 2
    hp_ref[...] = _pack_bf16_pair(h[:, :half], h[:, half:])

    logits = jnp.dot(h, wr_ref[...], preferred_element_type=F32, precision=HIGHEST) + br_ref[...]
    lane = lax.broadcasted_iota(I32, (tm, LANES), 1)
    lanef = lane.astype(F32)
    ninf = -jnp.inf
    big = float(LANES)

    def first_argmax(vals):
        top = jnp.max(vals, axis=-1, keepdims=True)
        idx = jnp.min(jnp.where(vals == top, lanef, big), axis=-1, keepdims=True)
        return top, idx

    is_group = lane < N_GROUPS
    gl = jnp.where(is_group, logits, ninf)
    gmax, gidx = first_argmax(gl)
    g_w = 1.0 / jnp.sum(jnp.where(is_group, jnp.exp(gl - gmax), 0.0), axis=-1, keepdims=True)
    lane_group = _div_pow2(lane - N_GROUPS, EXPERTS_PER_GROUP).astype(F32)
    in_group = (lane >= N_GROUPS) & (lane < N_GROUPS + N_EXPERTS) & (lane_group == gidx)
    el = jnp.where(in_group, logits, ninf)
    v0, i0 = first_argmax(el)
    v1, i1 = first_argmax(jnp.where(lanef == i0, ninf, el))
    tt = jnp.exp(v1 - v0)
    w0 = g_w / (1.0 + tt)
    w1 = g_w * tt / (1.0 + tt)
    e0 = i0 - N_GROUPS
    e1 = i1 - N_GROUPS

    member = (lanef == e0) | (lanef == e1)
    r = lax.broadcasted_iota(I32, (tm, tm), 0)
    c = lax.broadcasted_iota(I32, (tm, tm), 1)
    earlier = jnp.where(c < r, 1.0, 0.0).astype(BF16)
    before = jnp.dot(earlier, jnp.where(member, 1.0, 0.0).astype(BF16), preferred_element_type=F32)
    before = before + run_ref[...]
    r0 = jnp.sum(jnp.where(lanef == e0, before, 0.0), axis=-1, keepdims=True)
    r1 = jnp.sum(jnp.where(lanef == e1, before, 0.0), axis=-1, keepdims=True)
    run_ref[...] += jnp.sum(jnp.where(member, 1.0, 0.0), axis=0, keepdims=True)
    cnt_ref[...] = run_ref[...]

    slab = jnp.zeros((tm, LANES), F32)
    for colv, val in ((ROUTE_E0, e0), (ROUTE_E1, e1), (ROUTE_W0, w0), (ROUTE_W1, w1), (ROUTE_R0, r0), (ROUTE_R1, r1)):
        slab = jnp.where(lane == colv, val, slab)
    rt_ref[...] = slab


def _router(x, g, sc, sh, w_group, b_group, w_router, b_router):
    b, s, d = x.shape
    tm = _row_tile(s, 1024)
    wr = jnp.zeros((d, LANES), F32).at[:, :N_GROUPS].set(w_group).at[:, N_GROUPS:N_GROUPS + N_EXPERTS].set(w_router)
    br = jnp.zeros((1, LANES), F32).at[0, :N_GROUPS].set(b_group).at[0, N_GROUPS:N_GROUPS + N_EXPERTS].set(b_router)
    vec = pl.BlockSpec((None, 1, d), lambda bi, i: (bi, 0, 0))
    return pl.pallas_call(
        functools.partial(_router_kernel, tm=tm),
        out_shape=(jax.ShapeDtypeStruct((b, s, d // 2), U32),
                   jax.ShapeDtypeStruct((b, s, LANES), F32),
                   jax.ShapeDtypeStruct((1, LANES), F32)),
        grid=(b, s // tm),
        in_specs=[
            pl.BlockSpec((None, tm, d), lambda bi, i: (bi, i, 0)),
            pl.BlockSpec((1, d), lambda bi, i: (0, 0)),
            vec, vec,
            pl.BlockSpec((d, LANES), lambda bi, i: (0, 0)),
            pl.BlockSpec((1, LANES), lambda bi, i: (0, 0)),
        ],
        out_specs=(pl.BlockSpec((None, tm, d // 2), lambda bi, i: (bi, i, 0)),
                   pl.BlockSpec((None, tm, LANES), lambda bi, i: (bi, i, 0)),
                   pl.BlockSpec((1, LANES), lambda bi, i: (0, 0))),
        scratch_shapes=[pltpu.VMEM((1, LANES), F32)],
        compiler_params=_cparams("arbitrary", "arbitrary"),
        name="moe_router",
    )(x, g.reshape(1, d), sc, sh, wr, br)


def _zero_kernel(o_ref):
    o_ref[...] = jnp.zeros_like(o_ref)


def _zeros_u32(rows, cols):
    tr = _row_tile(rows, 2048)
    return pl.pallas_call(
        _zero_kernel,
        out_shape=jax.ShapeDtypeStruct((rows, cols), U32),
        grid=(rows // tr,),
        out_specs=pl.BlockSpec((tr, cols), lambda i: (i, 0)),
        compiler_params=_cparams("parallel"),
        name="moe_zero_slots",
    )()


def _dispatch_kernel(dest_ref, h_ref, xs_in_ref, xs_ref, sem, *, chunk):
    del xs_in_ref
    base = pl.program_id(0) * chunk

    def body(r, carry):
        for k in range(2):
            pltpu.make_async_copy(h_ref.at[pl.ds(r, 1), :],
                                  xs_ref.at[pl.ds(dest_ref[2 * (base + r) + k], 1), :], sem).start()
        return carry

    lax.fori_loop(0, chunk, body, 0, unroll=DMA_ISSUE_UNROLL)
    for _ in range(2):
        pltpu.make_async_copy(h_ref, xs_ref.at[pl.ds(0, chunk), :], sem).wait()


def _dispatch(dest, h_packed, slots):
    n, w = h_packed.shape
    chunk = _row_tile(n, 256)
    return pl.pallas_call(
        functools.partial(_dispatch_kernel, chunk=chunk),
        out_shape=jax.ShapeDtypeStruct(slots.shape, U32),
        grid_spec=pltpu.PrefetchScalarGridSpec(
            num_scalar_prefetch=1,
            grid=(n // chunk,),
            in_specs=[pl.BlockSpec((chunk, w), lambda i, d: (i, 0)), pl.BlockSpec(memory_space=pl.ANY)],
            out_specs=pl.BlockSpec(memory_space=pl.ANY),
            scratch_shapes=[pltpu.SemaphoreType.DMA(())],
        ),
        input_output_aliases={2: 0},
        compiler_params=_cparams("arbitrary"),
        name="moe_dispatch",
    )(dest, h_packed, slots)


def _expert_kernel(be_ref, nv_ref, xs_ref, wg_ref, wu_ref, wd_ref, ys_ref, wg_b, wu_b, wd_b):
    blk = pl.program_id(0)
    valid = blk < nv_ref[0]
    changed = (blk == 0) | (be_ref[blk] != be_ref[jnp.maximum(blk - 1, 0)])

    @pl.when(valid & changed)
    def _():
        wg_b[...] = wg_ref[...].astype(BF16)
        wu_b[...] = wu_ref[...].astype(BF16)
        wd_b[...] = wd_ref[...].astype(BF16)

    @pl.when(valid)
    def _():
        x_lo, x_hi = _unpack_bf16_pair(xs_ref[...])
        half = x_lo.shape[-1]

        def proj(w):
            return (jnp.dot(x_lo, w[:half, :], preferred_element_type=F32)
                    + jnp.dot(x_hi, w[half:, :], preferred_element_type=F32))

        gate = proj(wg_b)
        hid = (gate * jax.nn.sigmoid(gate)) * proj(wu_b)
        ys_ref[...] = jnp.dot(hid.astype(BF16), wd_b[...], preferred_element_type=F32)

    @pl.when(jnp.logical_not(valid))
    def _():
        ys_ref[...] = jnp.zeros_like(ys_ref)


def _experts(block_e, n_valid, xs, w_gate, w_up, w_down, layer):
    n_slots, half = xs.shape
    d = 2 * half
    de = w_gate.shape[-1]
    n_blocks = n_slots // MOE_BLOCK
    last = lambda blk, nv: jnp.minimum(blk, nv[0] - 1)
    w_in = pl.BlockSpec((None, None, d, de), lambda blk, be, nv: (layer, be[last(blk, nv)], 0, 0))
    return pl.pallas_call(
        _expert_kernel,
        out_shape=jax.ShapeDtypeStruct((n_slots, d), F32),
        grid_spec=pltpu.PrefetchScalarGridSpec(
            num_scalar_prefetch=2,
            grid=(n_blocks,),
            in_specs=[pl.BlockSpec((MOE_BLOCK, half), lambda blk, be, nv: (last(blk, nv), 0)),
                      w_in, w_in,
                      pl.BlockSpec((None, None, de, d), lambda blk, be, nv: (layer, be[last(blk, nv)], 0, 0))],
            out_specs=pl.BlockSpec((MOE_BLOCK, d), lambda blk, be, nv: (blk, 0)),
            scratch_shapes=[pltpu.VMEM((d, de), BF16), pltpu.VMEM((d, de), BF16), pltpu.VMEM((de, d), BF16)],
        ),
        compiler_params=_cparams("arbitrary"),
        name="moe_experts",
    )(block_e, n_valid, xs, w_gate, w_up, w_down)


def _combine_kernel(dest_ref, x_ref, rt_ref, g_ref, ys_ref, o_ref, buf, sems, *, tm):
    i = pl.program_id(0)
    cur = lax.rem(i, 2)

    def gather(tile, half):
        base = tile * tm

        def body(r, carry):
            for k in range(2):
                pltpu.make_async_copy(ys_ref.at[pl.ds(dest_ref[2 * (base + r) + k], 1), :],
                                      buf.at[half, k, pl.ds(r, 1), :], sems.at[half]).start()
            return carry

        lax.fori_loop(0, tm, body, 0, unroll=DMA_ISSUE_UNROLL)

    @pl.when(i == 0)
    def _():
        gather(0, 0)

    @pl.when(i + 1 < pl.num_programs(0))
    def _():
        gather(i + 1, 1 - cur)

    for k in range(2):
        pltpu.make_async_copy(ys_ref.at[pl.ds(0, tm), :], buf.at[cur, k], sems.at[cur]).wait()
    rt = rt_ref[...]
    w0 = rt[:, ROUTE_W0:ROUTE_W0 + 1]
    w1 = rt[:, ROUTE_W1:ROUTE_W1 + 1]
    o_ref[...] = x_ref[...] + g_ref[...] * (w0 * buf[cur, 0] + w1 * buf[cur, 1])


def _combine(dest, x2, route, gate, ys, tiles_per_batch_of):
    n, d = x2.shape
    tm = _row_tile(n, 256)
    per_batch = tiles_per_batch_of(tm)
    return pl.pallas_call(
        functools.partial(_combine_kernel, tm=tm),
        out_shape=jax.ShapeDtypeStruct((n, d), F32),
        grid_spec=pltpu.PrefetchScalarGridSpec(
            num_scalar_prefetch=1,
            grid=(n // tm,),
            in_specs=[pl.BlockSpec((tm, d), lambda i, de: (i, 0)),
                      pl.BlockSpec((tm, LANES), lambda i, de: (i, 0)),
                      pl.BlockSpec((None, 1, d), lambda i, de: (i // per_batch, 0, 0)),
                      pl.BlockSpec(memory_space=pl.ANY)],
            out_specs=pl.BlockSpec((tm, d), lambda i, de: (i, 0)),
            scratch_shapes=[pltpu.VMEM((2, 2, tm, d), F32), pltpu.SemaphoreType.DMA((2,))],
        ),
        compiler_params=_cparams("arbitrary"),
        name="moe_combine",
    )(dest, x2, route, gate, ys)


def _moe_slot_count(n_tokens):
    return (-(-(n_tokens * 2) // MOE_BLOCK) + N_EXPERTS) * MOE_BLOCK


def _moe_layer(x, slots, g, sc, sh, gate, w_group, b_group, w_router, b_router, w_gate, w_up, w_down, layer):
    b, s, d = x.shape
    n = b * s
    h_packed, route, counts = _router(x, g, sc, sh, w_group, b_group, w_router, b_router)
    route = route.reshape(n, LANES)
    counts = counts[0, :N_EXPERTS].astype(I32)
    padded = (counts + MOE_BLOCK - 1) // MOE_BLOCK * MOE_BLOCK
    pad_end = jnp.cumsum(padded)
    pad_start = pad_end - padded
    e_id = route[:, ROUTE_E0:ROUTE_E1 + 1].astype(I32)
    rank = route[:, ROUTE_R0:ROUTE_R1 + 1].astype(I32)
    is_e = e_id[:, :, None] == jnp.arange(N_EXPERTS, dtype=I32)
    dest = (jnp.sum(jnp.where(is_e, pad_start, 0), axis=-1) + rank).reshape(-1)
    n_blocks = slots.shape[0] // MOE_BLOCK
    block_start = jnp.arange(n_blocks, dtype=I32) * MOE_BLOCK
    block_e = jnp.minimum(jnp.sum((pad_end[None, :] <= block_start[:, None]).astype(I32), axis=1), N_EXPERTS - 1)
    n_valid = (pad_end[-1:] // MOE_BLOCK).astype(I32)
    xs = _dispatch(dest, h_packed.reshape(n, d // 2), slots)
    ys = _experts(block_e, n_valid, xs, w_gate, w_up, w_down, layer)
    out = _combine(dest, x.reshape(n, d), route, gate, ys, lambda tm: s // tm)
    return out.reshape(b, s, d), xs


def kernel(x, c, w_ada, b_ada, mix_norm_g, ffn_norm_g, even_w_in, conv_b_glu, conv_dw, conv_dw_b, conv_ln_g, conv_ln_b, even_w_out, odd_w_in, qk_norm_q, qk_norm_k, lambda_q1, lambda_k1, lambda_q2, lambda_k2, diff_sub_g, odd_w_out, moe_w_group, moe_b_group, moe_w_router, moe_b_router, moe_w_gate, moe_w_up, moe_w_down):
    depth = w_ada.shape[0]
    d = x.shape[-1]
    sb_width = d // 2
    conv_width = d // 2
    diff_heads = d // (2 * HEAD_DIM)
    qk_width = diff_heads * 2 * HEAD_DIM
    mod = _ada_mod(c, w_ada, b_ada)
    slots = _zeros_u32(_moe_slot_count(x.shape[0] * x.shape[1]), d // 2)
    for layer in range(depth):
        sh1, sc1, g1, sh2, sc2, g2 = [m[:, None, :] for m in jnp.split(mod[layer], 6, axis=-1)]
        j = layer // 2
        if layer % 2 == 0:
            q_gain = jnp.full((1, sb_width), LOG2E * HEAD_DIM ** -0.5, F32)
            proj = _inproj(x, mix_norm_g[layer], sc1, sh1, even_w_in[j], q_gain, head_norm=False)
            a_out = _stick_breaking(proj, sb_width)
            b_out = _conformer_conv(proj, 3 * sb_width, conv_width, conv_b_glu[j], conv_dw[j], conv_dw_b[j],
                                    conv_ln_g[j], conv_ln_b[j])
            x = _outproj_residual([a_out, b_out], [even_w_out[j][:sb_width], even_w_out[j][sb_width:]], x, g1)
        else:
            lam_init = 0.8 - 0.6 * math.exp(-0.3 * layer)
            reps = qk_width // HEAD_DIM
            qk_gain = jnp.concatenate([jnp.tile(qk_norm_q[j] * (LOG2E * HEAD_DIM ** -0.5), reps),
                                       jnp.tile(qk_norm_k[j], reps)])[None, :]
            proj = _inproj(x, mix_norm_g[layer], sc1, sh1, odd_w_in[j], qk_gain, head_norm=True)
            o = _diff_attention(proj, diff_heads, lambda_q1[j], lambda_k1[j], lambda_q2[j], lambda_k2[j],
                                diff_sub_g[j], lam_init)
            x = _outproj_residual([o], [odd_w_out[j]], x, g1)
        x, slots = _moe_layer(x, slots, ffn_norm_g[layer], sc2, sh2, g2, moe_w_group[layer], moe_b_group[layer],
                              moe_w_router[layer], moe_b_router[layer], moe_w_gate, moe_w_up, moe_w_down, layer)
    return x
```

```python
import functools
import math

import jax
import jax.numpy as jnp
from jax import lax
from jax.experimental import pallas as pl
from jax.experimental.pallas import tpu as pltpu

F32 = jnp.float32
BF16 = jnp.bfloat16
U32 = jnp.uint32
I32 = jnp.int32

EPS = 1e-6
LANES = 128
HEAD_DIM = 64
CHUNK = 64
CONV_KERNEL = 31
CONV_HALO = 32
N_GROUPS = 4
EXPERTS_PER_GROUP = 8
N_EXPERTS = N_GROUPS * EXPERTS_PER_GROUP
MOE_BLOCK = 512
ALIBI_MAX_EXP = 8.0
VMEM_LIMIT_BYTES = 56 * 1024 * 1024
HIGHEST = lax.Precision.HIGHEST


def _cparams(*sem):
    return pltpu.CompilerParams(dimension_semantics=sem, vmem_limit_bytes=VMEM_LIMIT_BYTES)


def _div_pow2(v, n):
    shift = n.bit_length() - 1
    assert 1 << shift == n
    return lax.shift_right_arithmetic(v, jnp.int32(shift))


def _row_tile(n, want):
    t = min(n, want)
    assert n % t == 0
    return t


def _ada_kernel(c_ref, w_ref, b_ref, o_ref):
    c = c_ref[...]
    c_act = c * jax.nn.sigmoid(c)
    o_ref[...] = jnp.dot(c_act, w_ref[...], preferred_element_type=F32, precision=HIGHEST) + b_ref[...]


def _ada_mod(c, w_ada, b_ada):
    depth, d, d6 = w_ada.shape
    b = c.shape[0]
    rows = 8
    c_pad = jnp.zeros((rows, d), F32).at[:b].set(c)
    tn = _row_tile(d6, 1536)
    out = pl.pallas_call(
        _ada_kernel,
        out_shape=jax.ShapeDtypeStruct((depth, rows, d6), F32),
        grid=(depth, d6 // tn),
        in_specs=[
            pl.BlockSpec((rows, d), lambda l, j: (0, 0)),
            pl.BlockSpec((None, d, tn), lambda l, j: (l, 0, j)),
            pl.BlockSpec((None, 1, tn), lambda l, j: (l, 0, j)),
        ],
        out_specs=pl.BlockSpec((None, rows, tn), lambda l, j: (l, 0, j)),
        compiler_params=_cparams("parallel", "parallel"),
        name="ada_mod",
    )(c_pad, w_ada, b_ada.reshape(depth, 1, d6))
    return out[:, :b]


def _modulated_norm(x, g, sc, sh):
    ms = jnp.mean(x * x, axis=-1, keepdims=True)
    return (x * lax.rsqrt(ms + EPS) * g) * (1.0 + sc) + sh


def _inproj_kernel(x_ref, g_ref, sc_ref, sh_ref, w_ref, gn_ref, o_ref, *scratch, n_gain, head_norm, tn):
    h = _modulated_norm(x_ref[...], g_ref[...], sc_ref[...], sh_ref[...]).astype(BF16)
    n_out = o_ref.shape[-1]
    for c0 in range(0, n_out, tn):
        y = jnp.dot(h, w_ref[:, c0:c0 + tn], preferred_element_type=F32)
        if c0 < n_gain and head_norm:
            scratch[0][:, c0:c0 + tn] = y
        elif c0 < n_gain:
            o_ref[:, c0:c0 + tn] = (y * gn_ref[:, c0:c0 + tn]).astype(o_ref.dtype)
        else:
            o_ref[:, c0:c0 + tn] = y.astype(o_ref.dtype)
    if head_norm:
        r = _div_pow2(lax.broadcasted_iota(I32, (tn, tn), 0), HEAD_DIM)
        c = _div_pow2(lax.broadcasted_iota(I32, (tn, tn), 1), HEAD_DIM)
        group_ones = jnp.where(r == c, 1.0, 0.0).astype(BF16)
        for c0 in range(0, n_gain, tn):
            y = scratch[0][:, c0:c0 + tn]
            gs = jnp.dot((y * y).astype(BF16), group_ones, preferred_element_type=F32)
            y = y * lax.rsqrt(gs * (1.0 / HEAD_DIM) + EPS) * gn_ref[:, c0:c0 + tn]
            o_ref[:, c0:c0 + tn] = y.astype(o_ref.dtype)


def _inproj(x, g, sc, sh, w, gain, head_norm):
    b, s, d = x.shape
    n_out = w.shape[1]
    tm = _row_tile(s, 512)
    tn = 256
    n_gain = gain.shape[1]
    assert n_out % tn == 0 and n_gain % tn == 0
    gn = jnp.zeros((1, n_out), F32).at[:, :n_gain].set(gain)
    vec = pl.BlockSpec((None, 1, d), lambda bi, i: (bi, 0, 0))
    return pl.pallas_call(
        functools.partial(_inproj_kernel, n_gain=n_gain, head_norm=head_norm, tn=tn),
        out_shape=jax.ShapeDtypeStruct((b, s, n_out), BF16),
        grid=(b, s // tm),
        in_specs=[
            pl.BlockSpec((None, tm, d), lambda bi, i: (bi, i, 0)),
            pl.BlockSpec((1, d), lambda bi, i: (0, 0)),
            vec, vec,
            pl.BlockSpec((d, n_out), lambda bi, i: (0, 0)),
            pl.BlockSpec((1, n_out), lambda bi, i: (0, 0)),
        ],
        out_specs=pl.BlockSpec((None, tm, n_out), lambda bi, i: (bi, i, 0)),
        scratch_shapes=[pltpu.VMEM((tm, n_gain), F32)] if head_norm else [],
        compiler_params=_cparams("parallel", "parallel"),
        name="inproj",
    )(x, g.reshape(1, d), sc, sh, w.astype(BF16), gn)


def _outproj_kernel(*refs, n_in):
    ins, ws = refs[:n_in], refs[n_in:2 * n_in]
    x_ref, g_ref, o_ref = refs[2 * n_in:]
    acc = jnp.dot(ins[0][...], ws[0][...], preferred_element_type=F32)
    for a, w in zip(ins[1:], ws[1:]):
        acc += jnp.dot(a[...], w[...], preferred_element_type=F32)
    o_ref[...] = x_ref[...] + g_ref[...] * acc


def _outproj_residual(parts, weights, x, gate):
    b, s, d = x.shape
    tm = _row_tile(s, 512)
    n_in = len(parts)
    in_specs = [pl.BlockSpec((None, tm, p.shape[-1]), lambda bi, i: (bi, i, 0)) for p in parts]
    in_specs += [pl.BlockSpec(w.shape, lambda bi, i: (0, 0)) for w in weights]
    in_specs += [pl.BlockSpec((None, tm, d), lambda bi, i: (bi, i, 0)),
                 pl.BlockSpec((None, 1, d), lambda bi, i: (bi, 0, 0))]
    return pl.pallas_call(
        functools.partial(_outproj_kernel, n_in=n_in),
        out_shape=jax.ShapeDtypeStruct((b, s, d), F32),
        grid=(b, s // tm),
        in_specs=in_specs,
        out_specs=pl.BlockSpec((None, tm, d), lambda bi, i: (bi, i, 0)),
        compiler_params=_cparams("parallel", "parallel"),
        name="outproj",
    )(*parts, *[w.astype(BF16) for w in weights], x, gate)


LOG2E = 1.4426950408889634
MASKED_LOG2 = -1e30
ONES_ROWS = 16
BIAS_PAST, BIAS_DIAG0, BIAS_DIAG1 = range(3)


def _dot_nt(a, b):
    return lax.dot_general(a, b, (((1,), (1,)), ((), ())), preferred_element_type=F32)


def _sb_kernel(q_ref, k_ref, v_ref, o_ref, acc_ref, z_ref, w_ref, *, tq, tk):
    i = pl.program_id(2)
    n_heads = LANES // HEAD_DIM
    n_diag = tq // tk
    lane = lax.broadcasted_iota(I32, (1, LANES), 1)
    r = lax.broadcasted_iota(I32, (tk, tk), 0)
    c = lax.broadcasted_iota(I32, (tk, tk), 1)
    later = jnp.where(r >= c, 1.0, 0.0).astype(BF16)
    row = lax.broadcasted_iota(I32, (tq, tk), 0)
    col = lax.broadcasted_iota(I32, (tq, tk), 1)
    q = q_ref[...]
    qs = [jnp.where((lane >= hh * HEAD_DIM) & (lane < (hh + 1) * HEAD_DIM), q, jnp.zeros_like(q))
          for hh in range(n_heads)]

    def key_tile(ref, j):
        return ref[pl.ds(pl.multiple_of(j * tk, tk), tk), :]

    def weighted_values(slot, j, carry_of_tile):
        for hh in range(n_heads):
            a = jnp.exp2((w_ref[slot, hh] - carry_of_tile[hh]).astype(BF16))
            acc_ref[hh] += jnp.dot(a, key_tile(v_ref, j), preferred_element_type=F32)

    def step(slot, j, j_next, j_prev, state, causal):
        carry, carry_prev = state
        new_carry = []
        weighted_values(1 - slot, j_prev, carry_prev)
        for hh in range(n_heads):
            z = z_ref[slot, hh]
            neg_abs = pltpu.bitcast(pltpu.bitcast(z, U32) | jnp.uint32(0x80000000), F32)
            fail = jnp.maximum(z, 0.0) + jnp.log2(1.0 + jnp.exp2(neg_abs))
            if causal is not None:
                fail = jnp.where(causal, fail, 0.0)
            incl = jnp.dot(fail.astype(BF16), later, preferred_element_type=F32)
            w = z - incl
            if causal is not None:
                w = jnp.where(causal, w, MASKED_LOG2)
            w_ref[slot, hh] = w
            new_carry.append(carry[hh] + incl[:, 0:1])
        for hh in range(n_heads):
            z_ref[1 - slot, hh] = _dot_nt(qs[hh], key_tile(k_ref, j_next))
        return tuple(new_carry), carry

    assert n_diag == 2
    top = i * n_diag + 1
    acc_ref[...] = jnp.zeros_like(acc_ref)
    w_ref[1] = jnp.full(w_ref.shape[1:], MASKED_LOG2, F32)
    for hh in range(n_heads):
        z_ref[0, hh] = _dot_nt(qs[hh], key_tile(k_ref, top))
    zero = tuple(jnp.zeros((tq, 1), F32) for _ in range(n_heads))
    state = (zero, zero)
    state = step(0, top, top - 1, top, state, col + tk < row)
    state = step(1, top - 1, jnp.maximum(top - 2, 0), top, state, col < row)

    def pair(ja, state):
        state = step(0, ja, ja - 1, ja + 1, state, None)
        return step(1, ja - 1, jnp.maximum(ja - 2, 0), ja, state, None)

    state = lax.fori_loop(0, i // 2, lambda jj, st: pair(top - 4 - 4 * jj, pair(top - 2 - 4 * jj, st)), state)
    carry, carry_prev = lax.fori_loop(0, i % 2, lambda jj, st: pair(top - 2 - 4 * (i // 2), st), state)
    weighted_values(1, 0, carry_prev)
    o_ref[...] = jnp.where(lane < HEAD_DIM, acc_ref[0], acc_ref[1]).astype(o_ref.dtype)


def _stick_breaking(proj, width):
    b, s, _ = proj.shape
    tk = _row_tile(s, 256)
    tq = _row_tile(s, 2 * tk)
    nblk = width // LANES
    return pl.pallas_call(
        functools.partial(_sb_kernel, tq=tq, tk=tk),
        out_shape=jax.ShapeDtypeStruct((b, s, width), BF16),
        grid=(b, nblk, s // tq),
        in_specs=[
            pl.BlockSpec((None, tq, LANES), lambda bi, h, i: (bi, i, h)),
            pl.BlockSpec((None, s, LANES), lambda bi, h, i: (bi, 0, nblk + h)),
            pl.BlockSpec((None, s, LANES), lambda bi, h, i: (bi, 0, 2 * nblk + h)),
        ],
        out_specs=pl.BlockSpec((None, tq, LANES), lambda bi, h, i: (bi, i, h)),
        scratch_shapes=[pltpu.VMEM((LANES // HEAD_DIM, tq, LANES), F32),
                        pltpu.VMEM((2, LANES // HEAD_DIM, tq, tk), F32),
                        pltpu.VMEM((2, LANES // HEAD_DIM, tq, tk), F32)],
        compiler_params=_cparams("parallel", "parallel", "parallel"),
        name="stick_breaking",
    )(proj, proj, proj)


def _conv_kernel(a_ref, g_ref, pa_ref, pg_ref, bglu_ref, dw_ref, dwb_ref, lng_ref, lnb_ref, o_ref, hbuf, *, ts, rc):
    i = pl.program_id(1)
    w = a_ref.shape[-1]

    def glu(a, g):
        a = a.astype(F32) + bglu_ref[:, :w]
        g = g.astype(F32) + bglu_ref[:, w:]
        return a * jax.nn.sigmoid(g)

    prev = glu(pa_ref[...], pg_ref[...])
    hbuf[0, 0:CONV_HALO] = jnp.where(i > 0, prev, 0.0)
    hbuf[0, CONV_HALO:CONV_HALO + ts] = glu(a_ref[...], g_ref[...])
    off = CONV_HALO - (CONV_KERNEL - 1)
    n_phase = hbuf.shape[0]
    for p in range(1, n_phase):
        hbuf[p, 0:CONV_HALO + ts - n_phase] = hbuf[0, p:p + CONV_HALO + ts - n_phase]
    for r0 in range(0, ts, rc):
        acc = jnp.zeros((rc, w), F32) + dwb_ref[...]
        for k in range(CONV_KERNEL):
            p = (off + k) % n_phase
            start = r0 + off + k - p
            acc = acc + dw_ref[k:k + 1, :] * hbuf[p, start:start + rc, :]
        mu = jnp.mean(acc, axis=-1, keepdims=True)
        cen = acc - mu
        var = jnp.mean(cen * cen, axis=-1, keepdims=True)
        y = cen * lax.rsqrt(var + EPS) * lng_ref[...] + lnb_ref[...]
        o_ref[r0:r0 + rc, :] = (y * jax.nn.sigmoid(y)).astype(o_ref.dtype)


def _conformer_conv(proj, col0, width, b_glu, dw, dw_b, ln_g, ln_b):
    b, s, _ = proj.shape
    ts = _row_tile(s, 256)
    assert col0 % width == 0 and ts % CONV_HALO == 0
    cb = col0 // width
    hb = ts // CONV_HALO
    cur = lambda off: pl.BlockSpec((None, ts, width), lambda bi, i: (bi, i, cb + off))
    prev = lambda off: pl.BlockSpec((None, CONV_HALO, width),
                                    lambda bi, i: (bi, jnp.maximum(i * hb - 1, 0), cb + off))
    row = lambda n: pl.BlockSpec((1, n), lambda bi, i: (0, 0))
    return pl.pallas_call(
        functools.partial(_conv_kernel, ts=ts, rc=32),
        out_shape=jax.ShapeDtypeStruct((b, s, width), BF16),
        grid=(b, s // ts),
        in_specs=[cur(0), cur(1), prev(0), prev(1), row(2 * width),
                  pl.BlockSpec((CONV_KERNEL, width), lambda bi, i: (0, 0)),
                  row(width), row(width), row(width)],
        out_specs=pl.BlockSpec((None, ts, width), lambda bi, i: (bi, i, 0)),
        scratch_shapes=[pltpu.VMEM((8, CONV_HALO + ts, width), F32)],
        compiler_params=_cparams("parallel", "parallel"),
        name="conformer_conv",
    )(proj, proj, proj, proj, b_glu.reshape(1, -1), dw, dw_b.reshape(1, -1),
      ln_g.reshape(1, -1), ln_b.reshape(1, -1))


def _diff_kernel(slope_ref, q_ref, k_ref, vt_ref, bias_ref, lq1_ref, lk1_ref, lq2_ref, lk2_ref, subg_ref, o_ref,
                 acc_ref, s_ref, p_ref, *, tq, tk, lam_init):
    h = pl.program_id(1)
    i = pl.program_id(2)
    n_diag = tq // tk
    slope = slope_ref[h]
    lane = lax.broadcasted_iota(I32, (1, LANES), 1)
    q = q_ref[...]
    zero = jnp.zeros_like(q)
    qs = (jnp.where(lane < HEAD_DIM, q, zero), jnp.where(lane >= HEAD_DIM, q, zero))

    def scores(slot, j, bias):
        kt = k_ref[pl.ds(pl.multiple_of(j * tk, tk), tk), :]
        tile_max = []
        for n in range(2):
            s = _dot_nt(kt, qs[n]) + bias
            s_ref[slot, n] = s
            tile_max.append(jnp.max(s, axis=0, keepdims=True))
        return tuple(tile_max)

    def weighted_values(slot, j, alpha):
        vt = vt_ref[:, pl.ds(pl.multiple_of(j * tk, tk), tk)]
        for n in range(2):
            acc_ref[n] = alpha[n] * acc_ref[n] + jnp.dot(vt, p_ref[slot, n], preferred_element_type=F32)

    def step(slot, j_next, j_prev, state, bias_next, shift):
        stats, alpha_prev, tile_max = state
        weighted_values(1 - slot, j_prev, alpha_prev)
        new_stats, alphas = [], []
        for n in range(2):
            m_new = jnp.maximum(stats[n], tile_max[n] + shift)
            alphas.append(jnp.exp2(stats[n] - m_new))
            p_ref[slot, n] = jnp.exp2((s_ref[slot, n] - (m_new - shift)).astype(BF16))
            new_stats.append(m_new)
        return tuple(new_stats), tuple(alphas), scores(1 - slot, j_next, bias_next)

    def shift_of(j):
        return -slope * ((i * n_diag - j) * tk).astype(F32)

    assert n_diag == 2
    top = i * n_diag + 1
    acc_ref[...] = jnp.zeros_like(acc_ref)
    p_ref[1] = jnp.zeros(p_ref.shape[1:], BF16)
    floor = jnp.full((1, tq), MASKED_LOG2, F32)
    one = jnp.ones((1, tq), F32)
    state = ((floor, floor), (one, one), scores(0, top, bias_ref[BIAS_DIAG1]))
    state = step(0, top - 1, top, state, bias_ref[BIAS_DIAG0], 0.0)
    state = step(1, jnp.maximum(top - 2, 0), top, state, bias_ref[BIAS_PAST], 0.0)

    def pair(ja, state):
        state = step(0, ja - 1, ja + 1, state, bias_ref[BIAS_PAST], shift_of(ja))
        return step(1, jnp.maximum(ja - 2, 0), ja, state, bias_ref[BIAS_PAST], shift_of(ja - 1))

    state = lax.fori_loop(0, i // 2, lambda jj, st: pair(top - 4 - 4 * jj, pair(top - 2 - 4 * jj, st)), state)
    _, alpha_last, _ = lax.fori_loop(0, i % 2, lambda jj, st: pair(top - 2 - 4 * (i // 2), st), state)
    weighted_values(1, 0, alpha_last)
    dv = acc_ref.shape[1] - ONES_ROWS
    l1, l2 = acc_ref[0, dv:dv + 1, :], acc_ref[1, dv:dv + 1, :]
    lam = (jnp.exp(jnp.sum(lq1_ref[...] * lk1_ref[...], axis=-1, keepdims=True))
           - jnp.exp(jnp.sum(lq2_ref[...] * lk2_ref[...], axis=-1, keepdims=True)) + lam_init)
    o = acc_ref[0, :dv, :] / l1 - lam * (acc_ref[1, :dv, :] / l2)
    ms = jnp.mean(o * o, axis=0, keepdims=True)
    o = o * lax.rsqrt(ms + EPS) * (1.0 - lam_init)
    o_ref[...] = (o.T * subg_ref[...]).astype(o_ref.dtype)


def _diff_attention(proj, n_heads, lq1, lk1, lq2, lk2, sub_g, lam_init):
    b, s, _ = proj.shape
    tk = _row_tile(s, 256)
    tq = _row_tile(s, 2 * tk)
    assert tk % CHUNK == 0
    v_t = jnp.swapaxes(proj[:, :, 2 * n_heads * LANES:], 1, 2).reshape(b, n_heads, LANES, s)
    v_t = jnp.concatenate([v_t, jnp.ones((b, n_heads, ONES_ROWS, s), BF16)], axis=2)
    slopes = LOG2E * jnp.exp2(-ALIBI_MAX_EXP * jnp.arange(1, n_heads + 1, dtype=F32) / n_heads)
    key = jnp.arange(tk, dtype=I32)[:, None]
    qry = jnp.arange(tq, dtype=I32)[None, :]
    rel = (qry - key).astype(F32)

    def diag(dj):
        allowed = (key + dj * tk) // CHUNK <= qry // CHUNK
        return jnp.where(allowed, -jnp.abs(rel - float(dj * tk)), -jnp.inf)

    bias = slopes[:, None, None, None] * jnp.stack([-rel, diag(0), diag(1)])
    vec = lambda n: pl.BlockSpec((1, n), lambda bi, h, i, sl: (0, 0))
    return pl.pallas_call(
        functools.partial(_diff_kernel, tq=tq, tk=tk, lam_init=lam_init),
        out_shape=jax.ShapeDtypeStruct((b, s, n_heads * LANES), BF16),
        grid_spec=pltpu.PrefetchScalarGridSpec(
            num_scalar_prefetch=1,
            grid=(b, n_heads, s // tq),
            in_specs=[
                pl.BlockSpec((None, tq, LANES), lambda bi, h, i, sl: (bi, i, h)),
                pl.BlockSpec((None, s, LANES), lambda bi, h, i, sl: (bi, 0, n_heads + h)),
                pl.BlockSpec((None, None, LANES + ONES_ROWS, s), lambda bi, h, i, sl: (bi, h, 0, 0)),
                pl.BlockSpec((None, 3, tk, tq), lambda bi, h, i, sl: (h, 0, 0, 0)),
                vec(HEAD_DIM), vec(HEAD_DIM), vec(HEAD_DIM), vec(HEAD_DIM), vec(LANES),
            ],
            out_specs=pl.BlockSpec((None, tq, LANES), lambda bi, h, i, sl: (bi, i, h)),
            scratch_shapes=[pltpu.VMEM((2, LANES + ONES_ROWS, tq), F32), pltpu.VMEM((2, 2, tk, tq), F32),
                            pltpu.VMEM((2, 2, tk, tq), BF16)],
        ),
        compiler_params=_cparams("parallel", "parallel", "parallel"),
        name="diff_attention",
    )(slopes, proj, proj, v_t, bias, lq1.reshape(1, -1), lk1.reshape(1, -1), lq2.reshape(1, -1),
      lk2.reshape(1, -1), sub_g.reshape(1, -1))


ROUTE_E0, ROUTE_E1, ROUTE_W0, ROUTE_W1, ROUTE_R0, ROUTE_R1 = range(6)
DMA_ISSUE_UNROLL = 8


def _pack_bf16_pair(lo, hi):
    lo_bits = pltpu.bitcast(lo.astype(BF16).astype(F32), U32) >> 16
    hi_bits = pltpu.bitcast(hi.astype(BF16).astype(F32), U32) & jnp.uint32(0xFFFF0000)
    return hi_bits | lo_bits


def _unpack_bf16_pair(u):
    lo = pltpu.bitcast(u << 16, F32).astype(BF16)
    hi = pltpu.bitcast(u & jnp.uint32(0xFFFF0000), F32).astype(BF16)
    return lo, hi


def _router_kernel(x_ref, g_ref, sc_ref, sh_ref, wr_ref, br_ref, hp_ref, rt_ref, cnt_ref, run_ref, *, tm):
    first = (pl.program_id(0) == 0) & (pl.program_id(1) == 0)

    @pl.when(first)
    def _():
        run_ref[...] = jnp.zeros_like(run_ref)

    h = _modulated_norm(x_ref[...], g_ref[...], sc_ref[...], sh_ref[...])
    half = h.shape[-1] // 2
    hp_ref[...] = _pack_bf16_pair(h[:, :half], h[:, half:])

    logits = jnp.dot(h, wr_ref[...], preferred_element_type=F32, precision=HIGHEST) + br_ref[...]
    lane = lax.broadcasted_iota(I32, (tm, LANES), 1)
    lanef = lane.astype(F32)
    ninf = -jnp.inf
    big = float(LANES)

    def first_argmax(vals):
        top = jnp.max(vals, axis=-1, keepdims=True)
        idx = jnp.min(jnp.where(vals == top, lanef, big), axis=-1, keepdims=True)
        return top, idx

    is_group = lane < N_GROUPS
    gl = jnp.where(is_group, logits, ninf)
    gmax, gidx = first_argmax(gl)
    g_w = 1.0 / jnp.sum(jnp.where(is_group, jnp.exp(gl - gmax), 0.0), axis=-1, keepdims=True)
    lane_group = _div_pow2(lane - N_GROUPS, EXPERTS_PER_GROUP).astype(F32)
    in_group = (lane >= N_GROUPS) & (lane < N_GROUPS + N_EXPERTS) & (lane_group == gidx)
    el = jnp.where(in_group, logits, ninf)
    v0, i0 = first_argmax(el)
    v1, i1 = first_argmax(jnp.where(lanef == i0, ninf, el))
    tt = jnp.exp(v1 - v0)
    w0 = g_w / (1.0 + tt)
    w1 = g_w * tt / (1.0 + tt)
    e0 = i0 - N_GROUPS
    e1 = i1 - N_GROUPS

    member = (lanef == e0) | (lanef == e1)
    r = lax.broadcasted_iota(I32, (tm, tm), 0)
    c = lax.broadcasted_iota(I32, (tm, tm), 1)
    earlier = jnp.where(c < r, 1.0, 0.0).astype(BF16)
    before = jnp.dot(earlier, jnp.where(member, 1.0, 0.0).astype(BF16), preferred_element_type=F32)
    before = before + run_ref[...]
    r0 = jnp.sum(jnp.where(lanef == e0, before, 0.0), axis=-1, keepdims=True)
    r1 = jnp.sum(jnp.where(lanef == e1, before, 0.0), axis=-1, keepdims=True)
    run_ref[...] += jnp.sum(jnp.where(member, 1.0, 0.0), axis=0, keepdims=True)
    cnt_ref[...] = run_ref[...]

    slab = jnp.zeros((tm, LANES), F32)
    for colv, val in ((ROUTE_E0, e0), (ROUTE_E1, e1), (ROUTE_W0, w0), (ROUTE_W1, w1), (ROUTE_R0, r0), (ROUTE_R1, r1)):
        slab = jnp.where(lane == colv, val, slab)
    rt_ref[...] = slab


def _router(x, g, sc, sh, w_group, b_group, w_router, b_router):
    b, s, d = x.shape
    tm = _row_tile(s, 1024)
    wr = jnp.zeros((d, LANES), F32).at[:, :N_GROUPS].set(w_group).at[:, N_GROUPS:N_GROUPS + N_EXPERTS].set(w_router)
    br = jnp.zeros((1, LANES), F32).at[0, :N_GROUPS].set(b_group).at[0, N_GROUPS:N_GROUPS + N_EXPERTS].set(b_router)
    vec = pl.BlockSpec((None, 1, d), lambda bi, i: (bi, 0, 0))
    return pl.pallas_call(
        functools.partial(_router_kernel, tm=tm),
        out_shape=(jax.ShapeDtypeStruct((b, s, d // 2), U32),
                   jax.ShapeDtypeStruct((b, s, LANES), F32),
                   jax.ShapeDtypeStruct((1, LANES), F32)),
        grid=(b, s // tm),
        in_specs=[
            pl.BlockSpec((None, tm, d), lambda bi, i: (bi, i, 0)),
            pl.BlockSpec((1, d), lambda bi, i: (0, 0)),
            vec, vec,
            pl.BlockSpec((d, LANES), lambda bi, i: (0, 0)),
            pl.BlockSpec((1, LANES), lambda bi, i: (0, 0)),
        ],
        out_specs=(pl.BlockSpec((None, tm, d // 2), lambda bi, i: (bi, i, 0)),
                   pl.BlockSpec((None, tm, LANES), lambda bi, i: (bi, i, 0)),
                   pl.BlockSpec((1, LANES), lambda bi, i: (0, 0))),
        scratch_shapes=[pltpu.VMEM((1, LANES), F32)],
        compiler_params=_cparams("arbitrary", "arbitrary"),
        name="moe_router",
    )(x, g.reshape(1, d), sc, sh, wr, br)


def _zero_kernel(o_ref):
    o_ref[...] = jnp.zeros_like(o_ref)


def _zeros_u32(rows, cols):
    tr = _row_tile(rows, 2048)
    return pl.pallas_call(
        _zero_kernel,
        out_shape=jax.ShapeDtypeStruct((rows, cols), U32),
        grid=(rows // tr,),
        out_specs=pl.BlockSpec((tr, cols), lambda i: (i, 0)),
        compiler_params=_cparams("parallel"),
        name="moe_zero_slots",
    )()


def _dispatch_kernel(dest_ref, h_ref, xs_in_ref, xs_ref, sem, *, chunk):
    del xs_in_ref
    base = pl.program_id(0) * chunk

    def body(r, carry):
        for k in range(2):
            pltpu.make_async_copy(h_ref.at[pl.ds(r, 1), :],
                                  xs_ref.at[pl.ds(dest_ref[2 * (base + r) + k], 1), :], sem).start()
        return carry

    lax.fori_loop(0, chunk, body, 0, unroll=DMA_ISSUE_UNROLL)
    for _ in range(2):
        pltpu.make_async_copy(h_ref, xs_ref.at[pl.ds(0, chunk), :], sem).wait()


def _dispatch(dest, h_packed, slots):
    n, w = h_packed.shape
    chunk = _row_tile(n, 1024)
    return pl.pallas_call(
        functools.partial(_dispatch_kernel, chunk=chunk),
        out_shape=jax.ShapeDtypeStruct(slots.shape, U32),
        grid_spec=pltpu.PrefetchScalarGridSpec(
            num_scalar_prefetch=1,
            grid=(n // chunk,),
            in_specs=[pl.BlockSpec((chunk, w), lambda i, d: (i, 0)), pl.BlockSpec(memory_space=pl.ANY)],
            out_specs=pl.BlockSpec(memory_space=pl.ANY),
            scratch_shapes=[pltpu.SemaphoreType.DMA(())],
        ),
        input_output_aliases={2: 0},
        compiler_params=_cparams("arbitrary"),
        name="moe_dispatch",
    )(dest, h_packed, slots)


def _expert_kernel(be_ref, nv_ref, xs_ref, wg_ref, wu_ref, wd_ref, ys_ref, wg_b, wu_b, wd_b):
    blk = pl.program_id(0)
    valid = blk < nv_ref[0]
    changed = (blk == 0) | (be_ref[blk] != be_ref[jnp.maximum(blk - 1, 0)])

    @pl.when(valid & changed)
    def _():
        wg_b[...] = wg_ref[...].astype(BF16)
        wu_b[...] = wu_ref[...].astype(BF16)
        wd_b[...] = wd_ref[...].astype(BF16)

    @pl.when(valid)
    def _():
        x_lo, x_hi = _unpack_bf16_pair(xs_ref[...])
        half = x_lo.shape[-1]

        def proj(w):
            return (jnp.dot(x_lo, w[:half, :], preferred_element_type=F32)
                    + jnp.dot(x_hi, w[half:, :], preferred_element_type=F32))

        gate = proj(wg_b)
        hid = (gate * jax.nn.sigmoid(gate)) * proj(wu_b)
        ys_ref[...] = jnp.dot(hid.astype(BF16), wd_b[...], preferred_element_type=F32)

    @pl.when(jnp.logical_not(valid))
    def _():
        ys_ref[...] = jnp.zeros_like(ys_ref)


def _experts(block_e, n_valid, xs, w_gate, w_up, w_down, layer):
    n_slots, half = xs.shape
    d = 2 * half
    de = w_gate.shape[-1]
    n_blocks = n_slots // MOE_BLOCK
    last = lambda blk, nv: jnp.minimum(blk, nv[0] - 1)
    w_in = pl.BlockSpec((None, None, d, de), lambda blk, be, nv: (layer, be[last(blk, nv)], 0, 0))
    return pl.pallas_call(
        _expert_kernel,
        out_shape=jax.ShapeDtypeStruct((n_slots, d), F32),
        grid_spec=pltpu.PrefetchScalarGridSpec(
            num_scalar_prefetch=2,
            grid=(n_blocks,),
            in_specs=[pl.BlockSpec((MOE_BLOCK, half), lambda blk, be, nv: (last(blk, nv), 0)),
                      w_in, w_in,
                      pl.BlockSpec((None, None, de, d), lambda blk, be, nv: (layer, be[last(blk, nv)], 0, 0))],
            out_specs=pl.BlockSpec((MOE_BLOCK, d), lambda blk, be, nv: (blk, 0)),
            scratch_shapes=[pltpu.VMEM((d, de), BF16), pltpu.VMEM((d, de), BF16), pltpu.VMEM((de, d), BF16)],
        ),
        compiler_params=_cparams("arbitrary"),
        name="moe_experts",
    )(block_e, n_valid, xs, w_gate, w_up, w_down)


def _combine_kernel(dest_ref, x_ref, rt_ref, g_ref, ys_ref, o_ref, buf, sems, *, tm):
    i = pl.program_id(0)
    cur = lax.rem(i, 2)

    def gather(tile, half):
        base = tile * tm

        def body(r, carry):
            for k in range(2):
                pltpu.make_async_copy(ys_ref.at[pl.ds(dest_ref[2 * (base + r) + k], 1), :],
                                      buf.at[half, k, pl.ds(r, 1), :], sems.at[half]).start()
            return carry

        lax.fori_loop(0, tm, body, 0, unroll=DMA_ISSUE_UNROLL)

    @pl.when(i == 0)
    def _():
        gather(0, 0)

    @pl.when(i + 1 < pl.num_programs(0))
    def _():
        gather(i + 1, 1 - cur)

    for k in range(2):
        pltpu.make_async_copy(ys_ref.at[pl.ds(0, tm), :], buf.at[cur, k], sems.at[cur]).wait()
    rt = rt_ref[...]
    w0 = rt[:, ROUTE_W0:ROUTE_W0 + 1]
    w1 = rt[:, ROUTE_W1:ROUTE_W1 + 1]
    o_ref[...] = x_ref[...] + g_ref[...] * (w0 * buf[cur, 0] + w1 * buf[cur, 1])


def _combine(dest, x2, route, gate, ys, tiles_per_batch_of):
    n, d = x2.shape
    tm = _row_tile(n, 512)
    per_batch = tiles_per_batch_of(tm)
    return pl.pallas_call(
        functools.partial(_combine_kernel, tm=tm),
        out_shape=jax.ShapeDtypeStruct((n, d), F32),
        grid_spec=pltpu.PrefetchScalarGridSpec(
            num_scalar_prefetch=1,
            grid=(n // tm,),
            in_specs=[pl.BlockSpec((tm, d), lambda i, de: (i, 0)),
                      pl.BlockSpec((tm, LANES), lambda i, de: (i, 0)),
                      pl.BlockSpec((None, 1, d), lambda i, de: (i // per_batch, 0, 0)),
                      pl.BlockSpec(memory_space=pl.ANY)],
            out_specs=pl.BlockSpec((tm, d), lambda i, de: (i, 0)),
            scratch_shapes=[pltpu.VMEM((2, 2, tm, d), F32), pltpu.SemaphoreType.DMA((2,))],
        ),
        compiler_params=_cparams("arbitrary"),
        name="moe_combine",
    )(dest, x2, route, gate, ys)


def _moe_slot_count(n_tokens):
    return (-(-(n_tokens * 2) // MOE_BLOCK) + N_EXPERTS) * MOE_BLOCK


def _moe_layer(x, slots, g, sc, sh, gate, w_group, b_group, w_router, b_router, w_gate, w_up, w_down, layer):
    b, s, d = x.shape
    n = b * s
    h_packed, route, counts = _router(x, g, sc, sh, w_group, b_group, w_router, b_router)
    route = route.reshape(n, LANES)
    counts = counts[0, :N_EXPERTS].astype(I32)
    padded = (counts + MOE_BLOCK - 1) // MOE_BLOCK * MOE_BLOCK
    pad_end = jnp.cumsum(padded)
    pad_start = pad_end - padded
    e_id = route[:, ROUTE_E0:ROUTE_E1 + 1].astype(I32)
    rank = route[:, ROUTE_R0:ROUTE_R1 + 1].astype(I32)
    is_e = e_id[:, :, None] == jnp.arange(N_EXPERTS, dtype=I32)
    dest = (jnp.sum(jnp.where(is_e, pad_start, 0), axis=-1) + rank).reshape(-1)
    n_blocks = slots.shape[0] // MOE_BLOCK
    block_start = jnp.arange(n_blocks, dtype=I32) * MOE_BLOCK
    block_e = jnp.minimum(jnp.sum((pad_end[None, :] <= block_start[:, None]).astype(I32), axis=1), N_EXPERTS - 1)
    n_valid = (pad_end[-1:] // MOE_BLOCK).astype(I32)
    xs = _dispatch(dest, h_packed.reshape(n, d // 2), slots)
    ys = _experts(block_e, n_valid, xs, w_gate, w_up, w_down, layer)
    out = _combine(dest, x.reshape(n, d), route, gate, ys, lambda tm: s // tm)
    return out.reshape(b, s, d), xs


def kernel(x, c, w_ada, b_ada, mix_norm_g, ffn_norm_g, even_w_in, conv_b_glu, conv_dw, conv_dw_b, conv_ln_g, conv_ln_b, even_w_out, odd_w_in, qk_norm_q, qk_norm_k, lambda_q1, lambda_k1, lambda_q2, lambda_k2, diff_sub_g, odd_w_out, moe_w_group, moe_b_group, moe_w_router, moe_b_router, moe_w_gate, moe_w_up, moe_w_down):
    depth = w_ada.shape[0]
    d = x.shape[-1]
    sb_width = d // 2
    conv_width = d // 2
    diff_heads = d // (2 * HEAD_DIM)
    qk_width = diff_heads * 2 * HEAD_DIM
    mod = _ada_mod(c, w_ada, b_ada)
    slots = _zeros_u32(_moe_slot_count(x.shape[0] * x.shape[1]), d // 2)
    for layer in range(depth):
        sh1, sc1, g1, sh2, sc2, g2 = [m[:, None, :] for m in jnp.split(mod[layer], 6, axis=-1)]
        j = layer // 2
        if layer % 2 == 0:
            q_gain = jnp.full((1, sb_width), LOG2E * HEAD_DIM ** -0.5, F32)
            proj = _inproj(x, mix_norm_g[layer], sc1, sh1, even_w_in[j], q_gain, head_norm=False)
            a_out = _stick_breaking(proj, sb_width)
            b_out = _conformer_conv(proj, 3 * sb_width, conv_width, conv_b_glu[j], conv_dw[j], conv_dw_b[j],
                                    conv_ln_g[j], conv_ln_b[j])
            x = _outproj_residual([a_out, b_out], [even_w_out[j][:sb_width], even_w_out[j][sb_width:]], x, g1)
        else:
            lam_init = 0.8 - 0.6 * math.exp(-0.3 * layer)
            reps = qk_width // HEAD_DIM
            qk_gain = jnp.concatenate([jnp.tile(qk_norm_q[j] * (LOG2E * HEAD_DIM ** -0.5), reps),
                                       jnp.tile(qk_norm_k[j], reps)])[None, :]
            proj = _inproj(x, mix_norm_g[layer], sc1, sh1, odd_w_in[j], qk_gain, head_norm=True)
            o = _diff_attention(proj, diff_heads, lambda_q1[j], lambda_k1[j], lambda_q2[j], lambda_k2[j],
                                diff_sub_g[j], lam_init)
            x = _outproj_residual([o], [odd_w_out[j]], x, g1)
        x, slots = _moe_layer(x, slots, ffn_norm_g[layer], sc2, sh2, g2, moe_w_group[layer], moe_b_group[layer],
                              moe_w_router[layer], moe_b_router[layer], moe_w_gate, moe_w_up, moe_w_down, layer)
    return x
```

```python
import functools
import math

import jax
import jax.numpy as jnp
from jax import lax
from jax.experimental import pallas as pl
from jax.experimental.pallas import tpu as pltpu

F32 = jnp.float32
BF16 = jnp.bfloat16
U32 = jnp.uint32
I32 = jnp.int32

EPS = 1e-6
LANES = 128
HEAD_DIM = 64
CHUNK = 64
CONV_KERNEL = 31
CONV_HALO = 32
N_GROUPS = 4
EXPERTS_PER_GROUP = 8
N_EXPERTS = N_GROUPS * EXPERTS_PER_GROUP
MOE_BLOCK = 512
ALIBI_MAX_EXP = 8.0
VMEM_LIMIT_BYTES = 56 * 1024 * 1024
HIGHEST = lax.Precision.HIGHEST


def _cparams(*sem):
    return pltpu.CompilerParams(dimension_semantics=sem, vmem_limit_bytes=VMEM_LIMIT_BYTES)


def _div_pow2(v, n):
    shift = n.bit_length() - 1
    assert 1 << shift == n
    return lax.shift_right_arithmetic(v, jnp.int32(shift))


def _row_tile(n, want):
    t = min(n, want)
    assert n % t == 0
    return t


def _ada_kernel(c_ref, w_ref, b_ref, o_ref):
    c = c_ref[...]
    c_act = c * jax.nn.sigmoid(c)
    o_ref[...] = jnp.dot(c_act, w_ref[...], preferred_element_type=F32, precision=HIGHEST) + b_ref[...]


def _ada_mod(c, w_ada, b_ada):
    depth, d, d6 = w_ada.shape
    b = c.shape[0]
    rows = 8
    c_pad = jnp.zeros((rows, d), F32).at[:b].set(c)
    tn = _row_tile(d6, 1536)
    out = pl.pallas_call(
        _ada_kernel,
        out_shape=jax.ShapeDtypeStruct((depth, rows, d6), F32),
        grid=(depth, d6 // tn),
        in_specs=[
            pl.BlockSpec((rows, d), lambda l, j: (0, 0)),
            pl.BlockSpec((None, d, tn), lambda l, j: (l, 0, j)),
            pl.BlockSpec((None, 1, tn), lambda l, j: (l, 0, j)),
        ],
        out_specs=pl.BlockSpec((None, rows, tn), lambda l, j: (l, 0, j)),
        compiler_params=_cparams("parallel", "parallel"),
        name="ada_mod",
    )(c_pad, w_ada, b_ada.reshape(depth, 1, d6))
    return out[:, :b]


def _modulated_norm(x, g, sc, sh):
    ms = jnp.mean(x * x, axis=-1, keepdims=True)
    return (x * lax.rsqrt(ms + EPS) * g) * (1.0 + sc) + sh


def _inproj_kernel(x_ref, g_ref, sc_ref, sh_ref, w_ref, gn_ref, o_ref, *scratch, n_gain, head_norm, tn):
    h = _modulated_norm(x_ref[...], g_ref[...], sc_ref[...], sh_ref[...]).astype(BF16)
    n_out = o_ref.shape[-1]
    for c0 in range(0, n_out, tn):
        y = jnp.dot(h, w_ref[:, c0:c0 + tn], preferred_element_type=F32)
        if c0 < n_gain and head_norm:
            scratch[0][:, c0:c0 + tn] = y
        elif c0 < n_gain:
            o_ref[:, c0:c0 + tn] = (y * gn_ref[:, c0:c0 + tn]).astype(o_ref.dtype)
        else:
            o_ref[:, c0:c0 + tn] = y.astype(o_ref.dtype)
    if head_norm:
        r = _div_pow2(lax.broadcasted_iota(I32, (tn, tn), 0), HEAD_DIM)
        c = _div_pow2(lax.broadcasted_iota(I32, (tn, tn), 1), HEAD_DIM)
        group_ones = jnp.where(r == c, 1.0, 0.0).astype(BF16)
        for c0 in range(0, n_gain, tn):
            y = scratch[0][:, c0:c0 + tn]
            gs = jnp.dot((y * y).astype(BF16), group_ones, preferred_element_type=F32)
            y = y * lax.rsqrt(gs * (1.0 / HEAD_DIM) + EPS) * gn_ref[:, c0:c0 + tn]
            o_ref[:, c0:c0 + tn] = y.astype(o_ref.dtype)


def _inproj(x, g, sc, sh, w, gain, head_norm):
    b, s, d = x.shape
    n_out = w.shape[1]
    tm = _row_tile(s, 512)
    tn = 256
    n_gain = gain.shape[1]
    assert n_out % tn == 0 and n_gain % tn == 0
    gn = jnp.zeros((1, n_out), F32).at[:, :n_gain].set(gain)
    vec = pl.BlockSpec((None, 1, d), lambda bi, i: (bi, 0, 0))
    return pl.pallas_call(
        functools.partial(_inproj_kernel, n_gain=n_gain, head_norm=head_norm, tn=tn),
        out_shape=jax.ShapeDtypeStruct((b, s, n_out), BF16),
        grid=(b, s // tm),
        in_specs=[
            pl.BlockSpec((None, tm, d), lambda bi, i: (bi, i, 0)),
            pl.BlockSpec((1, d), lambda bi, i: (0, 0)),
            vec, vec,
            pl.BlockSpec((d, n_out), lambda bi, i: (0, 0)),
            pl.BlockSpec((1, n_out), lambda bi, i: (0, 0)),
        ],
        out_specs=pl.BlockSpec((None, tm, n_out), lambda bi, i: (bi, i, 0)),
        scratch_shapes=[pltpu.VMEM((tm, n_gain), F32)] if head_norm else [],
        compiler_params=_cparams("parallel", "parallel"),
        name="inproj",
    )(x, g.reshape(1, d), sc, sh, w.astype(BF16), gn)


def _outproj_kernel(*refs, n_in):
    ins, ws = refs[:n_in], refs[n_in:2 * n_in]
    x_ref, g_ref, o_ref = refs[2 * n_in:]
    acc = jnp.dot(ins[0][...], ws[0][...], preferred_element_type=F32)
    for a, w in zip(ins[1:], ws[1:]):
        acc += jnp.dot(a[...], w[...], preferred_element_type=F32)
    o_ref[...] = x_ref[...] + g_ref[...] * acc


def _outproj_residual(parts, weights, x, gate):
    b, s, d = x.shape
    tm = _row_tile(s, 512)
    n_in = len(parts)
    in_specs = [pl.BlockSpec((None, tm, p.shape[-1]), lambda bi, i: (bi, i, 0)) for p in parts]
    in_specs += [pl.BlockSpec(w.shape, lambda bi, i: (0, 0)) for w in weights]
    in_specs += [pl.BlockSpec((None, tm, d), lambda bi, i: (bi, i, 0)),
                 pl.BlockSpec((None, 1, d), lambda bi, i: (bi, 0, 0))]
    return pl.pallas_call(
        functools.partial(_outproj_kernel, n_in=n_in),
        out_shape=jax.ShapeDtypeStruct((b, s, d), F32),
        grid=(b, s // tm),
        in_specs=in_specs,
        out_specs=pl.BlockSpec((None, tm, d), lambda bi, i: (bi, i, 0)),
        compiler_params=_cparams("parallel", "parallel"),
        name="outproj",
    )(*parts, *[w.astype(BF16) for w in weights], x, gate)


LOG2E = 1.4426950408889634
MASKED_LOG2 = -1e30
ONES_ROWS = 16
BIAS_PAST, BIAS_DIAG0, BIAS_DIAG1 = range(3)


def _dot_nt(a, b):
    return lax.dot_general(a, b, (((1,), (1,)), ((), ())), preferred_element_type=F32)


def _sb_kernel(q_ref, k_ref, v_ref, o_ref, acc_ref, z_ref, w_ref, *, tq, tk):
    i = pl.program_id(2)
    n_heads = LANES // HEAD_DIM
    n_diag = tq // tk
    lane = lax.broadcasted_iota(I32, (1, LANES), 1)
    r = lax.broadcasted_iota(I32, (tk, tk), 0)
    c = lax.broadcasted_iota(I32, (tk, tk), 1)
    later = jnp.where(r >= c, 1.0, 0.0).astype(BF16)
    row = lax.broadcasted_iota(I32, (tq, tk), 0)
    col = lax.broadcasted_iota(I32, (tq, tk), 1)
    q = q_ref[...]
    qs = [jnp.where((lane >= hh * HEAD_DIM) & (lane < (hh + 1) * HEAD_DIM), q, jnp.zeros_like(q))
          for hh in range(n_heads)]

    def key_tile(ref, j):
        return ref[pl.ds(pl.multiple_of(j * tk, tk), tk), :]

    def weighted_values(slot, j, carry_of_tile):
        for hh in range(n_heads):
            a = jnp.exp2((w_ref[slot, hh] - carry_of_tile[hh]).astype(BF16))
            acc_ref[hh] += jnp.dot(a, key_tile(v_ref, j), preferred_element_type=F32)

    def step(slot, j, j_next, j_prev, state, causal):
        carry, carry_prev = state
        new_carry = []
        weighted_values(1 - slot, j_prev, carry_prev)
        for hh in range(n_heads):
            z = z_ref[slot, hh]
            neg_abs = pltpu.bitcast(pltpu.bitcast(z, U32) | jnp.uint32(0x80000000), F32)
            fail = jnp.maximum(z, 0.0) + jnp.log2(1.0 + jnp.exp2(neg_abs))
            if causal is not None:
                fail = jnp.where(causal, fail, 0.0)
            incl = jnp.dot(fail.astype(BF16), later, preferred_element_type=F32)
            w = z - incl
            if causal is not None:
                w = jnp.where(causal, w, MASKED_LOG2)
            w_ref[slot, hh] = w
            new_carry.append(carry[hh] + incl[:, 0:1])
        for hh in range(n_heads):
            z_ref[1 - slot, hh] = _dot_nt(qs[hh], key_tile(k_ref, j_next))
        return tuple(new_carry), carry

    assert n_diag == 2
    top = i * n_diag + 1
    acc_ref[...] = jnp.zeros_like(acc_ref)
    w_ref[1] = jnp.full(w_ref.shape[1:], MASKED_LOG2, F32)
    for hh in range(n_heads):
        z_ref[0, hh] = _dot_nt(qs[hh], key_tile(k_ref, top))
    zero = tuple(jnp.zeros((tq, 1), F32) for _ in range(n_heads))
    state = (zero, zero)
    state = step(0, top, top - 1, top, state, col + tk < row)
    state = step(1, top - 1, jnp.maximum(top - 2, 0), top, state, col < row)

    def pair(ja, state):
        state = step(0, ja, ja - 1, ja + 1, state, None)
        return step(1, ja - 1, jnp.maximum(ja - 2, 0), ja, state, None)

    state = lax.fori_loop(0, i // 2, lambda jj, st: pair(top - 4 - 4 * jj, pair(top - 2 - 4 * jj, st)), state)
    carry, carry_prev = lax.fori_loop(0, i % 2, lambda jj, st: pair(top - 2 - 4 * (i // 2), st), state)
    weighted_values(1, 0, carry_prev)
    o_ref[...] = jnp.where(lane < HEAD_DIM, acc_ref[0], acc_ref[1]).astype(o_ref.dtype)


def _stick_breaking(proj, width):
    b, s, _ = proj.shape
    tk = _row_tile(s, 256)
    tq = _row_tile(s, 2 * tk)
    nblk = width // LANES
    return pl.pallas_call(
        functools.partial(_sb_kernel, tq=tq, tk=tk),
        out_shape=jax.ShapeDtypeStruct((b, s, width), BF16),
        grid=(b, nblk, s // tq),
        in_specs=[
            pl.BlockSpec((None, tq, LANES), lambda bi, h, i: (bi, i, h)),
            pl.BlockSpec((None, s, LANES), lambda bi, h, i: (bi, 0, nblk + h)),
            pl.BlockSpec((None, s, LANES), lambda bi, h, i: (bi, 0, 2 * nblk + h)),
        ],
        out_specs=pl.BlockSpec((None, tq, LANES), lambda bi, h, i: (bi, i, h)),
        scratch_shapes=[pltpu.VMEM((LANES // HEAD_DIM, tq, LANES), F32),
                        pltpu.VMEM((2, LANES // HEAD_DIM, tq, tk), F32),
                        pltpu.VMEM((2, LANES // HEAD_DIM, tq, tk), F32)],
        compiler_params=_cparams("parallel", "parallel", "parallel"),
        name="stick_breaking",
    )(proj, proj, proj)


def _conv_kernel(a_ref, g_ref, pa_ref, pg_ref, bglu_ref, dw_ref, dwb_ref, lng_ref, lnb_ref, o_ref, hbuf, *, ts, rc):
    i = pl.program_id(1)
    w = a_ref.shape[-1]

    def glu(a, g):
        a = a.astype(F32) + bglu_ref[:, :w]
        g = g.astype(F32) + bglu_ref[:, w:]
        return a * jax.nn.sigmoid(g)

    prev = glu(pa_ref[...], pg_ref[...])
    hbuf[0, 0:CONV_HALO] = jnp.where(i > 0, prev, 0.0)
    hbuf[0, CONV_HALO:CONV_HALO + ts] = glu(a_ref[...], g_ref[...])
    off = CONV_HALO - (CONV_KERNEL - 1)
    n_phase = hbuf.shape[0]
    for p in range(1, n_phase):
        hbuf[p, 0:CONV_HALO + ts - n_phase] = hbuf[0, p:p + CONV_HALO + ts - n_phase]
    for r0 in range(0, ts, rc):
        acc = jnp.zeros((rc, w), F32) + dwb_ref[...]
        for k in range(CONV_KERNEL):
            p = (off + k) % n_phase
            start = r0 + off + k - p
            acc = acc + dw_ref[k:k + 1, :] * hbuf[p, start:start + rc, :]
        mu = jnp.mean(acc, axis=-1, keepdims=True)
        cen = acc - mu
        var = jnp.mean(cen * cen, axis=-1, keepdims=True)
        y = cen * lax.rsqrt(var + EPS) * lng_ref[...] + lnb_ref[...]
        o_ref[r0:r0 + rc, :] = (y * jax.nn.sigmoid(y)).astype(o_ref.dtype)


def _conformer_conv(proj, col0, width, b_glu, dw, dw_b, ln_g, ln_b):
    b, s, _ = proj.shape
    ts = _row_tile(s, 256)
    assert col0 % width == 0 and ts % CONV_HALO == 0
    cb = col0 // width
    hb = ts // CONV_HALO
    cur = lambda off: pl.BlockSpec((None, ts, width), lambda bi, i: (bi, i, cb + off))
    prev = lambda off: pl.BlockSpec((None, CONV_HALO, width),
                                    lambda bi, i: (bi, jnp.maximum(i * hb - 1, 0), cb + off))
    row = lambda n: pl.BlockSpec((1, n), lambda bi, i: (0, 0))
    return pl.pallas_call(
        functools.partial(_conv_kernel, ts=ts, rc=32),
        out_shape=jax.ShapeDtypeStruct((b, s, width), BF16),
        grid=(b, s // ts),
        in_specs=[cur(0), cur(1), prev(0), prev(1), row(2 * width),
                  pl.BlockSpec((CONV_KERNEL, width), lambda bi, i: (0, 0)),
                  row(width), row(width), row(width)],
        out_specs=pl.BlockSpec((None, ts, width), lambda bi, i: (bi, i, 0)),
        scratch_shapes=[pltpu.VMEM((8, CONV_HALO + ts, width), F32)],
        compiler_params=_cparams("parallel", "parallel"),
        name="conformer_conv",
    )(proj, proj, proj, proj, b_glu.reshape(1, -1), dw, dw_b.reshape(1, -1),
      ln_g.reshape(1, -1), ln_b.reshape(1, -1))


def _diff_kernel(slope_ref, q_ref, k_ref, vt_ref, bias_ref, lq1_ref, lk1_ref, lq2_ref, lk2_ref, subg_ref, o_ref,
                 acc_ref, s_ref, p_ref, *, tq, tk, lam_init):
    h = pl.program_id(1)
    i = pl.program_id(2)
    n_diag = tq // tk
    slope = slope_ref[h]
    lane = lax.broadcasted_iota(I32, (1, LANES), 1)
    q = q_ref[...]
    zero = jnp.zeros_like(q)
    qs = (jnp.where(lane < HEAD_DIM, q, zero), jnp.where(lane >= HEAD_DIM, q, zero))

    def scores(slot, j, bias):
        kt = k_ref[pl.ds(pl.multiple_of(j * tk, tk), tk), :]
        tile_max = []
        for n in range(2):
            s = _dot_nt(kt, qs[n]) + bias
            s_ref[slot, n] = s
            tile_max.append(jnp.max(s, axis=0, keepdims=True))
        return tuple(tile_max)

    def weighted_values(slot, j, alpha):
        vt = vt_ref[:, pl.ds(pl.multiple_of(j * tk, tk), tk)]
        for n in range(2):
            acc_ref[n] = alpha[n] * acc_ref[n] + jnp.dot(vt, p_ref[slot, n], preferred_element_type=F32)

    def step(slot, j_next, j_prev, state, bias_next, shift):
        stats, alpha_prev, tile_max = state
        weighted_values(1 - slot, j_prev, alpha_prev)
        new_stats, alphas = [], []
        for n in range(2):
            m_new = jnp.maximum(stats[n], tile_max[n] + shift)
            alphas.append(jnp.exp2(stats[n] - m_new))
            p_ref[slot, n] = jnp.exp2((s_ref[slot, n] - (m_new - shift)).astype(BF16))
            new_stats.append(m_new)
        return tuple(new_stats), tuple(alphas), scores(1 - slot, j_next, bias_next)

    def shift_of(j):
        return -slope * ((i * n_diag - j) * tk).astype(F32)

    assert n_diag == 2
    top = i * n_diag + 1
    acc_ref[...] = jnp.zeros_like(acc_ref)
    p_ref[1] = jnp.zeros(p_ref.shape[1:], BF16)
    floor = jnp.full((1, tq), MASKED_LOG2, F32)
    one = jnp.ones((1, tq), F32)
    state = ((floor, floor), (one, one), scores(0, top, bias_ref[BIAS_DIAG1]))
    state = step(0, top - 1, top, state, bias_ref[BIAS_DIAG0], 0.0)
    state = step(1, jnp.maximum(top - 2, 0), top, state, bias_ref[BIAS_PAST], 0.0)

    def pair(ja, state):
        state = step(0, ja - 1, ja + 1, state, bias_ref[BIAS_PAST], shift_of(ja))
        return step(1, jnp.maximum(ja - 2, 0), ja, state, bias_ref[BIAS_PAST], shift_of(ja - 1))

    state = lax.fori_loop(0, i // 2, lambda jj, st: pair(top - 4 - 4 * jj, pair(top - 2 - 4 * jj, st)), state)
    _, alpha_last, _ = lax.fori_loop(0, i % 2, lambda jj, st: pair(top - 2 - 4 * (i // 2), st), state)
    weighted_values(1, 0, alpha_last)
    dv = acc_ref.shape[1] - ONES_ROWS
    l1, l2 = acc_ref[0, dv:dv + 1, :], acc_ref[1, dv:dv + 1, :]
    lam = (jnp.exp(jnp.sum(lq1_ref[...] * lk1_ref[...], axis=-1, keepdims=True))
           - jnp.exp(jnp.sum(lq2_ref[...] * lk2_ref[...], axis=-1, keepdims=True)) + lam_init)
    o = acc_ref[0, :dv, :] / l1 - lam * (acc_ref[1, :dv, :] / l2)
    ms = jnp.mean(o * o, axis=0, keepdims=True)
    o = o * lax.rsqrt(ms + EPS) * (1.0 - lam_init)
    o_ref[...] = (o.T * subg_ref[...]).astype(o_ref.dtype)


def _diff_attention(proj, n_heads, lq1, lk1, lq2, lk2, sub_g, lam_init):
    b, s, _ = proj.shape
    tk = _row_tile(s, 256)
    tq = _row_tile(s, 2 * tk)
    assert tk % CHUNK == 0
    v_t = jnp.swapaxes(proj[:, :, 2 * n_heads * LANES:], 1, 2).reshape(b, n_heads, LANES, s)
    v_t = jnp.concatenate([v_t, jnp.ones((b, n_heads, ONES_ROWS, s), BF16)], axis=2)
    slopes = LOG2E * jnp.exp2(-ALIBI_MAX_EXP * jnp.arange(1, n_heads + 1, dtype=F32) / n_heads)
    key = jnp.arange(tk, dtype=I32)[:, None]
    qry = jnp.arange(tq, dtype=I32)[None, :]
    rel = (qry - key).astype(F32)

    def diag(dj):
        allowed = (key + dj * tk) // CHUNK <= qry // CHUNK
        return jnp.where(allowed, -jnp.abs(rel - float(dj * tk)), -jnp.inf)

    bias = slopes[:, None, None, None] * jnp.stack([-rel, diag(0), diag(1)])
    vec = lambda n: pl.BlockSpec((1, n), lambda bi, h, i, sl: (0, 0))
    return pl.pallas_call(
        functools.partial(_diff_kernel, tq=tq, tk=tk, lam_init=lam_init),
        out_shape=jax.ShapeDtypeStruct((b, s, n_heads * LANES), BF16),
        grid_spec=pltpu.PrefetchScalarGridSpec(
            num_scalar_prefetch=1,
            grid=(b, n_heads, s // tq),
            in_specs=[
                pl.BlockSpec((None, tq, LANES), lambda bi, h, i, sl: (bi, i, h)),
                pl.BlockSpec((None, s, LANES), lambda bi, h, i, sl: (bi, 0, n_heads + h)),
                pl.BlockSpec((None, None, LANES + ONES_ROWS, s), lambda bi, h, i, sl: (bi, h, 0, 0)),
                pl.BlockSpec((None, 3, tk, tq), lambda bi, h, i, sl: (h, 0, 0, 0)),
                vec(HEAD_DIM), vec(HEAD_DIM), vec(HEAD_DIM), vec(HEAD_DIM), vec(LANES),
            ],
            out_specs=pl.BlockSpec((None, tq, LANES), lambda bi, h, i, sl: (bi, i, h)),
            scratch_shapes=[pltpu.VMEM((2, LANES + ONES_ROWS, tq), F32), pltpu.VMEM((2, 2, tk, tq), F32),
                            pltpu.VMEM((2, 2, tk, tq), BF16)],
        ),
        compiler_params=_cparams("parallel", "parallel", "parallel"),
        name="diff_attention",
    )(slopes, proj, proj, v_t, bias, lq1.reshape(1, -1), lk1.reshape(1, -1), lq2.reshape(1, -1),
      lk2.reshape(1, -1), sub_g.reshape(1, -1))


ROUTE_E0, ROUTE_E1, ROUTE_W0, ROUTE_W1, ROUTE_R0, ROUTE_R1 = range(6)
DMA_ISSUE_UNROLL = 8


def _pack_bf16_pair(lo, hi):
    lo_bits = pltpu.bitcast(lo.astype(BF16).astype(F32), U32) >> 16
    hi_bits = pltpu.bitcast(hi.astype(BF16).astype(F32), U32) & jnp.uint32(0xFFFF0000)
    return hi_bits | lo_bits


def _unpack_bf16_pair(u):
    lo = pltpu.bitcast(u << 16, F32).astype(BF16)
    hi = pltpu.bitcast(u & jnp.uint32(0xFFFF0000), F32).astype(BF16)
    return lo, hi


def _router_kernel(x_ref, g_ref, sc_ref, sh_ref, wr_ref, br_ref, hp_ref, rt_ref, cnt_ref, run_ref, *, tm):
    first = (pl.program_id(0) == 0) & (pl.program_id(1) == 0)

    @pl.when(first)
    def _():
        run_ref[...] = jnp.zeros_like(run_ref)

    h = _modulated_norm(x_ref[...], g_ref[...], sc_ref[...], sh_ref[...])
    half = h.shape[-1] // 2
    hp_ref[...] = _pack_bf16_pair(h[:, :half], h[:, half:])

    logits = jnp.dot(h, wr_ref[...], preferred_element_type=F32, precision=HIGHEST) + br_ref[...]
    lane = lax.broadcasted_iota(I32, (tm, LANES), 1)
    lanef = lane.astype(F32)
    ninf = -jnp.inf
    big = float(LANES)

    def first_argmax(vals):
        top = jnp.max(vals, axis=-1, keepdims=True)
        idx = jnp.min(jnp.where(vals == top, lanef, big), axis=-1, keepdims=True)
        return top, idx

    is_group = lane < N_GROUPS
    gl = jnp.where(is_group, logits, ninf)
    gmax, gidx = first_argmax(gl)
    g_w = 1.0 / jnp.sum(jnp.where(is_group, jnp.exp(gl - gmax), 0.0), axis=-1, keepdims=True)
    lane_group = _div_pow2(lane - N_GROUPS, EXPERTS_PER_GROUP).astype(F32)
    in_group = (lane >= N_GROUPS) & (lane < N_GROUPS + N_EXPERTS) & (lane_group == gidx)
    el = jnp.where(in_group, logits, ninf)
    v0, i0 = first_argmax(el)
    v1, i1 = first_argmax(jnp.where(lanef == i0, ninf, el))
    tt = jnp.exp(v1 - v0)
    w0 = g_w / (1.0 + tt)
    w1 = g_w * tt / (1.0 + tt)
    e0 = i0 - N_GROUPS
    e1 = i1 - N_GROUPS

    member = (lanef == e0) | (lanef == e1)
    r = lax.broadcasted_iota(I32, (tm, tm), 0)
    c = lax.broadcasted_iota(I32, (tm, tm), 1)
    earlier = jnp.where(c < r, 1.0, 0.0).astype(BF16)
    before = jnp.dot(earlier, jnp.where(member, 1.0, 0.0).astype(BF16), preferred_element_type=F32)
    before = before + run_ref[...]
    r0 = jnp.sum(jnp.where(lanef == e0, before, 0.0), axis=-1, keepdims=True)
    r1 = jnp.sum(jnp.where(lanef == e1, before, 0.0), axis=-1, keepdims=True)
    run_ref[...] += jnp.sum(jnp.where(member, 1.0, 0.0), axis=0, keepdims=True)
    cnt_ref[...] = run_ref[...]

    slab = jnp.zeros((tm, LANES), F32)
    for colv, val in ((ROUTE_E0, e0), (ROUTE_E1, e1), (ROUTE_W0, w0), (ROUTE_W1, w1), (ROUTE_R0, r0), (ROUTE_R1, r1)):
        slab = jnp.where(lane == colv, val, slab)
    rt_ref[...] = slab


def _router(x, g, sc, sh, w_group, b_group, w_router, b_router):
    b, s, d = x.shape
    tm = _row_tile(s, 1024)
    wr = jnp.zeros((d, LANES), F32).at[:, :N_GROUPS].set(w_group).at[:, N_GROUPS:N_GROUPS + N_EXPERTS].set(w_router)
    br = jnp.zeros((1, LANES), F32).at[0, :N_GROUPS].set(b_group).at[0, N_GROUPS:N_GROUPS + N_EXPERTS].set(b_router)
    vec = pl.BlockSpec((None, 1, d), lambda bi, i: (bi, 0, 0))
    return pl.pallas_call(
        functools.partial(_router_kernel, tm=tm),
        out_shape=(jax.ShapeDtypeStruct((b, s, d // 2), U32),
                   jax.ShapeDtypeStruct((b, s, LANES), F32),
                   jax.ShapeDtypeStruct((1, LANES), F32)),
        grid=(b, s // tm),
        in_specs=[
            pl.BlockSpec((None, tm, d), lambda bi, i: (bi, i, 0)),
            pl.BlockSpec((1, d), lambda bi, i: (0, 0)),
            vec, vec,
            pl.BlockSpec((d, LANES), lambda bi, i: (0, 0)),
            pl.BlockSpec((1, LANES), lambda bi, i: (0, 0)),
        ],
        out_specs=(pl.BlockSpec((None, tm, d // 2), lambda bi, i: (bi, i, 0)),
                   pl.BlockSpec((None, tm, LANES), lambda bi, i: (bi, i, 0)),
                   pl.BlockSpec((1, LANES), lambda bi, i: (0, 0))),
        scratch_shapes=[pltpu.VMEM((1, LANES), F32)],
        compiler_params=_cparams("arbitrary", "arbitrary"),
        name="moe_router",
    )(x, g.reshape(1, d), sc, sh, wr, br)


def _zero_kernel(o_ref):
    o_ref[...] = jnp.zeros_like(o_ref)


def _zeros_u32(rows, cols):
    tr = _row_tile(rows, 2048)
    return pl.pallas_call(
        _zero_kernel,
        out_shape=jax.ShapeDtypeStruct((rows, cols), U32),
        grid=(rows // tr,),
        out_specs=pl.BlockSpec((tr, cols), lambda i: (i, 0)),
        compiler_params=_cparams("parallel"),
        name="moe_zero_slots",
    )()


def _dispatch_kernel(dest_ref, h_ref, xs_in_ref, xs_ref, sem, *, chunk):
    del xs_in_ref
    base = pl.program_id(0) * chunk

    def body(r, carry):
        for k in range(2):
            pltpu.make_async_copy(h_ref.at[pl.ds(r, 1), :],
                                  xs_ref.at[pl.ds(dest_ref[2 * (base + r) + k], 1), :], sem).start(priority=k)
        return carry

    lax.fori_loop(0, chunk, body, 0, unroll=DMA_ISSUE_UNROLL)
    for _ in range(2):
        pltpu.make_async_copy(h_ref, xs_ref.at[pl.ds(0, chunk), :], sem).wait()


def _dispatch(dest, h_packed, slots):
    n, w = h_packed.shape
    chunk = _row_tile(n, 1024)
    return pl.pallas_call(
        functools.partial(_dispatch_kernel, chunk=chunk),
        out_shape=jax.ShapeDtypeStruct(slots.shape, U32),
        grid_spec=pltpu.PrefetchScalarGridSpec(
            num_scalar_prefetch=1,
            grid=(n // chunk,),
            in_specs=[pl.BlockSpec((chunk, w), lambda i, d: (i, 0)), pl.BlockSpec(memory_space=pl.ANY)],
            out_specs=pl.BlockSpec(memory_space=pl.ANY),
            scratch_shapes=[pltpu.SemaphoreType.DMA(())],
        ),
        input_output_aliases={2: 0},
        compiler_params=_cparams("arbitrary"),
        name="moe_dispatch",
    )(dest, h_packed, slots)


def _expert_kernel(be_ref, nv_ref, xs_ref, wg_ref, wu_ref, wd_ref, ys_ref, wg_b, wu_b, wd_b):
    blk = pl.program_id(0)
    valid = blk < nv_ref[0]
    changed = (blk == 0) | (be_ref[blk] != be_ref[jnp.maximum(blk - 1, 0)])

    @pl.when(valid & changed)
    def _():
        wg_b[...] = wg_ref[...].astype(BF16)
        wu_b[...] = wu_ref[...].astype(BF16)
        wd_b[...] = wd_ref[...].astype(BF16)

    @pl.when(valid)
    def _():
        x_lo, x_hi = _unpack_bf16_pair(xs_ref[...])
        half = x_lo.shape[-1]

        def proj(w):
            return (jnp.dot(x_lo, w[:half, :], preferred_element_type=F32)
                    + jnp.dot(x_hi, w[half:, :], preferred_element_type=F32))

        gate = proj(wg_b)
        hid = (gate * jax.nn.sigmoid(gate)) * proj(wu_b)
        ys_ref[...] = jnp.dot(hid.astype(BF16), wd_b[...], preferred_element_type=F32)

    @pl.when(jnp.logical_not(valid))
    def _():
        ys_ref[...] = jnp.zeros_like(ys_ref)


def _experts(block_e, n_valid, xs, w_gate, w_up, w_down, layer):
    n_slots, half = xs.shape
    d = 2 * half
    de = w_gate.shape[-1]
    n_blocks = n_slots // MOE_BLOCK
    last = lambda blk, nv: jnp.minimum(blk, nv[0] - 1)
    w_in = pl.BlockSpec((None, None, d, de), lambda blk, be, nv: (layer, be[last(blk, nv)], 0, 0))
    return pl.pallas_call(
        _expert_kernel,
        out_shape=jax.ShapeDtypeStruct((n_slots, d), F32),
        grid_spec=pltpu.PrefetchScalarGridSpec(
            num_scalar_prefetch=2,
            grid=(n_blocks,),
            in_specs=[pl.BlockSpec((MOE_BLOCK, half), lambda blk, be, nv: (last(blk, nv), 0)),
                      w_in, w_in,
                      pl.BlockSpec((None, None, de, d), lambda blk, be, nv: (layer, be[last(blk, nv)], 0, 0))],
            out_specs=pl.BlockSpec((MOE_BLOCK, d), lambda blk, be, nv: (blk, 0)),
            scratch_shapes=[pltpu.VMEM((d, de), BF16), pltpu.VMEM((d, de), BF16), pltpu.VMEM((de, d), BF16)],
        ),
        compiler_params=_cparams("arbitrary"),
        name="moe_experts",
    )(block_e, n_valid, xs, w_gate, w_up, w_down)


def _combine_kernel(dest_ref, x_ref, rt_ref, g_ref, ys_ref, o_ref, buf, sems, *, tm):
    i = pl.program_id(0)
    cur = lax.rem(i, 2)

    def gather(tile, half):
        base = tile * tm

        def body(r, carry):
            for k in range(2):
                pltpu.make_async_copy(ys_ref.at[pl.ds(dest_ref[2 * (base + r) + k], 1), :],
                                      buf.at[half, k, pl.ds(r, 1), :], sems.at[half]).start(priority=k)
            return carry

        lax.fori_loop(0, tm, body, 0, unroll=DMA_ISSUE_UNROLL)

    @pl.when(i == 0)
    def _():
        gather(0, 0)

    @pl.when(i + 1 < pl.num_programs(0))
    def _():
        gather(i + 1, 1 - cur)

    for k in range(2):
        pltpu.make_async_copy(ys_ref.at[pl.ds(0, tm), :], buf.at[cur, k], sems.at[cur]).wait()
    rt = rt_ref[...]
    w0 = rt[:, ROUTE_W0:ROUTE_W0 + 1]
    w1 = rt[:, ROUTE_W1:ROUTE_W1 + 1]
    o_ref[...] = x_ref[...] + g_ref[...] * (w0 * buf[cur, 0] + w1 * buf[cur, 1])


def _combine(dest, x2, route, gate, ys, tiles_per_batch_of):
    n, d = x2.shape
    tm = _row_tile(n, 512)
    per_batch = tiles_per_batch_of(tm)
    return pl.pallas_call(
        functools.partial(_combine_kernel, tm=tm),
        out_shape=jax.ShapeDtypeStruct((n, d), F32),
        grid_spec=pltpu.PrefetchScalarGridSpec(
            num_scalar_prefetch=1,
            grid=(n // tm,),
            in_specs=[pl.BlockSpec((tm, d), lambda i, de: (i, 0)),
                      pl.BlockSpec((tm, LANES), lambda i, de: (i, 0)),
                      pl.BlockSpec((None, 1, d), lambda i, de: (i // per_batch, 0, 0)),
                      pl.BlockSpec(memory_space=pl.ANY)],
            out_specs=pl.BlockSpec((tm, d), lambda i, de: (i, 0)),
            scratch_shapes=[pltpu.VMEM((2, 2, tm, d), F32), pltpu.SemaphoreType.DMA((2,))],
        ),
        compiler_params=_cparams("arbitrary"),
        name="moe_combine",
    )(dest, x2, route, gate, ys)


def _moe_slot_count(n_tokens):
    return (-(-(n_tokens * 2) // MOE_BLOCK) + N_EXPERTS) * MOE_BLOCK


def _moe_layer(x, slots, g, sc, sh, gate, w_group, b_group, w_router, b_router, w_gate, w_up, w_down, layer):
    b, s, d = x.shape
    n = b * s
    h_packed, route, counts = _router(x, g, sc, sh, w_group, b_group, w_router, b_router)
    route = route.reshape(n, LANES)
    counts = counts[0, :N_EXPERTS].astype(I32)
    padded = (counts + MOE_BLOCK - 1) // MOE_BLOCK * MOE_BLOCK
    pad_end = jnp.cumsum(padded)
    pad_start = pad_end - padded
    e_id = route[:, ROUTE_E0:ROUTE_E1 + 1].astype(I32)
    rank = route[:, ROUTE_R0:ROUTE_R1 + 1].astype(I32)
    is_e = e_id[:, :, None] == jnp.arange(N_EXPERTS, dtype=I32)
    dest = (jnp.sum(jnp.where(is_e, pad_start, 0), axis=-1) + rank).reshape(-1)
    n_blocks = slots.shape[0] // MOE_BLOCK
    block_start = jnp.arange(n_blocks, dtype=I32) * MOE_BLOCK
    block_e = jnp.minimum(jnp.sum((pad_end[None, :] <= block_start[:, None]).astype(I32), axis=1), N_EXPERTS - 1)
    n_valid = (pad_end[-1:] // MOE_BLOCK).astype(I32)
    xs = _dispatch(dest, h_packed.reshape(n, d // 2), slots)
    ys = _experts(block_e, n_valid, xs, w_gate, w_up, w_down, layer)
    out = _combine(dest, x.reshape(n, d), route, gate, ys, lambda tm: s // tm)
    return out.reshape(b, s, d), xs


def kernel(x, c, w_ada, b_ada, mix_norm_g, ffn_norm_g, even_w_in, conv_b_glu, conv_dw, conv_dw_b, conv_ln_g, conv_ln_b, even_w_out, odd_w_in, qk_norm_q, qk_norm_k, lambda_q1, lambda_k1, lambda_q2, lambda_k2, diff_sub_g, odd_w_out, moe_w_group, moe_b_group, moe_w_router, moe_b_router, moe_w_gate, moe_w_up, moe_w_down):
    depth = w_ada.shape[0]
    d = x.shape[-1]
    sb_width = d // 2
    conv_width = d // 2
    diff_heads = d // (2 * HEAD_DIM)
    qk_width = diff_heads * 2 * HEAD_DIM
    mod = _ada_mod(c, w_ada, b_ada)
    slots = _zeros_u32(_moe_slot_count(x.shape[0] * x.shape[1]), d // 2)
    for layer in range(depth):
        sh1, sc1, g1, sh2, sc2, g2 = [m[:, None, :] for m in jnp.split(mod[layer], 6, axis=-1)]
        j = layer // 2
        if layer % 2 == 0:
            q_gain = jnp.full((1, sb_width), LOG2E * HEAD_DIM ** -0.5, F32)
            proj = _inproj(x, mix_norm_g[layer], sc1, sh1, even_w_in[j], q_gain, head_norm=False)
            a_out = _stick_breaking(proj, sb_width)
            b_out = _conformer_conv(proj, 3 * sb_width, conv_width, conv_b_glu[j], conv_dw[j], conv_dw_b[j],
                                    conv_ln_g[j], conv_ln_b[j])
            x = _outproj_residual([a_out, b_out], [even_w_out[j][:sb_width], even_w_out[j][sb_width:]], x, g1)
        else:
            lam_init = 0.8 - 0.6 * math.exp(-0.3 * layer)
            reps = qk_width // HEAD_DIM
            qk_gain = jnp.concatenate([jnp.tile(qk_norm_q[j] * (LOG2E * HEAD_DIM ** -0.5), reps),
                                       jnp.tile(qk_norm_k[j], reps)])[None, :]
            proj = _inproj(x, mix_norm_g[layer], sc1, sh1, odd_w_in[j], qk_gain, head_norm=True)
            o = _diff_attention(proj, diff_heads, lambda_q1[j], lambda_k1[j], lambda_q2[j], lambda_k2[j],
                                diff_sub_g[j], lam_init)
            x = _outproj_residual([o], [odd_w_out[j]], x, g1)
        x, slots = _moe_layer(x, slots, ffn_norm_g[layer], sc2, sh2, g2, moe_w_group[layer], moe_b_group[layer],
                              moe_w_router[layer], moe_b_router[layer], moe_w_gate, moe_w_up, moe_w_down, layer)
    return x
```

```python
import functools
import math

import jax
import jax.numpy as jnp
from jax import lax
from jax.experimental import pallas as pl
from jax.experimental.pallas import tpu as pltpu

F32 = jnp.float32
BF16 = jnp.bfloat16
U32 = jnp.uint32
I32 = jnp.int32

EPS = 1e-6
LANES = 128
HEAD_DIM = 64
CHUNK = 64
CONV_KERNEL = 31
CONV_HALO = 32
N_GROUPS = 4
EXPERTS_PER_GROUP = 8
N_EXPERTS = N_GROUPS * EXPERTS_PER_GROUP
MOE_BLOCK = 512
ALIBI_MAX_EXP = 8.0
VMEM_LIMIT_BYTES = 56 * 1024 * 1024
HIGHEST = lax.Precision.HIGHEST


def _cparams(*sem):
    return pltpu.CompilerParams(dimension_semantics=sem, vmem_limit_bytes=VMEM_LIMIT_BYTES)


def _div_pow2(v, n):
    shift = n.bit_length() - 1
    assert 1 << shift == n
    return lax.shift_right_arithmetic(v, jnp.int32(shift))


def _row_tile(n, want):
    t = min(n, want)
    assert n % t == 0
    return t


def _ada_kernel(c_ref, w_ref, b_ref, o_ref):
    c = c_ref[...]
    c_act = c * jax.nn.sigmoid(c)
    o_ref[...] = jnp.dot(c_act, w_ref[...], preferred_element_type=F32, precision=HIGHEST) + b_ref[...]


def _ada_mod(c, w_ada, b_ada):
    depth, d, d6 = w_ada.shape
    b = c.shape[0]
    rows = 8
    c_pad = jnp.zeros((rows, d), F32).at[:b].set(c)
    tn = _row_tile(d6, 1536)
    out = pl.pallas_call(
        _ada_kernel,
        out_shape=jax.ShapeDtypeStruct((depth, rows, d6), F32),
        grid=(depth, d6 // tn),
        in_specs=[
            pl.BlockSpec((rows, d), lambda l, j: (0, 0)),
            pl.BlockSpec((None, d, tn), lambda l, j: (l, 0, j)),
            pl.BlockSpec((None, 1, tn), lambda l, j: (l, 0, j)),
        ],
        out_specs=pl.BlockSpec((None, rows, tn), lambda l, j: (l, 0, j)),
        compiler_params=_cparams("parallel", "parallel"),
        name="ada_mod",
    )(c_pad, w_ada, b_ada.reshape(depth, 1, d6))
    return out[:, :b]


def _modulated_norm(x, g, sc, sh):
    ms = jnp.mean(x * x, axis=-1, keepdims=True)
    return (x * lax.rsqrt(ms + EPS) * g) * (1.0 + sc) + sh


def _inproj_kernel(x_ref, g_ref, sc_ref, sh_ref, w_ref, gn_ref, o_ref, *scratch, n_gain, head_norm, tn):
    h = _modulated_norm(x_ref[...], g_ref[...], sc_ref[...], sh_ref[...]).astype(BF16)
    n_out = o_ref.shape[-1]
    for c0 in range(0, n_out, tn):
        y = jnp.dot(h, w_ref[:, c0:c0 + tn], preferred_element_type=F32)
        if c0 < n_gain and head_norm:
            scratch[0][:, c0:c0 + tn] = y
        elif c0 < n_gain:
            o_ref[:, c0:c0 + tn] = (y * gn_ref[:, c0:c0 + tn]).astype(o_ref.dtype)
        else:
            o_ref[:, c0:c0 + tn] = y.astype(o_ref.dtype)
    if head_norm:
        r = _div_pow2(lax.broadcasted_iota(I32, (tn, tn), 0), HEAD_DIM)
        c = _div_pow2(lax.broadcasted_iota(I32, (tn, tn), 1), HEAD_DIM)
        group_ones = jnp.where(r == c, 1.0, 0.0).astype(BF16)
        for c0 in range(0, n_gain, tn):
            y = scratch[0][:, c0:c0 + tn]
            gs = jnp.dot((y * y).astype(BF16), group_ones, preferred_element_type=F32)
            y = y * lax.rsqrt(gs * (1.0 / HEAD_DIM) + EPS) * gn_ref[:, c0:c0 + tn]
            o_ref[:, c0:c0 + tn] = y.astype(o_ref.dtype)


def _inproj(x, g, sc, sh, w, gain, head_norm):
    b, s, d = x.shape
    n_out = w.shape[1]
    tm = _row_tile(s, 512)
    tn = 256
    n_gain = gain.shape[1]
    assert n_out % tn == 0 and n_gain % tn == 0
    gn = jnp.zeros((1, n_out), F32).at[:, :n_gain].set(gain)
    vec = pl.BlockSpec((None, 1, d), lambda bi, i: (bi, 0, 0))
    return pl.pallas_call(
        functools.partial(_inproj_kernel, n_gain=n_gain, head_norm=head_norm, tn=tn),
        out_shape=jax.ShapeDtypeStruct((b, s, n_out), BF16),
        grid=(b, s // tm),
        in_specs=[
            pl.BlockSpec((None, tm, d), lambda bi, i: (bi, i, 0)),
            pl.BlockSpec((1, d), lambda bi, i: (0, 0)),
            vec, vec,
            pl.BlockSpec((d, n_out), lambda bi, i: (0, 0)),
            pl.BlockSpec((1, n_out), lambda bi, i: (0, 0)),
        ],
        out_specs=pl.BlockSpec((None, tm, n_out), lambda bi, i: (bi, i, 0)),
        scratch_shapes=[pltpu.VMEM((tm, n_gain), F32)] if head_norm else [],
        compiler_params=_cparams("parallel", "parallel"),
        name="inproj",
    )(x, g.reshape(1, d), sc, sh, w.astype(BF16), gn)


def _outproj_kernel(*refs, n_in):
    ins, ws = refs[:n_in], refs[n_in:2 * n_in]
    x_ref, g_ref, o_ref = refs[2 * n_in:]
    acc = jnp.dot(ins[0][...], ws[0][...], preferred_element_type=F32)
    for a, w in zip(ins[1:], ws[1:]):
        acc += jnp.dot(a[...], w[...], preferred_element_type=F32)
    o_ref[...] = x_ref[...] + g_ref[...] * acc


def _outproj_residual(parts, weights, x, gate):
    b, s, d = x.shape
    tm = _row_tile(s, 512)
    n_in = len(parts)
    in_specs = [pl.BlockSpec((None, tm, p.shape[-1]), lambda bi, i: (bi, i, 0)) for p in parts]
    in_specs += [pl.BlockSpec(w.shape, lambda bi, i: (0, 0)) for w in weights]
    in_specs += [pl.BlockSpec((None, tm, d), lambda bi, i: (bi, i, 0)),
                 pl.BlockSpec((None, 1, d), lambda bi, i: (bi, 0, 0))]
    return pl.pallas_call(
        functools.partial(_outproj_kernel, n_in=n_in),
        out_shape=jax.ShapeDtypeStruct((b, s, d), F32),
        grid=(b, s // tm),
        in_specs=in_specs,
        out_specs=pl.BlockSpec((None, tm, d), lambda bi, i: (bi, i, 0)),
        compiler_params=_cparams("parallel", "parallel"),
        name="outproj",
    )(*parts, *[w.astype(BF16) for w in weights], x, gate)


LOG2E = 1.4426950408889634
MASKED_LOG2 = -1e30
ONES_ROWS = 16
BIAS_PAST, BIAS_DIAG0, BIAS_DIAG1 = range(3)


def _dot_nt(a, b):
    return lax.dot_general(a, b, (((1,), (1,)), ((), ())), preferred_element_type=F32)


def _sb_kernel(q_ref, k_ref, v_ref, o_ref, acc_ref, z_ref, w_ref, *, tq, tk):
    i = pl.program_id(2)
    n_heads = LANES // HEAD_DIM
    n_diag = tq // tk
    lane = lax.broadcasted_iota(I32, (1, LANES), 1)
    r = lax.broadcasted_iota(I32, (tk, tk), 0)
    c = lax.broadcasted_iota(I32, (tk, tk), 1)
    later = jnp.where(r >= c, 1.0, 0.0).astype(BF16)
    row = lax.broadcasted_iota(I32, (tq, tk), 0)
    col = lax.broadcasted_iota(I32, (tq, tk), 1)
    q = q_ref[...]
    qs = [jnp.where((lane >= hh * HEAD_DIM) & (lane < (hh + 1) * HEAD_DIM), q, jnp.zeros_like(q))
          for hh in range(n_heads)]

    def key_tile(ref, j):
        return ref[pl.ds(pl.multiple_of(j * tk, tk), tk), :]

    def weighted_values(slot, j, carry_of_tile):
        for hh in range(n_heads):
            a = jnp.exp2((w_ref[slot, hh] - carry_of_tile[hh]).astype(BF16))
            acc_ref[hh] += jnp.dot(a, key_tile(v_ref, j), preferred_element_type=F32)

    def step(slot, j, j_next, j_prev, state, causal):
        carry, carry_prev = state
        new_carry = []
        weighted_values(1 - slot, j_prev, carry_prev)
        for hh in range(n_heads):
            z = z_ref[slot, hh]
            neg_abs = pltpu.bitcast(pltpu.bitcast(z, U32) | jnp.uint32(0x80000000), F32)
            fail = jnp.maximum(z, 0.0) + jnp.log2(1.0 + jnp.exp2(neg_abs))
            if causal is not None:
                fail = jnp.where(causal, fail, 0.0)
            incl = jnp.dot(fail.astype(BF16), later, preferred_element_type=F32)
            w = z - incl
            if causal is not None:
                w = jnp.where(causal, w, MASKED_LOG2)
            w_ref[slot, hh] = w
            new_carry.append(carry[hh] + incl[:, 0:1])
        for hh in range(n_heads):
            z_ref[1 - slot, hh] = _dot_nt(qs[hh], key_tile(k_ref, j_next))
        return tuple(new_carry), carry

    assert n_diag == 2
    top = i * n_diag + 1
    acc_ref[...] = jnp.zeros_like(acc_ref)
    w_ref[1] = jnp.full(w_ref.shape[1:], MASKED_LOG2, F32)
    for hh in range(n_heads):
        z_ref[0, hh] = _dot_nt(qs[hh], key_tile(k_ref, top))
    zero = tuple(jnp.zeros((tq, 1), F32) for _ in range(n_heads))
    state = (zero, zero)
    state = step(0, top, top - 1, top, state, col + tk < row)
    state = step(1, top - 1, jnp.maximum(top - 2, 0), top, state, col < row)

    def pair(ja, state):
        state = step(0, ja, ja - 1, ja + 1, state, None)
        return step(1, ja - 1, jnp.maximum(ja - 2, 0), ja, state, None)

    state = lax.fori_loop(0, i // 2, lambda jj, st: pair(top - 4 - 4 * jj, pair(top - 2 - 4 * jj, st)), state)
    carry, carry_prev = lax.fori_loop(0, i % 2, lambda jj, st: pair(top - 2 - 4 * (i // 2), st), state)
    weighted_values(1, 0, carry_prev)
    o_ref[...] = jnp.where(lane < HEAD_DIM, acc_ref[0], acc_ref[1]).astype(o_ref.dtype)


def _stick_breaking(proj, width):
    b, s, _ = proj.shape
    tk = _row_tile(s, 256)
    tq = _row_tile(s, 2 * tk)
    nblk = width // LANES
    return pl.pallas_call(
        functools.partial(_sb_kernel, tq=tq, tk=tk),
        out_shape=jax.ShapeDtypeStruct((b, s, width), BF16),
        grid=(b, nblk, s // tq),
        in_specs=[
            pl.BlockSpec((None, tq, LANES), lambda bi, h, i: (bi, i, h)),
            pl.BlockSpec((None, s, LANES), lambda bi, h, i: (bi, 0, nblk + h)),
            pl.BlockSpec((None, s, LANES), lambda bi, h, i: (bi, 0, 2 * nblk + h)),
        ],
        out_specs=pl.BlockSpec((None, tq, LANES), lambda bi, h, i: (bi, i, h)),
        scratch_shapes=[pltpu.VMEM((LANES // HEAD_DIM, tq, LANES), F32),
                        pltpu.VMEM((2, LANES // HEAD_DIM, tq, tk), F32),
                        pltpu.VMEM((2, LANES // HEAD_DIM, tq, tk), F32)],
        compiler_params=_cparams("parallel", "parallel", "parallel"),
        name="stick_breaking",
    )(proj, proj, proj)


def _conv_kernel(a_ref, g_ref, pa_ref, pg_ref, bglu_ref, dw_ref, dwb_ref, lng_ref, lnb_ref, o_ref, hbuf, *, ts, rc):
    i = pl.program_id(1)
    w = a_ref.shape[-1]

    def glu(a, g):
        a = a.astype(F32) + bglu_ref[:, :w]
        g = g.astype(F32) + bglu_ref[:, w:]
        return a * jax.nn.sigmoid(g)

    prev = glu(pa_ref[...], pg_ref[...])
    hbuf[0, 0:CONV_HALO] = jnp.where(i > 0, prev, 0.0)
    hbuf[0, CONV_HALO:CONV_HALO + ts] = glu(a_ref[...], g_ref[...])
    off = CONV_HALO - (CONV_KERNEL - 1)
    n_phase = hbuf.shape[0]
    for p in range(1, n_phase):
        hbuf[p, 0:CONV_HALO + ts - n_phase] = hbuf[0, p:p + CONV_HALO + ts - n_phase]
    for r0 in range(0, ts, rc):
        acc = jnp.zeros((rc, w), F32) + dwb_ref[...]
        for k in range(CONV_KERNEL):
            p = (off + k) % n_phase
            start = r0 + off + k - p
            acc = acc + dw_ref[k:k + 1, :] * hbuf[p, start:start + rc, :]
        mu = jnp.mean(acc, axis=-1, keepdims=True)
        cen = acc - mu
        var = jnp.mean(cen * cen, axis=-1, keepdims=True)
        y = cen * lax.rsqrt(var + EPS) * lng_ref[...] + lnb_ref[...]
        o_ref[r0:r0 + rc, :] = (y * jax.nn.sigmoid(y)).astype(o_ref.dtype)


def _conformer_conv(proj, col0, width, b_glu, dw, dw_b, ln_g, ln_b):
    b, s, _ = proj.shape
    ts = _row_tile(s, 256)
    assert col0 % width == 0 and ts % CONV_HALO == 0
    cb = col0 // width
    hb = ts // CONV_HALO
    cur = lambda off: pl.BlockSpec((None, ts, width), lambda bi, i: (bi, i, cb + off))
    prev = lambda off: pl.BlockSpec((None, CONV_HALO, width),
                                    lambda bi, i: (bi, jnp.maximum(i * hb - 1, 0), cb + off))
    row = lambda n: pl.BlockSpec((1, n), lambda bi, i: (0, 0))
    return pl.pallas_call(
        functools.partial(_conv_kernel, ts=ts, rc=32),
        out_shape=jax.ShapeDtypeStruct((b, s, width), BF16),
        grid=(b, s // ts),
        in_specs=[cur(0), cur(1), prev(0), prev(1), row(2 * width),
                  pl.BlockSpec((CONV_KERNEL, width), lambda bi, i: (0, 0)),
                  row(width), row(width), row(width)],
        out_specs=pl.BlockSpec((None, ts, width), lambda bi, i: (bi, i, 0)),
        scratch_shapes=[pltpu.VMEM((8, CONV_HALO + ts, width), F32)],
        compiler_params=_cparams("parallel", "parallel"),
        name="conformer_conv",
    )(proj, proj, proj, proj, b_glu.reshape(1, -1), dw, dw_b.reshape(1, -1),
      ln_g.reshape(1, -1), ln_b.reshape(1, -1))


def _diff_kernel(slope_ref, q_ref, k_ref, vt_ref, bias_ref, lq1_ref, lk1_ref, lq2_ref, lk2_ref, subg_ref, o_ref,
                 acc_ref, s_ref, p_ref, *, tq, tk, lam_init):
    h = pl.program_id(1)
    i = pl.program_id(2)
    n_diag = tq // tk
    slope = slope_ref[h]
    lane = lax.broadcasted_iota(I32, (1, LANES), 1)
    q = q_ref[...]
    zero = jnp.zeros_like(q)
    qs = (jnp.where(lane < HEAD_DIM, q, zero), jnp.where(lane >= HEAD_DIM, q, zero))

    def scores(slot, j, bias):
        kt = k_ref[pl.ds(pl.multiple_of(j * tk, tk), tk), :]
        tile_max = []
        for n in range(2):
            s = _dot_nt(kt, qs[n]) + bias
            s_ref[slot, n] = s
            tile_max.append(jnp.max(s, axis=0, keepdims=True))
        return tuple(tile_max)

    def weighted_values(slot, j, alpha):
        vt = vt_ref[:, pl.ds(pl.multiple_of(j * tk, tk), tk)]
        for n in range(2):
            acc_ref[n] = alpha[n] * acc_ref[n] + jnp.dot(vt, p_ref[slot, n], preferred_element_type=F32)

    def step(slot, j_next, j_prev, state, bias_next, shift):
        stats, alpha_prev, tile_max = state
        weighted_values(1 - slot, j_prev, alpha_prev)
        new_stats, alphas = [], []
        for n in range(2):
            m_new = jnp.maximum(stats[n], tile_max[n] + shift)
            alphas.append(jnp.exp2(stats[n] - m_new))
            p_ref[slot, n] = jnp.exp2((s_ref[slot, n] - (m_new - shift)).astype(BF16))
            new_stats.append(m_new)
        return tuple(new_stats), tuple(alphas), scores(1 - slot, j_next, bias_next)

    def shift_of(j):
        return -slope * ((i * n_diag - j) * tk).astype(F32)

    assert n_diag == 2
    top = i * n_diag + 1
    acc_ref[...] = jnp.zeros_like(acc_ref)
    p_ref[1] = jnp.zeros(p_ref.shape[1:], BF16)
    floor = jnp.full((1, tq), MASKED_LOG2, F32)
    one = jnp.ones((1, tq), F32)
    state = ((floor, floor), (one, one), scores(0, top, bias_ref[BIAS_DIAG1]))
    state = step(0, top - 1, top, state, bias_ref[BIAS_DIAG0], 0.0)
    state = step(1, jnp.maximum(top - 2, 0), top, state, bias_ref[BIAS_PAST], 0.0)

    def pair(ja, state):
        state = step(0, ja - 1, ja + 1, state, bias_ref[BIAS_PAST], shift_of(ja))
        return step(1, jnp.maximum(ja - 2, 0), ja, state, bias_ref[BIAS_PAST], shift_of(ja - 1))

    state = lax.fori_loop(0, i // 2, lambda jj, st: pair(top - 4 - 4 * jj, pair(top - 2 - 4 * jj, st)), state)
    _, alpha_last, _ = lax.fori_loop(0, i % 2, lambda jj, st: pair(top - 2 - 4 * (i // 2), st), state)
    weighted_values(1, 0, alpha_last)
    dv = acc_ref.shape[1] - ONES_ROWS
    l1, l2 = acc_ref[0, dv:dv + 1, :], acc_ref[1, dv:dv + 1, :]
    lam = (jnp.exp(jnp.sum(lq1_ref[...] * lk1_ref[...], axis=-1, keepdims=True))
           - jnp.exp(jnp.sum(lq2_ref[...] * lk2_ref[...], axis=-1, keepdims=True)) + lam_init)
    o = acc_ref[0, :dv, :] / l1 - lam * (acc_ref[1, :dv, :] / l2)
    ms = jnp.mean(o * o, axis=0, keepdims=True)
    o = o * lax.rsqrt(ms + EPS) * (1.0 - lam_init)
    o_ref[...] = (o.T * subg_ref[...]).astype(o_ref.dtype)


def _diff_attention(proj, n_heads, lq1, lk1, lq2, lk2, sub_g, lam_init):
    b, s, _ = proj.shape
    tk = _row_tile(s, 256)
    tq = _row_tile(s, 2 * tk)
    assert tk % CHUNK == 0
    v_t = jnp.swapaxes(proj[:, :, 2 * n_heads * LANES:], 1, 2).reshape(b, n_heads, LANES, s)
    v_t = jnp.concatenate([v_t, jnp.ones((b, n_heads, ONES_ROWS, s), BF16)], axis=2)
    slopes = LOG2E * jnp.exp2(-ALIBI_MAX_EXP * jnp.arange(1, n_heads + 1, dtype=F32) / n_heads)
    key = jnp.arange(tk, dtype=I32)[:, None]
    qry = jnp.arange(tq, dtype=I32)[None, :]
    rel = (qry - key).astype(F32)

    def diag(dj):
        allowed = (key + dj * tk) // CHUNK <= qry // CHUNK
        return jnp.where(allowed, -jnp.abs(rel - float(dj * tk)), -jnp.inf)

    bias = slopes[:, None, None, None] * jnp.stack([-rel, diag(0), diag(1)])
    vec = lambda n: pl.BlockSpec((1, n), lambda bi, h, i, sl: (0, 0))
    return pl.pallas_call(
        functools.partial(_diff_kernel, tq=tq, tk=tk, lam_init=lam_init),
        out_shape=jax.ShapeDtypeStruct((b, s, n_heads * LANES), BF16),
        grid_spec=pltpu.PrefetchScalarGridSpec(
            num_scalar_prefetch=1,
            grid=(b, n_heads, s // tq),
            in_specs=[
                pl.BlockSpec((None, tq, LANES), lambda bi, h, i, sl: (bi, i, h)),
                pl.BlockSpec((None, s, LANES), lambda bi, h, i, sl: (bi, 0, n_heads + h)),
                pl.BlockSpec((None, None, LANES + ONES_ROWS, s), lambda bi, h, i, sl: (bi, h, 0, 0)),
                pl.BlockSpec((None, 3, tk, tq), lambda bi, h, i, sl: (h, 0, 0, 0)),
                vec(HEAD_DIM), vec(HEAD_DIM), vec(HEAD_DIM), vec(HEAD_DIM), vec(LANES),
            ],
            out_specs=pl.BlockSpec((None, tq, LANES), lambda bi, h, i, sl: (bi, i, h)),
            scratch_shapes=[pltpu.VMEM((2, LANES + ONES_ROWS, tq), F32), pltpu.VMEM((2, 2, tk, tq), F32),
                            pltpu.VMEM((2, 2, tk, tq), BF16)],
        ),
        compiler_params=_cparams("parallel", "parallel", "parallel"),
        name="diff_attention",
    )(slopes, proj, proj, v_t, bias, lq1.reshape(1, -1), lk1.reshape(1, -1), lq2.reshape(1, -1),
      lk2.reshape(1, -1), sub_g.reshape(1, -1))


ROUTE_E0, ROUTE_E1, ROUTE_W0, ROUTE_W1, ROUTE_R0, ROUTE_R1 = range(6)
DMA_ISSUE_UNROLL = 8


def _pack_bf16_pair(lo, hi):
    lo_bits = pltpu.bitcast(lo.astype(BF16).astype(F32), U32) >> 16
    hi_bits = pltpu.bitcast(hi.astype(BF16).astype(F32), U32) & jnp.uint32(0xFFFF0000)
    return hi_bits | lo_bits


def _unpack_bf16_pair(u):
    lo = pltpu.bitcast(u << 16, F32).astype(BF16)
    hi = pltpu.bitcast(u & jnp.uint32(0xFFFF0000), F32).astype(BF16)
    return lo, hi


def _router_kernel(x_ref, g_ref, sc_ref, sh_ref, wr_ref, br_ref, hp_ref, rt_ref, cnt_ref, run_ref, *, tm):
    first = (pl.program_id(0) == 0) & (pl.program_id(1) == 0)

    @pl.when(first)
    def _():
        run_ref[...] = jnp.zeros_like(run_ref)

    h = _modulated_norm(x_ref[...], g_ref[...], sc_ref[...], sh_ref[...])
    half = h.shape[-1] // 2
    hp_ref[...] = _pack_bf16_pair(h[:, :half], h[:, half:])

    logits = jnp.dot(h, wr_ref[...], preferred_element_type=F32, precision=HIGHEST) + br_ref[...]
    lane = lax.broadcasted_iota(I32, (tm, LANES), 1)
    lanef = lane.astype(F32)
    ninf = -jnp.inf
    big = float(LANES)

    def first_argmax(vals):
        top = jnp.max(vals, axis=-1, keepdims=True)
        idx = jnp.min(jnp.where(vals == top, lanef, big), axis=-1, keepdims=True)
        return top, idx

    is_group = lane < N_GROUPS
    gl = jnp.where(is_group, logits, ninf)
    gmax, gidx = first_argmax(gl)
    g_w = 1.0 / jnp.sum(jnp.where(is_group, jnp.exp(gl - gmax), 0.0), axis=-1, keepdims=True)
    lane_group = _div_pow2(lane - N_GROUPS, EXPERTS_PER_GROUP).astype(F32)
    in_group = (lane >= N_GROUPS) & (lane < N_GROUPS + N_EXPERTS) & (lane_group == gidx)
    el = jnp.where(in_group, logits, ninf)
    v0, i0 = first_argmax(el)
    v1, i1 = first_argmax(jnp.where(lanef == i0, ninf, el))
    tt = jnp.exp(v1 - v0)
    w0 = g_w / (1.0 + tt)
    w1 = g_w * tt / (1.0 + tt)
    e0 = i0 - N_GROUPS
    e1 = i1 - N_GROUPS

    member = (lanef == e0) | (lanef == e1)
    r = lax.broadcasted_iota(I32, (tm, tm), 0)
    c = lax.broadcasted_iota(I32, (tm, tm), 1)
    earlier = jnp.where(c < r, 1.0, 0.0).astype(BF16)
    before = jnp.dot(earlier, jnp.where(member, 1.0, 0.0).astype(BF16), preferred_element_type=F32)
    before = before + run_ref[...]
    r0 = jnp.sum(jnp.where(lanef == e0, before, 0.0), axis=-1, keepdims=True)
    r1 = jnp.sum(jnp.where(lanef == e1, before, 0.0), axis=-1, keepdims=True)
    run_ref[...] += jnp.sum(jnp.where(member, 1.0, 0.0), axis=0, keepdims=True)
    cnt_ref[...] = run_ref[...]

    slab = jnp.zeros((tm, LANES), F32)
    for colv, val in ((ROUTE_E0, e0), (ROUTE_E1, e1), (ROUTE_W0, w0), (ROUTE_W1, w1), (ROUTE_R0, r0), (ROUTE_R1, r1)):
        slab = jnp.where(lane == colv, val, slab)
    rt_ref[...] = slab


def _router(x, g, sc, sh, w_group, b_group, w_router, b_router):
    b, s, d = x.shape
    tm = _row_tile(s, 1024)
    wr = jnp.zeros((d, LANES), F32).at[:, :N_GROUPS].set(w_group).at[:, N_GROUPS:N_GROUPS + N_EXPERTS].set(w_router)
    br = jnp.zeros((1, LANES), F32).at[0, :N_GROUPS].set(b_group).at[0, N_GROUPS:N_GROUPS + N_EXPERTS].set(b_router)
    vec = pl.BlockSpec((None, 1, d), lambda bi, i: (bi, 0, 0))
    return pl.pallas_call(
        functools.partial(_router_kernel, tm=tm),
        out_shape=(jax.ShapeDtypeStruct((b, s, d // 2), U32),
                   jax.ShapeDtypeStruct((b, s, LANES), F32),
                   jax.ShapeDtypeStruct((1, LANES), F32)),
        grid=(b, s // tm),
        in_specs=[
            pl.BlockSpec((None, tm, d), lambda bi, i: (bi, i, 0)),
            pl.BlockSpec((1, d), lambda bi, i: (0, 0)),
            vec, vec,
            pl.BlockSpec((d, LANES), lambda bi, i: (0, 0)),
            pl.BlockSpec((1, LANES), lambda bi, i: (0, 0)),
        ],
        out_specs=(pl.BlockSpec((None, tm, d // 2), lambda bi, i: (bi, i, 0)),
                   pl.BlockSpec((None, tm, LANES), lambda bi, i: (bi, i, 0)),
                   pl.BlockSpec((1, LANES), lambda bi, i: (0, 0))),
        scratch_shapes=[pltpu.VMEM((1, LANES), F32)],
        compiler_params=_cparams("arbitrary", "arbitrary"),
        name="moe_router",
    )(x, g.reshape(1, d), sc, sh, wr, br)


def _zero_kernel(o_ref):
    o_ref[...] = jnp.zeros_like(o_ref)


def _zeros_u32(rows, cols):
    tr = _row_tile(rows, 2048)
    return pl.pallas_call(
        _zero_kernel,
        out_shape=jax.ShapeDtypeStruct((rows, cols), U32),
        grid=(rows // tr,),
        out_specs=pl.BlockSpec((tr, cols), lambda i: (i, 0)),
        compiler_params=_cparams("parallel"),
        name="moe_zero_slots",
    )()


def _dispatch_kernel(dest_ref, h_ref, xs_in_ref, xs_ref, sem, *, chunk):
    del xs_in_ref
    base = pl.program_id(0) * chunk

    def body(r, carry):
        for k in range(2):
            pltpu.make_async_copy(h_ref.at[pl.ds(r, 1), :],
                                  xs_ref.at[pl.ds(dest_ref[2 * (base + r) + k], 1), :], sem).start()
        return carry

    lax.fori_loop(0, chunk, body, 0, unroll=DMA_ISSUE_UNROLL)
    for _ in range(2):
        pltpu.make_async_copy(h_ref, xs_ref.at[pl.ds(0, chunk), :], sem).wait()


def _dispatch(dest, h_packed, slots):
    n, w = h_packed.shape
    chunk = _row_tile(n, 1024)
    return pl.pallas_call(
        functools.partial(_dispatch_kernel, chunk=chunk),
        out_shape=jax.ShapeDtypeStruct(slots.shape, U32),
        grid_spec=pltpu.PrefetchScalarGridSpec(
            num_scalar_prefetch=1,
            grid=(n // chunk,),
            in_specs=[pl.BlockSpec((chunk, w), lambda i, d: (i, 0)), pl.BlockSpec(memory_space=pl.ANY)],
            out_specs=pl.BlockSpec(memory_space=pl.ANY),
            scratch_shapes=[pltpu.SemaphoreType.DMA(())],
        ),
        input_output_aliases={2: 0},
        compiler_params=_cparams("arbitrary"),
        name="moe_dispatch",
    )(dest, h_packed, slots)


def _expert_kernel(be_ref, nv_ref, xs_ref, wg_ref, wu_ref, wd_ref, ys_ref, wg_b, wu_b, wd_b):
    blk = pl.program_id(0)
    valid = blk < nv_ref[0]
    changed = (blk == 0) | (be_ref[blk] != be_ref[jnp.maximum(blk - 1, 0)])

    @pl.when(valid & changed)
    def _():
        wg_b[...] = wg_ref[...].astype(BF16)
        wu_b[...] = wu_ref[...].astype(BF16)
        wd_b[...] = wd_ref[...].astype(BF16)

    @pl.when(valid)
    def _():
        x_lo, x_hi = _unpack_bf16_pair(xs_ref[...])
        half = x_lo.shape[-1]

        def proj(w):
            return (jnp.dot(x_lo, w[:half, :], preferred_element_type=F32)
                    + jnp.dot(x_hi, w[half:, :], preferred_element_type=F32))

        gate = proj(wg_b)
        hid = (gate * jax.nn.sigmoid(gate)) * proj(wu_b)
        y = jnp.dot(hid.astype(BF16), wd_b[...], preferred_element_type=F32)
        ys_ref[...] = _pack_bf16_pair(y[:, :half], y[:, half:])

    @pl.when(jnp.logical_not(valid))
    def _():
        ys_ref[...] = jnp.zeros_like(ys_ref)


def _experts(block_e, n_valid, xs, w_gate, w_up, w_down, layer):
    n_slots, half = xs.shape
    d = 2 * half
    de = w_gate.shape[-1]
    n_blocks = n_slots // MOE_BLOCK
    last = lambda blk, nv: jnp.minimum(blk, nv[0] - 1)
    w_in = pl.BlockSpec((None, None, d, de), lambda blk, be, nv: (layer, be[last(blk, nv)], 0, 0))
    return pl.pallas_call(
        _expert_kernel,
        out_shape=jax.ShapeDtypeStruct((n_slots, half), U32),
        grid_spec=pltpu.PrefetchScalarGridSpec(
            num_scalar_prefetch=2,
            grid=(n_blocks,),
            in_specs=[pl.BlockSpec((MOE_BLOCK, half), lambda blk, be, nv: (last(blk, nv), 0)),
                      w_in, w_in,
                      pl.BlockSpec((None, None, de, d), lambda blk, be, nv: (layer, be[last(blk, nv)], 0, 0))],
            out_specs=pl.BlockSpec((MOE_BLOCK, half), lambda blk, be, nv: (blk, 0)),
            scratch_shapes=[pltpu.VMEM((d, de), BF16), pltpu.VMEM((d, de), BF16), pltpu.VMEM((de, d), BF16)],
        ),
        compiler_params=_cparams("arbitrary"),
        name="moe_experts",
    )(block_e, n_valid, xs, w_gate, w_up, w_down)


def _combine_kernel(dest_ref, x_ref, rt_ref, g_ref, ys_ref, o_ref, buf, sems, *, tm):
    i = pl.program_id(0)
    cur = lax.rem(i, 2)

    def gather(tile, half):
        base = tile * tm

        def body(r, carry):
            for k in range(2):
                pltpu.make_async_copy(ys_ref.at[pl.ds(dest_ref[2 * (base + r) + k], 1), :],
                                      buf.at[half, k, pl.ds(r, 1), :], sems.at[half]).start()
            return carry

        lax.fori_loop(0, tm, body, 0, unroll=DMA_ISSUE_UNROLL)

    @pl.when(i == 0)
    def _():
        gather(0, 0)

    @pl.when(i + 1 < pl.num_programs(0))
    def _():
        gather(i + 1, 1 - cur)

    for k in range(2):
        pltpu.make_async_copy(ys_ref.at[pl.ds(0, tm), :], buf.at[cur, k], sems.at[cur]).wait()
    rt = rt_ref[...]
    w0 = rt[:, ROUTE_W0:ROUTE_W0 + 1]
    w1 = rt[:, ROUTE_W1:ROUTE_W1 + 1]
    y0, y1 = buf[cur, 0], buf[cur, 1]
    half = y0.shape[-1]
    mask = jnp.uint32(0xFFFF0000)
    lo = w0 * pltpu.bitcast(y0 << 16, F32) + w1 * pltpu.bitcast(y1 << 16, F32)
    hi = w0 * pltpu.bitcast(y0 & mask, F32) + w1 * pltpu.bitcast(y1 & mask, F32)
    o_ref[:, :half] = x_ref[:, :half] + g_ref[:, :half] * lo
    o_ref[:, half:] = x_ref[:, half:] + g_ref[:, half:] * hi


def _combine(dest, x2, route, gate, ys, tiles_per_batch_of):
    n, d = x2.shape
    tm = _row_tile(n, 512)
    per_batch = tiles_per_batch_of(tm)
    return pl.pallas_call(
        functools.partial(_combine_kernel, tm=tm),
        out_shape=jax.ShapeDtypeStruct((n, d), F32),
        grid_spec=pltpu.PrefetchScalarGridSpec(
            num_scalar_prefetch=1,
            grid=(n // tm,),
            in_specs=[pl.BlockSpec((tm, d), lambda i, de: (i, 0)),
                      pl.BlockSpec((tm, LANES), lambda i, de: (i, 0)),
                      pl.BlockSpec((None, 1, d), lambda i, de: (i // per_batch, 0, 0)),
                      pl.BlockSpec(memory_space=pl.ANY)],
            out_specs=pl.BlockSpec((tm, d), lambda i, de: (i, 0)),
            scratch_shapes=[pltpu.VMEM((2, 2, tm, d // 2), U32), pltpu.SemaphoreType.DMA((2,))],
        ),
        compiler_params=_cparams("arbitrary"),
        name="moe_combine",
    )(dest, x2, route, gate, ys)


def _moe_slot_count(n_tokens):
    return (-(-(n_tokens * 2) // MOE_BLOCK) + N_EXPERTS) * MOE_BLOCK


def _moe_layer(x, slots, g, sc, sh, gate, w_group, b_group, w_router, b_router, w_gate, w_up, w_down, layer):
    b, s, d = x.shape
    n = b * s
    h_packed, route, counts = _router(x, g, sc, sh, w_group, b_group, w_router, b_router)
    route = route.reshape(n, LANES)
    counts = counts[0, :N_EXPERTS].astype(I32)
    padded = (counts + MOE_BLOCK - 1) // MOE_BLOCK * MOE_BLOCK
    pad_end = jnp.cumsum(padded)
    pad_start = pad_end - padded
    e_id = route[:, ROUTE_E0:ROUTE_E1 + 1].astype(I32)
    rank = route[:, ROUTE_R0:ROUTE_R1 + 1].astype(I32)
    is_e = e_id[:, :, None] == jnp.arange(N_EXPERTS, dtype=I32)
    dest = (jnp.sum(jnp.where(is_e, pad_start, 0), axis=-1) + rank).reshape(-1)
    n_blocks = slots.shape[0] // MOE_BLOCK
    block_start = jnp.arange(n_blocks, dtype=I32) * MOE_BLOCK
    block_e = jnp.minimum(jnp.sum((pad_end[None, :] <= block_start[:, None]).astype(I32), axis=1), N_EXPERTS - 1)
    n_valid = (pad_end[-1:] // MOE_BLOCK).astype(I32)
    xs = _dispatch(dest, h_packed.reshape(n, d // 2), slots)
    ys = _experts(block_e, n_valid, xs, w_gate, w_up, w_down, layer)
    out = _combine(dest, x.reshape(n, d), route, gate, ys, lambda tm: s // tm)
    return out.reshape(b, s, d), xs


def kernel(x, c, w_ada, b_ada, mix_norm_g, ffn_norm_g, even_w_in, conv_b_glu, conv_dw, conv_dw_b, conv_ln_g, conv_ln_b, even_w_out, odd_w_in, qk_norm_q, qk_norm_k, lambda_q1, lambda_k1, lambda_q2, lambda_k2, diff_sub_g, odd_w_out, moe_w_group, moe_b_group, moe_w_router, moe_b_router, moe_w_gate, moe_w_up, moe_w_down):
    depth = w_ada.shape[0]
    d = x.shape[-1]
    sb_width = d // 2
    conv_width = d // 2
    diff_heads = d // (2 * HEAD_DIM)
    qk_width = diff_heads * 2 * HEAD_DIM
    mod = _ada_mod(c, w_ada, b_ada)
    slots = _zeros_u32(_moe_slot_count(x.shape[0] * x.shape[1]), d // 2)
    for layer in range(depth):
        sh1, sc1, g1, sh2, sc2, g2 = [m[:, None, :] for m in jnp.split(mod[layer], 6, axis=-1)]
        j = layer // 2
        if layer % 2 == 0:
            q_gain = jnp.full((1, sb_width), LOG2E * HEAD_DIM ** -0.5, F32)
            proj = _inproj(x, mix_norm_g[layer], sc1, sh1, even_w_in[j], q_gain, head_norm=False)
            a_out = _stick_breaking(proj, sb_width)
            b_out = _conformer_conv(proj, 3 * sb_width, conv_width, conv_b_glu[j], conv_dw[j], conv_dw_b[j],
                                    conv_ln_g[j], conv_ln_b[j])
            x = _outproj_residual([a_out, b_out], [even_w_out[j][:sb_width], even_w_out[j][sb_width:]], x, g1)
        else:
            lam_init = 0.8 - 0.6 * math.exp(-0.3 * layer)
            reps = qk_width // HEAD_DIM
            qk_gain = jnp.concatenate([jnp.tile(qk_norm_q[j] * (LOG2E * HEAD_DIM ** -0.5), reps),
                                       jnp.tile(qk_norm_k[j], reps)])[None, :]
            proj = _inproj(x, mix_norm_g[layer], sc1, sh1, odd_w_in[j], qk_gain, head_norm=True)
            o = _diff_attention(proj, diff_heads, lambda_q1[j], lambda_k1[j], lambda_q2[j], lambda_k2[j],
                                diff_sub_g[j], lam_init)
            x = _outproj_residual([o], [odd_w_out[j]], x, g1)
        x, slots = _moe_layer(x, slots, ffn_norm_g[layer], sc2, sh2, g2, moe_w_group[layer], moe_b_group[layer],
                              moe_w_router[layer], moe_b_router[layer], moe_w_gate, moe_w_up, moe_w_down, layer)
    return x
```
